```python
import jax, jax.numpy as jnp
from jax import lax
import numpy as np

D_MODEL = 1024
BATCH = 2
SEQ = 8192
DEPTH = 1
DEC_BATCH = 32
DEC_SEQ = 1
PAST_LEN = 8192
PAGE_SIZE = 128

N_HEADS = 16
HEAD_DIM = 64
N_KV = 4
HPG = N_HEADS // N_KV
KV_W = N_KV * HEAD_DIM
ROT_DIM = HEAD_DIM // 4
ROPE_THETA = 500000.0
L_CMP = 32
D_CMP = 16
L_SLC = 64
N_SEL = 16
N_LOCAL = 2
WINDOW = 512
CMP_HID = 4 * HEAD_DIM
Q_BLOCK = 128
C_CONV = D_MODEL
CONV_K = 31
D_FF = 2816
FFN_K = 3
DN_ALPHA = (2 * DEPTH) ** 0.25
DN_BETA = (8 * DEPTH) ** -0.25
LN_EPS = 1e-5
NEG_INF = -1e30
N_IN = 2 * C_CONV + N_HEADS * HEAD_DIM + 6 * KV_W + 3 * N_HEADS + 2 * D_MODEL

kernel_name = 'nsa_conformer_convffn_hybrid'


def layer_norm(x, g, b):
    xf = x.astype(jnp.float32)
    mu = jnp.mean(xf, axis=-1, keepdims=True)
    var = jnp.mean(jnp.square(xf - mu), axis=-1, keepdims=True)
    return ((xf - mu) * lax.rsqrt(var + LN_EPS) * g + b).astype(x.dtype)


def partial_rope(x, pos):
    half = ROT_DIM // 2
    inv = ROPE_THETA ** (-jnp.arange(half, dtype=jnp.float32) * 2.0 / ROT_DIM)
    ang = pos.astype(jnp.float32)[:, None] * inv[None, :]
    cos = jnp.cos(ang)[None, :, None, :]
    sin = jnp.sin(ang)[None, :, None, :]
    xr = x[..., :ROT_DIM].astype(jnp.float32)
    x1, x2 = xr[..., :half], xr[..., half:]
    rot = jnp.concatenate([x1 * cos - x2 * sin, x1 * sin + x2 * cos], axis=-1).astype(x.dtype)
    return jnp.concatenate([rot, x[..., ROT_DIM:]], axis=-1)


def dwconv_valid(x, w):
    c = w.shape[1]
    return lax.conv_general_dilated(x, w[:, None, :].astype(x.dtype), (1,), 'VALID',
                                    dimension_numbers=('NWC', 'WIO', 'NWC'), feature_group_count=c)


def project_inputs(x, pos, w_in):
    n, t, _ = x.shape
    z = x @ w_in
    o1 = 2 * C_CONV
    o2 = o1 + N_HEADS * HEAD_DIM
    o3 = o2 + 6 * KV_W
    o4 = o3 + 3 * N_HEADS
    u = z[..., :C_CONV] * jax.nn.sigmoid(z[..., C_CONV:o1])
    q = partial_rope(z[..., o1:o2].reshape(n, t, N_HEADS, HEAD_DIM), pos)
    kv = z[..., o2:o3].reshape(n, t, 3, 2, N_KV, HEAD_DIM)
    k = partial_rope(kv[:, :, :, 0].reshape(n, t, 3 * N_KV, HEAD_DIM), pos).reshape(n, t, 3, N_KV, HEAD_DIM)
    kv = jnp.stack([k, kv[:, :, :, 1]], axis=3)
    g_br = jax.nn.sigmoid(z[..., o3:o4].reshape(n, t, 3, N_HEADS))
    g_mix = jax.nn.sigmoid(z[..., o4:].reshape(n, t, 2, D_MODEL))
    return u, q, kv[:, :, 0], kv[:, :, 1], kv[:, :, 2], g_br, g_mix


def conv_branch(u_hist, conv_dw_w, conv_dw_b, conv_ln_g, conv_ln_b, conv_w_pw):
    h = dwconv_valid(u_hist, conv_dw_w) + conv_dw_b
    h = jax.nn.silu(layer_norm(h, conv_ln_g, conv_ln_b))
    return h @ conv_w_pw


def compress(kv, cmp_pe, k_w1, k_w2, v_w1, v_w2):
    n, t = kv.shape[:2]

    def phi(x, w1, w2):
        xs = jnp.swapaxes(x, 1, 2).reshape(n * N_KV, t, HEAD_DIM)
        h = lax.conv_general_dilated(xs, w1.astype(xs.dtype), (D_CMP,), 'VALID',
                                     dimension_numbers=('NWC', 'WIO', 'NWC'))
        h = h + jnp.einsum('ld,ldh->h', cmp_pe, w1)
        h = jax.nn.gelu(h) @ w2
        return jnp.swapaxes(h.reshape(n, N_KV, -1, HEAD_DIM), 1, 2)

    return phi(kv[:, :, 0], k_w1, k_w2), phi(kv[:, :, 1], v_w1, v_w2)


def to_blocks(kv):
    n, t = kv.shape[:2]
    nb = -(-t // L_SLC)
    kv = jnp.pad(kv, ((0, 0), (0, nb * L_SLC - t), (0, 0), (0, 0), (0, 0)))
    kvb = kv.reshape(n, nb, L_SLC, 2, N_KV, HEAD_DIM).transpose(3, 0, 4, 1, 2, 5)
    return kvb[0], kvb[1]


def cmp_to_slc(nc, nb):
    start = jnp.arange(nc) * D_CMP
    bstart = jnp.arange(nb) * L_SLC
    ov = (start[:, None] < bstart[None, :] + L_SLC) & (start[:, None] + L_CMP > bstart[None, :])
    return ov.astype(jnp.float32)


def nsa_attend(q, q_pos, kc, vc, ks_blk, vs_blk, kw, vw, kw_pos, g_br):
    n, sq = q.shape[:2]
    f32 = jnp.float32
    qg = q.astype(f32).reshape(n, sq, N_KV, HPG, HEAD_DIM) * (HEAD_DIM ** -0.5)
    t = q_pos[:, None]
    nc = kc.shape[1]
    c_end = jnp.arange(nc) * D_CMP + (L_CMP - 1)
    c_vis = (c_end[None, :] <= t)[None, :, None, None, :]
    s_c = jnp.einsum('nsghd,ncgd->nsghc', qg, kc.astype(f32))
    p_c = jax.nn.softmax(jnp.where(c_vis, s_c, NEG_INF), axis=-1) * c_vis
    o_c = jnp.einsum('nsghc,ncgd->nsghd', p_c, vc.astype(f32))
    nb = ks_blk.shape[2]
    imp = jnp.einsum('nsgc,cb->nsgb', p_c.sum(axis=3), cmp_to_slc(nc, nb))
    blk = jnp.arange(nb)[None, :]
    cur = (q_pos // L_SLC)[:, None]
    b_vis = blk <= cur
    forced = (blk == 0) | ((blk <= cur) & (blk > cur - N_LOCAL))
    imp = jnp.where(forced[None, :, None, :], jnp.inf, imp)
    imp = jnp.where(b_vis[None, :, None, :], imp, -jnp.inf)
    top_v, top_i = lax.top_k(imp, min(N_SEL, nb))
    top_i = top_i.transpose(0, 2, 1, 3)
    sel_ok = (top_v > -jnp.inf).transpose(0, 2, 1, 3)
    ni = jnp.arange(n)[:, None, None, None]
    gi = jnp.arange(N_KV)[None, :, None, None]
    k_sel = ks_blk[ni, gi, top_i].astype(f32)
    v_sel = vs_blk[ni, gi, top_i].astype(f32)
    k_pos = top_i[..., None] * L_SLC + jnp.arange(L_SLC)
    s_ok = sel_ok[..., None] & (k_pos <= q_pos[None, None, :, None, None])
    s_s = jnp.einsum('nsghd,ngsjld->ngshjl', qg, k_sel)
    s_s = jnp.where(s_ok[:, :, :, None], s_s, NEG_INF)
    shp = s_s.shape
    p_s = jax.nn.softmax(s_s.reshape(shp[0], shp[1], shp[2], shp[3], -1), axis=-1).reshape(shp)
    o_s = jnp.einsum('ngshjl,ngsjld->nsghd', p_s, v_sel)
    w_ok = ((kw_pos[None, :] <= t) & (kw_pos[None, :] > t - WINDOW) & (kw_pos[None, :] >= 0))[None, :, None, None, :]
    s_w = jnp.einsum('nsghd,nkgd->nsghk', qg, kw.astype(f32))
    p_w = jax.nn.softmax(jnp.where(w_ok, s_w, NEG_INF), axis=-1)
    o_w = jnp.einsum('nsghk,nkgd->nsghd', p_w, vw.astype(f32))
    g = g_br.astype(f32).reshape(n, sq, 3, N_KV, HPG, 1)
    o = g[:, :, 0] * o_c + g[:, :, 1] * o_s + g[:, :, 2] * o_w
    return o.reshape(n, sq, N_HEADS * HEAD_DIM).astype(q.dtype)


def mix_out(x, y_conv, o, g_mix, nsa_w_o, w_out, ln1_g, ln1_b):
    merged = g_mix[:, :, 0] * y_conv + g_mix[:, :, 1] * (o @ nsa_w_o)
    return layer_norm(DN_ALPHA * x + merged @ w_out, ln1_g, ln1_b)


def ffn_out(h, up_hist, ffn_dw_w, ffn_w_down, ln2_g, ln2_b):
    a = dwconv_valid(up_hist, ffn_dw_w)
    val, gate = a[..., :D_FF], a[..., D_FF:]
    return layer_norm(DN_ALPHA * h + (jax.nn.silu(gate) * val) @ ffn_w_down, ln2_g, ln2_b)


def prompt_layer(x, w_in, cmp_pe, cmp_k_w1, cmp_k_w2, cmp_v_w1, cmp_v_w2, conv_dw_w, conv_dw_b,
                 conv_ln_g, conv_ln_b, conv_w_pw, nsa_w_o, w_out, ln1_g, ln1_b,
                 ffn_w_up, ffn_dw_w, ffn_w_down, ln2_g, ln2_b):
    n, s, _ = x.shape
    pos = jnp.arange(s)
    u, q, cmp_kv, slc_kv, win_kv, g_br, g_mix = project_inputs(x, pos, w_in)
    u_hist = jnp.pad(u, ((0, 0), (CONV_K - 1, 0), (0, 0)))
    y_conv = conv_branch(u_hist, conv_dw_w, conv_dw_b, conv_ln_g, conv_ln_b, conv_w_pw)
    kc, vc = compress(cmp_kv, cmp_pe, cmp_k_w1, cmp_k_w2, cmp_v_w1, cmp_v_w2)
    ks_blk, vs_blk = to_blocks(slc_kv)
    w_pad = jnp.pad(win_kv, ((0, 0), (WINDOW, 0), (0, 0), (0, 0), (0, 0)))

    def attend_block(start):
        qb = lax.dynamic_slice_in_dim(q, start, Q_BLOCK, axis=1)
        gb = lax.dynamic_slice_in_dim(g_br, start, Q_BLOCK, axis=1)
        wb = lax.dynamic_slice_in_dim(w_pad, start, WINDOW + Q_BLOCK, axis=1)
        q_pos = start + jnp.arange(Q_BLOCK)
        kw_pos = start - WINDOW + jnp.arange(WINDOW + Q_BLOCK)
        return nsa_attend(qb, q_pos, kc, vc, ks_blk, vs_blk, wb[:, :, 0], wb[:, :, 1], kw_pos, gb)

    o = lax.map(attend_block, jnp.arange(s // Q_BLOCK) * Q_BLOCK)
    o = jnp.swapaxes(o, 0, 1).reshape(n, s, N_HEADS * HEAD_DIM)
    h = mix_out(x, y_conv, o, g_mix, nsa_w_o, w_out, ln1_g, ln1_b)
    up_hist = jnp.pad(h @ ffn_w_up, ((0, 0), (FFN_K - 1, 0), (0, 0)))
    y = ffn_out(h, up_hist, ffn_dw_w, ffn_w_down, ln2_g, ln2_b)
    new_state = (cmp_kv, slc_kv, win_kv[:, -min(WINDOW, s):],
                 u_hist[:, -(CONV_K - 1):], up_hist[:, -(FFN_K - 1):])
    return y, new_state


def sample_layer(x, cache_cmp, cache_slc, state_win, state_conv, state_ffn, page_table,
                 w_in, cmp_pe, cmp_k_w1, cmp_k_w2, cmp_v_w1, cmp_v_w2, conv_dw_w, conv_dw_b,
                 conv_ln_g, conv_ln_b, conv_w_pw, nsa_w_o, w_out, ln1_g, ln1_b,
                 ffn_w_up, ffn_dw_w, ffn_w_down, ln2_g, ln2_b):
    n, sq, _ = x.shape
    past = page_table.shape[1] * PAGE_SIZE
    pos = past + jnp.arange(sq)
    u, q, cmp_kv, slc_kv, win_kv, g_br, g_mix = project_inputs(x, pos, w_in)
    u_hist = jnp.concatenate([state_conv, u], axis=1)
    y_conv = conv_branch(u_hist, conv_dw_w, conv_dw_b, conv_ln_g, conv_ln_b, conv_w_pw)
    past_cmp = cache_cmp[page_table].reshape(n, past, 2, N_KV, HEAD_DIM)
    past_slc = cache_slc[page_table].reshape(n, past, 2, N_KV, HEAD_DIM)
    kc, vc = compress(jnp.concatenate([past_cmp, cmp_kv], axis=1), cmp_pe, cmp_k_w1, cmp_k_w2, cmp_v_w1, cmp_v_w2)
    ks_blk, vs_blk = to_blocks(jnp.concatenate([past_slc, slc_kv], axis=1))
    win_all = jnp.concatenate([state_win, win_kv], axis=1)
    w_eff = state_win.shape[1]
    kw_pos = past - w_eff + jnp.arange(w_eff + sq)
    o = nsa_attend(q, pos, kc, vc, ks_blk, vs_blk, win_all[:, :, 0], win_all[:, :, 1], kw_pos, g_br)
    h = mix_out(x, y_conv, o, g_mix, nsa_w_o, w_out, ln1_g, ln1_b)
    up_hist = jnp.concatenate([state_ffn, h @ ffn_w_up], axis=1)
    y = ffn_out(h, up_hist, ffn_dw_w, ffn_w_down, ln2_g, ln2_b)
    new_state = (cmp_kv, slc_kv, win_all[:, -w_eff:],
                 u_hist[:, -(CONV_K - 1):], up_hist[:, -(FFN_K - 1):])
    return y, new_state


def setup_inputs(seed: int = 0) -> dict:
    key = jax.random.key(seed)
    ks = jax.random.split(key, 32)

    def nrm(k, shape, scale):
        return jax.random.normal(k, shape, jnp.float32) * scale

    L = DEPTH
    n_pages = PAST_LEN // PAGE_SIZE
    n_used = DEC_BATCH * n_pages
    n_pool = n_used + max(1, n_used // 4)
    w_eff = min(WINDOW, PAST_LEN)
    page_table = jax.random.permutation(ks[0], n_pool)[:n_used].reshape(DEC_BATCH, n_pages).astype(jnp.int32)
    return {
        'x_prompt': nrm(ks[1], (BATCH, SEQ, D_MODEL), 1.0),
        'x_sample': nrm(ks[2], (DEC_BATCH, DEC_SEQ, D_MODEL), 1.0),
        'cache_cmp': nrm(ks[3], (L, n_pool, PAGE_SIZE, 2, N_KV, HEAD_DIM), 1.0),
        'cache_slc': nrm(ks[4], (L, n_pool, PAGE_SIZE, 2, N_KV, HEAD_DIM), 1.0),
        'state_win': nrm(ks[5], (L, DEC_BATCH, w_eff, 2, N_KV, HEAD_DIM), 1.0),
        'state_conv': nrm(ks[6], (L, DEC_BATCH, CONV_K - 1, C_CONV), 0.5),
        'state_ffn': nrm(ks[7], (L, DEC_BATCH, FFN_K - 1, 2 * D_FF), 1.0),
        'page_table': page_table,
        'w_in': nrm(ks[8], (L, D_MODEL, N_IN), D_MODEL ** -0.5),
        'cmp_pe': nrm(ks[9], (L, L_CMP, HEAD_DIM), 0.1),
        'cmp_k_w1': nrm(ks[10], (L, L_CMP, HEAD_DIM, CMP_HID), (L_CMP * HEAD_DIM) ** -0.5),
        'cmp_k_w2': nrm(ks[11], (L, CMP_HID, HEAD_DIM), CMP_HID ** -0.5),
        'cmp_v_w1': nrm(ks[12], (L, L_CMP, HEAD_DIM, CMP_HID), (L_CMP * HEAD_DIM) ** -0.5),
        'cmp_v_w2': nrm(ks[13], (L, CMP_HID, HEAD_DIM), CMP_HID ** -0.5),
        'conv_dw_w': nrm(ks[14], (L, CONV_K, C_CONV), CONV_K ** -0.5),
        'conv_dw_b': nrm(ks[15], (L, C_CONV), 0.01),
        'conv_ln_g': 1.0 + nrm(ks[16], (L, C_CONV), 0.01),
        'conv_ln_b': nrm(ks[17], (L, C_CONV), 0.01),
        'conv_w_pw': nrm(ks[18], (L, C_CONV, D_MODEL), DN_BETA * C_CONV ** -0.5),
        'nsa_w_o': nrm(ks[19], (L, N_HEADS * HEAD_DIM, D_MODEL), DN_BETA * (N_HEADS * HEAD_DIM) ** -0.5),
        'w_out': nrm(ks[20], (L, D_MODEL, D_MODEL), DN_BETA * D_MODEL ** -0.5),
        'ln1_g': 1.0 + nrm(ks[21], (L, D_MODEL), 0.01),
        'ln1_b': nrm(ks[22], (L, D_MODEL), 0.01),
        'ffn_w_up': nrm(ks[23], (L, D_MODEL, 2 * D_FF), D_MODEL ** -0.5),
        'ffn_dw_w': nrm(ks[24], (L, FFN_K, 2 * D_FF), FFN_K ** -0.5),
        'ffn_w_down': nrm(ks[25], (L, D_FF, D_MODEL), DN_BETA * D_FF ** -0.5),
        'ln2_g': 1.0 + nrm(ks[26], (L, D_MODEL), 0.01),
        'ln2_b': nrm(ks[27], (L, D_MODEL), 0.01),
    }


def reference(x_prompt, x_sample, cache_cmp, cache_slc, state_win, state_conv, state_ffn, page_table,
              w_in, cmp_pe, cmp_k_w1, cmp_k_w2, cmp_v_w1, cmp_v_w2, conv_dw_w, conv_dw_b,
              conv_ln_g, conv_ln_b, conv_w_pw, nsa_w_o, w_out, ln1_g, ln1_b,
              ffn_w_up, ffn_dw_w, ffn_w_down, ln2_g, ln2_b):
    y_prompt, y_sample = x_prompt, x_sample
    st_p, st_s = [], []
    for l in range(DEPTH):
        lw = (w_in[l], cmp_pe[l], cmp_k_w1[l], cmp_k_w2[l], cmp_v_w1[l], cmp_v_w2[l],
              conv_dw_w[l], conv_dw_b[l], conv_ln_g[l], conv_ln_b[l], conv_w_pw[l],
              nsa_w_o[l], w_out[l], ln1_g[l], ln1_b[l],
              ffn_w_up[l], ffn_dw_w[l], ffn_w_down[l], ln2_g[l], ln2_b[l])
        y_prompt, sp = prompt_layer(y_prompt, *lw)
        y_sample, ss = sample_layer(y_sample, cache_cmp[l], cache_slc[l], state_win[l], state_conv[l],
                                    state_ffn[l], page_table, *lw)
        st_p.append(sp)
        st_s.append(ss)
    cmp_prompt = jnp.stack([st[0] for st in st_p])
    cmp_sample = jnp.stack([st[0] for st in st_s])
    slc_prompt = jnp.stack([st[1] for st in st_p])
    slc_sample = jnp.stack([st[1] for st in st_s])
    win_prompt = jnp.stack([st[2] for st in st_p])
    win_sample = jnp.stack([st[2] for st in st_s])
    conv_prompt = jnp.stack([st[3] for st in st_p])
    conv_sample = jnp.stack([st[3] for st in st_s])
    ffn_prompt = jnp.stack([st[4] for st in st_p])
    ffn_sample = jnp.stack([st[4] for st in st_s])
    return (y_prompt, y_sample, cmp_prompt, cmp_sample, slc_prompt, slc_sample,
            win_prompt, win_sample, conv_prompt, conv_sample, ffn_prompt, ffn_sample)
```

```python
import functools

import jax
import jax.numpy as jnp
from jax import lax
from jax.experimental import pallas as pl
from jax.experimental.pallas import tpu as pltpu

F32 = jnp.float32
BF16 = jnp.bfloat16

D_MODEL = 1024
N_HEADS = 16
HEAD_DIM = 64
N_KV = 4
HPG = N_HEADS // N_KV
KV_W = N_KV * HEAD_DIM
ROT_DIM = HEAD_DIM // 4
ROPE_THETA = 500000.0
L_CMP = 32
D_CMP = 16
L_SLC = 64
N_SEL = 16
N_LOCAL = 2
WINDOW = 512
CMP_HID = 4 * HEAD_DIM
Q_BLOCK = 128
C_CONV = D_MODEL
CONV_K = 31
D_FF = 2816
FFN_K = 3
DEPTH = 1
DN_ALPHA = (2 * DEPTH) ** 0.25
LN_EPS = 1e-5
NEG_INF = -1e30
PAGE_SIZE = 128

LANES = 128
SUBLANES = 8
VMEM_LIMIT = 56 * 1024 * 1024

O_A = 0
O_B = C_CONV
O_Q = 2 * C_CONV
O_KV = O_Q + N_HEADS * HEAD_DIM
O_GM = O_KV + 6 * KV_W
O_GB = O_GM + 2 * D_MODEL
N_GB = 3 * N_HEADS
N_PROJ = O_GB + LANES


def _cparams(sem):
    return pltpu.CompilerParams(dimension_semantics=sem, vmem_limit_bytes=VMEM_LIMIT)


def _resident(shape):
    nd = len(shape)
    return pl.BlockSpec(shape, lambda *_: (0,) * nd, pipeline_mode=pl.Buffered(1))


def _bdot(a, b):
    return jnp.dot(a, b, preferred_element_type=F32)


def _dot_nt(a, b):
    return lax.dot_general(a, b, (((1,), (1,)), ((), ())), preferred_element_type=F32)


def _rope_tables(pos):
    half = ROT_DIM // 2
    inv = ROPE_THETA ** (-jnp.arange(half, dtype=F32) * 2.0 / ROT_DIM)
    ang = pos.astype(F32)[:, None] * inv[None, :]
    cos, sin = jnp.cos(ang), jnp.sin(ang)
    r = pos.shape[0]
    one = jnp.ones((r, HEAD_DIM - ROT_DIM), F32)
    zero = jnp.zeros((r, HEAD_DIM - ROT_DIM), F32)
    zh = jnp.zeros((r, half), F32)
    c = jnp.concatenate([cos, cos, one], axis=1)
    sa = jnp.concatenate([-sin, zh, zero], axis=1)
    sb = jnp.concatenate([zh, sin, zero], axis=1)
    rep = LANES // HEAD_DIM
    return jnp.tile(c, (1, rep)), jnp.tile(sa, (1, rep)), jnp.tile(sb, (1, rep))


def _rope(x, c, sa, sb):
    return x * c + pltpu.roll(x, LANES - ROT_DIM // 2, 1) * sa + pltpu.roll(x, ROT_DIM // 2, 1) * sb


def _proj_kernel(x_ref, w_ref, c_ref, sa_ref, sb_ref,
                 u_ref, q_ref, cmp_ref, slc_ref, win_ref, gm_ref, gb_ref, *att_refs,
                 tm, tiles_per_seq):
    xb = x_ref[...].astype(BF16)
    c, sa, sb = c_ref[...], sa_ref[...], sb_ref[...]
    cw = 2 * LANES

    def z(col, width=cw):
        return _bdot(xb, w_ref[:, col:col + width])

    for j in range(C_CONV // cw):
        u_ref[:, j * cw:(j + 1) * cw] = z(O_A + j * cw) * jax.nn.sigmoid(z(O_B + j * cw))

    def roped(col):
        zz = z(col)
        return [_rope(zz[:, h * LANES:(h + 1) * LANES], c, sa, sb) for h in range(cw // LANES)]

    scale = HEAD_DIM ** -0.5
    for j in range(N_HEADS * HEAD_DIM // cw):
        halves = roped(O_Q + j * cw)
        for h, v in enumerate(halves):
            q_ref[:, j * cw + h * LANES:j * cw + (h + 1) * LANES] = (v * scale).astype(BF16)

    if att_refs:
        kaug_ref, vs_ref, kw_ref, vw_ref = att_refs
        lane = lax.broadcasted_iota(jnp.int32, (tm, LANES), 1)
        row = lax.broadcasted_iota(jnp.int32, (tm, LANES), 0)
        pos0 = (pl.program_id(0) % tiles_per_seq) * tm
        onehot = jnp.where(lane == (pos0 + row) // L_SLC, 1.0, 0.0).astype(BF16)
        low = lane < HEAD_DIM

        def per_group(halves):
            out = []
            for g in range(N_KV):
                v = halves[g // 2]
                if g % 2:
                    v = pltpu.roll(v, HEAD_DIM, 1)
                out.append(jnp.where(low, v, 0.0).astype(BF16))
            return out

    for br, dst in enumerate((cmp_ref, slc_ref, win_ref)):
        kh = roped(O_KV + br * 2 * KV_W)
        zv = z(O_KV + br * 2 * KV_W + KV_W)
        vh = [zv[:, h * LANES:(h + 1) * LANES] for h in range(cw // LANES)]
        for h in range(cw // LANES):
            dst[:, h * LANES:(h + 1) * LANES] = kh[h]
            dst[:, KV_W + h * LANES:KV_W + (h + 1) * LANES] = vh[h]
        if att_refs and br == 1:
            for g, (kg, vg) in enumerate(zip(per_group(kh), per_group(vh))):
                kaug_ref[0, g, :, 0:LANES] = onehot
                kaug_ref[0, g, :, LANES:2 * LANES] = kg
                vs_ref[0, g] = vg
        if att_refs and br == 2:
            for g, (kg, vg) in enumerate(zip(per_group(kh), per_group(vh))):
                kw_ref[0, g] = kg
                vw_ref[0, g] = vg

    for j in range(2 * D_MODEL // cw):
        gm_ref[:, j * cw:(j + 1) * cw] = jax.nn.sigmoid(z(O_GM + j * cw))
    gb_ref[...] = jax.nn.sigmoid(z(O_GB, LANES))[:, :N_GB]


def _proj(x2d, w_r, tabs, *, tm, seq_len, n_seq, emit_att):
    r = x2d.shape[0]
    tps = seq_len // tm
    grid = (r // tm,)
    row = lambda w: pl.BlockSpec((tm, w), lambda i: (i, 0))
    tab = pl.BlockSpec((tm, LANES), lambda i: (i % tps, 0))
    out_shape = [jax.ShapeDtypeStruct((r, C_CONV), F32), jax.ShapeDtypeStruct((r, N_HEADS * HEAD_DIM), BF16),
                 jax.ShapeDtypeStruct((r, 2 * KV_W), F32), jax.ShapeDtypeStruct((r, 2 * KV_W), F32),
                 jax.ShapeDtypeStruct((r, 2 * KV_W), F32), jax.ShapeDtypeStruct((r, 2 * D_MODEL), F32),
                 jax.ShapeDtypeStruct((r, N_GB), F32)]
    out_specs = [row(C_CONV), row(N_HEADS * HEAD_DIM), row(2 * KV_W), row(2 * KV_W), row(2 * KV_W),
                 row(2 * D_MODEL), row(N_GB)]
    if emit_att:
        for w in (2 * LANES, LANES, LANES, LANES):
            out_shape.append(jax.ShapeDtypeStruct((n_seq, N_KV, seq_len, w), BF16))
            out_specs.append(pl.BlockSpec((1, N_KV, tm, w), lambda i: (i // tps, 0, i % tps, 0)))
    return pl.pallas_call(
        functools.partial(_proj_kernel, tm=tm, tiles_per_seq=tps),
        grid=grid,
        in_specs=[row(D_MODEL), _resident(w_r.shape), tab, tab, tab],
        out_specs=out_specs, out_shape=out_shape,
        compiler_params=_cparams(("arbitrary",)), name="proj",
    )(x2d, w_r, *tabs)


def _reorder_w_in(w_in):
    pad = jnp.zeros((D_MODEL, LANES - N_GB), w_in.dtype)
    o3 = 2 * C_CONV + N_HEADS * HEAD_DIM + 6 * KV_W
    return jnp.concatenate([w_in[:, :o3], w_in[:, o3 + N_GB:], w_in[:, o3:o3 + N_GB], pad], axis=1).astype(BF16)


def _compress_pair(xl, wcat_ref, w2_ref, pe_ref, nh):
    lane = lax.broadcasted_iota(jnp.int32, (nh, LANES), 1)
    low = lane < HEAD_DIM
    even, odd = [], []
    for l in range(0, D_CMP, 2):
        a, b = xl(l), xl(l + 1)
        even.append(jnp.where(low, a, pltpu.roll(b, HEAD_DIM, 1)).astype(BF16))
        odd.append(jnp.where(low, pltpu.roll(a, HEAD_DIM, 1), b).astype(BF16))
    bias2 = _bdot(pe_ref[...], wcat_ref[...])
    bias = bias2[0:1, :CMP_HID] + bias2[1:2, CMP_HID:]
    outs = []
    for x_parts in (even, odd):
        ab = _bdot(jnp.concatenate(x_parts, axis=1), wcat_ref[...])
        h = ab[:, :CMP_HID] + pltpu.roll(ab[:, CMP_HID:], nh - 1, 0) + bias
        outs.append(_bdot(jax.nn.gelu(h).astype(BF16), w2_ref[...]))
    return outs


def _compress_kernel(x_ref, wcat_ref, w2_ref, pe_ref, o_ref, *, nh):
    xl = lambda l: x_ref[0, pl.ds(l, nh, stride=D_CMP), :]
    for j, v in enumerate(_compress_pair(xl, wcat_ref.at[0], w2_ref.at[0], pe_ref, nh)):
        o_ref[0, j] = v.astype(BF16)


def _compress_weights(cmp_pe, k_w1, k_w2, v_w1, v_w2):
    def cat(w1):
        return jnp.concatenate([w1[:D_CMP].reshape(D_CMP * HEAD_DIM, CMP_HID),
                                w1[D_CMP:].reshape(D_CMP * HEAD_DIM, CMP_HID)], axis=1)
    wcat = jnp.stack([cat(k_w1), cat(v_w1)]).astype(BF16)
    pad = jnp.zeros((CMP_HID, LANES - HEAD_DIM), F32)
    w2 = jnp.stack([jnp.concatenate([k_w2, pad], 1), jnp.concatenate([v_w2, pad], 1)]).astype(BF16)
    pe2 = cmp_pe.reshape(2, D_CMP * HEAD_DIM)
    pe8 = jnp.concatenate([pe2, jnp.zeros((SUBLANES - 2, D_CMP * HEAD_DIM), F32)], 0).astype(BF16)
    return wcat, w2, pe8


def _compress_prompt(cmp2d, cw, *, n_seq, seq_len):
    wcat, w2, pe8 = cw
    nh = seq_len // D_CMP
    x3 = cmp2d.reshape(n_seq, seq_len, 2 * KV_W)
    npair = 2 * KV_W // LANES
    return pl.pallas_call(
        functools.partial(_compress_kernel, nh=nh),
        grid=(n_seq, npair),
        in_specs=[pl.BlockSpec((1, seq_len, LANES), lambda n, p: (n, 0, p)),
                  pl.BlockSpec((1,) + wcat.shape[1:], lambda n, p: (p // (npair // 2), 0, 0)),
                  pl.BlockSpec((1,) + w2.shape[1:], lambda n, p: (p // (npair // 2), 0, 0)),
                  pl.BlockSpec(pe8.shape, lambda n, p: (0, 0))],
        out_specs=pl.BlockSpec((1, 2, nh, LANES), lambda n, p: (n, p, 0, 0)),
        out_shape=jax.ShapeDtypeStruct((n_seq, 2 * N_KV, nh, LANES), BF16),
        compiler_params=_cparams(("arbitrary", "arbitrary")), name="compress_prompt",
    )(x3, wcat, w2, pe8)


BIG = 1e30


def _overlap_matrix(ncp, nb_lanes):
    start = jnp.arange(ncp) * D_CMP
    bstart = jnp.arange(nb_lanes) * L_SLC
    ov = (start[:, None] < bstart[None, :] + L_SLC) & (start[:, None] + L_CMP > bstart[None, :])
    return ov.astype(BF16)


def _split_heads(x):
    r = x.shape[0]
    low = lax.broadcasted_iota(jnp.int32, (r, LANES), 1) < HEAD_DIM
    out = []
    for h in range(HPG):
        v = x[:, (h // 2) * LANES:(h // 2 + 1) * LANES]
        if h % 2:
            v = pltpu.roll(v, HEAD_DIM, 1)
        out.append(jnp.where(low, v, 0.0))
    return out


def _merge_heads(parts):
    r = parts[0].shape[0]
    low = lax.broadcasted_iota(jnp.int32, (r, LANES), 1) < HEAD_DIM
    return [jnp.where(low, parts[2 * p], pltpu.roll(parts[2 * p + 1], HEAD_DIM, 1)) for p in range(HPG // 2)]


def _softmax(s):
    m = jnp.max(s, axis=-1, keepdims=True)
    e = jnp.exp(s - m)
    return e * (1.0 / jnp.sum(e, axis=-1, keepdims=True))


def _select_blocks(imp, cur):
    r, nbl = imp.shape
    blk = lax.broadcasted_iota(jnp.int32, (r, nbl), 1)
    blkf = blk.astype(F32)
    forced = (blk == 0) | ((blk <= cur) & (blk > cur - N_LOCAL))
    v = jnp.where(forced, BIG, imp)
    v = jnp.where(blk <= cur, v, -BIG)
    sel = jnp.zeros((r, nbl), F32)
    for _ in range(N_SEL):
        m = jnp.max(v, axis=-1, keepdims=True)
        idx = jnp.min(jnp.where(v == m, blkf, float(nbl)), axis=-1, keepdims=True)
        hit = blkf == idx
        sel = jnp.where(hit & (m > -BIG), 1.0, sel)
        v = jnp.where(hit, -BIG, v)
    return sel


def _importance(psum, ovl):
    hi = psum.astype(BF16)
    lo = (psum - hi.astype(F32)).astype(BF16)
    return _bdot(hi, ovl) + _bdot(lo, ovl)


def _attn_kernel(q_ref, gb_ref, ovl_ref, kc_ref, vc_ref, kaug_ref, vs_ref, kw_ref, vw_ref, o_ref, *, tk, ncp):
    g = pl.program_id(1)
    i = pl.program_id(2)
    rows = HPG * Q_BLOCK
    q4 = jnp.concatenate([v.astype(BF16) for v in _split_heads(q_ref[...].astype(F32))], axis=0)
    tq = i * Q_BLOCK + lax.broadcasted_iota(jnp.int32, (Q_BLOCK, 1), 0)
    tq4 = jnp.concatenate([tq] * HPG, axis=0)

    s = _dot_nt(q4, kc_ref[0, 0])
    c_end = lax.broadcasted_iota(jnp.int32, (1, ncp), 1) * D_CMP + (L_CMP - 1)
    vis = c_end <= tq4
    p = jnp.where(vis, _softmax(jnp.where(vis, s, NEG_INF)), 0.0)
    o_c = _bdot(p.astype(BF16), vc_ref[0, 0])
    psum = p[0:Q_BLOCK]
    for h in range(1, HPG):
        psum = psum + p[h * Q_BLOCK:(h + 1) * Q_BLOCK]
    sel = _select_blocks(_importance(psum, ovl_ref[...]), tq // L_SLC)
    selbias = jnp.where(sel > 0.0, 0.0, NEG_INF).astype(BF16)
    q_aug = jnp.concatenate([jnp.concatenate([selbias] * HPG, axis=0), q4], axis=1)

    def tile(j, carry, causal):
        m, l, acc = carry
        off = pl.multiple_of(j * tk, tk)
        s = _dot_nt(q_aug, kaug_ref[0, 0, pl.ds(off, tk), :])
        if causal:
            kpos = j * tk + lax.broadcasted_iota(jnp.int32, (1, tk), 1)
            s = jnp.where(kpos <= tq4, s, NEG_INF)
        m_new = jnp.maximum(m, jnp.max(s, axis=-1, keepdims=True))
        a = jnp.exp(m - m_new)
        e = jnp.exp(s - m_new)
        l = a * l + jnp.sum(e, axis=-1, keepdims=True)
        acc = a * acc + _bdot(e.astype(BF16), vs_ref[0, 0, pl.ds(off, tk), :])
        return m_new, l, acc

    init = (jnp.full((rows, 1), NEG_INF, F32), jnp.zeros((rows, 1), F32), jnp.zeros((rows, LANES), F32))
    j_last = (i * Q_BLOCK) // tk
    carry = lax.fori_loop(0, j_last, lambda j, c: tile(j, c, False), init)
    _, l, acc = tile(j_last, carry, True)
    o_s = acc * (1.0 / l)

    wlen = WINDOW + Q_BLOCK
    start = pl.multiple_of(jnp.maximum(i - WINDOW // Q_BLOCK, 0) * Q_BLOCK, Q_BLOCK)
    s = _dot_nt(q4, kw_ref[0, 0, pl.ds(start, wlen), :])
    kpos = start + lax.broadcasted_iota(jnp.int32, (1, wlen), 1)
    ok = (kpos <= tq4) & (kpos > tq4 - WINDOW)
    o_w = _bdot(_softmax(jnp.where(ok, s, NEG_INF)).astype(BF16), vw_ref[0, 0, pl.ds(start, wlen), :])

    gb = gb_ref[...]
    col = lax.broadcasted_iota(jnp.int32, gb.shape, 1)
    outs = []
    for h in range(HPG):
        def gate(br):
            return jnp.sum(jnp.where(col == br * N_HEADS + g * HPG + h, gb, 0.0), axis=-1, keepdims=True)
        sl = slice(h * Q_BLOCK, (h + 1) * Q_BLOCK)
        outs.append(gate(0) * o_c[sl] + gate(1) * o_s[sl] + gate(2) * o_w[sl])
    for pr, v in enumerate(_merge_heads(outs)):
        o_ref[:, pr * LANES:(pr + 1) * LANES] = v.astype(BF16)


def _attn_prompt(q2d, gb2d, kc, kaug, vs, kw, vw, *, n_seq, seq_len, tk):
    ncp = seq_len // D_CMP
    nq = seq_len // Q_BLOCK
    ovl = _overlap_matrix(ncp, LANES)
    gw = HPG * HEAD_DIM
    per_ng = lambda a, off: pl.BlockSpec((1, 1) + a.shape[2:], lambda n, g, i: (n, g + off, 0, 0))
    return pl.pallas_call(
        functools.partial(_attn_kernel, tk=tk, ncp=ncp),
        grid=(n_seq, N_KV, nq),
        in_specs=[pl.BlockSpec((Q_BLOCK, gw), lambda n, g, i: (n * nq + i, g)),
                  pl.BlockSpec((Q_BLOCK, N_GB), lambda n, g, i: (n * nq + i, 0)),
                  pl.BlockSpec(ovl.shape, lambda n, g, i: (0, 0)),
                  per_ng(kc, 0), per_ng(kc, N_KV), per_ng(kaug, 0), per_ng(vs, 0), per_ng(kw, 0), per_ng(vw, 0)],
        out_specs=pl.BlockSpec((Q_BLOCK, gw), lambda n, g, i: (n * nq + i, g)),
        out_shape=jax.ShapeDtypeStruct((n_seq * seq_len, N_HEADS * HEAD_DIM), BF16),
        compiler_params=_cparams(("arbitrary", "arbitrary", "arbitrary")), name="attn_prompt",
    )(q2d, gb2d, ovl, kc, kc, kaug, vs, kw, vw)


def _layer_norm(x, g, b):
    mu = jnp.mean(x, axis=-1, keepdims=True)
    xc = x - mu
    var = jnp.mean(xc * xc, axis=-1, keepdims=True)
    return xc * lax.rsqrt(var + LN_EPS) * g + b


def _mix_tail(cv, o_ref, gm_ref, x_ref, w, h_ref):
    dwb, clg, clb, pw, wo, wout, l1g, l1b = w
    hc = _layer_norm(cv + dwb[...], clg[...], clb[...])
    act = hc * jax.nn.sigmoid(hc)
    yc = _bdot(act.astype(BF16), pw[...])
    yo = _bdot(o_ref[...], wo[...])
    merged = gm_ref[:, :D_MODEL] * yc + gm_ref[:, D_MODEL:] * yo
    hh = DN_ALPHA * x_ref[...] + _bdot(merged.astype(BF16), wout[...])
    h_ref[...] = _layer_norm(hh, l1g[...], l1b[...])


CONV_HALO = 32
CONV_ROWS = 32


def _mix_prompt_kernel(u_ref, halo_ref, o_ref, gm_ref, x_ref, dww_ref, *rest, ts):
    w, h_ref, win_s, cv_s = rest[:8], rest[8], rest[9], rest[10]
    i = pl.program_id(1)

    @pl.when(i == 0)
    def _():
        win_s[0:CONV_HALO] = jnp.zeros((CONV_HALO, C_CONV), F32)

    @pl.when(i > 0)
    def _():
        win_s[0:CONV_HALO] = halo_ref[0]

    win_s[CONV_HALO:CONV_HALO + ts] = u_ref[0]
    first = CONV_HALO - (CONV_K - 1)

    for c in range(ts // CONV_ROWS):
        base = c * CONV_ROWS + first
        acc = jnp.zeros((CONV_ROWS, C_CONV), F32)
        for k in range(CONV_K):
            acc = acc + win_s[base + k:base + k + CONV_ROWS, :] * dww_ref[k:k + 1, :]
        cv_s[c * CONV_ROWS:(c + 1) * CONV_ROWS, :] = acc
    _mix_tail(cv_s[...], o_ref, gm_ref, x_ref, w, h_ref)


def _mix_sample_kernel(u_ref, st_ref, o_ref, gm_ref, x_ref, dww_ref, *rest):
    w, h_ref = rest[:8], rest[8]
    cv = u_ref[...] * dww_ref[CONV_K - 1:CONV_K, :]
    for k in range(CONV_K - 1):
        cv = cv + st_ref[k] * dww_ref[k:k + 1, :]
    _mix_tail(cv, o_ref, gm_ref, x_ref, w, h_ref)


def _mix_weights(conv_dw_b, conv_ln_g, conv_ln_b, conv_w_pw, nsa_w_o, w_out, ln1_g, ln1_b):
    r = lambda v: v.reshape(1, -1)
    return (r(conv_dw_b), r(conv_ln_g), r(conv_ln_b), conv_w_pw.astype(BF16), nsa_w_o.astype(BF16),
            w_out.astype(BF16), r(ln1_g), r(ln1_b))


def _mix_prompt(u2d, o2d, gm2d, x2d, dww, mw, *, n_seq, seq_len, ts):
    tps = seq_len // ts
    hpt = ts // CONV_HALO
    u3 = u2d.reshape(n_seq, seq_len, C_CONV)
    row = lambda w: pl.BlockSpec((ts, w), lambda n, i: (n * tps + i, 0))
    return pl.pallas_call(
        functools.partial(_mix_prompt_kernel, ts=ts),
        grid=(n_seq, tps),
        in_specs=[pl.BlockSpec((1, ts, C_CONV), lambda n, i: (n, i, 0)),
                  pl.BlockSpec((1, CONV_HALO, C_CONV), lambda n, i: (n, jnp.maximum(i * hpt - 1, 0), 0)),
                  row(N_HEADS * HEAD_DIM), row(2 * D_MODEL), row(D_MODEL), _resident(dww.shape)]
                 + [_resident(a.shape) for a in mw],
        out_specs=row(D_MODEL),
        out_shape=jax.ShapeDtypeStruct((n_seq * seq_len, D_MODEL), F32),
        scratch_shapes=[pltpu.VMEM((CONV_HALO + ts, C_CONV), F32), pltpu.VMEM((ts, C_CONV), F32)],
        compiler_params=_cparams(("arbitrary", "arbitrary")), name="mix_prompt",
    )(u3, u3, o2d, gm2d, x2d, dww, *mw)


def _mix_sample(u2d, st_t, o2d, gm2d, x2d, dww, mw):
    args = (u2d, st_t, o2d, gm2d, x2d, dww) + tuple(mw)
    full = lambda a: pl.BlockSpec(a.shape, lambda i, nd=a.ndim: (0,) * nd)
    return pl.pallas_call(
        _mix_sample_kernel, grid=(1,),
        in_specs=[full(a) for a in args],
        out_specs=pl.BlockSpec(x2d.shape, lambda i: (0, 0)),
        out_shape=jax.ShapeDtypeStruct(x2d.shape, F32),
        compiler_params=_cparams(("arbitrary",)), name="mix_sample",
    )(*args)


FFN_CW = 256


def _ffn_down(conv, h, wdown_ref, l2g_ref, l2b_ref, y_ref):
    acc = jnp.zeros(h.shape, F32)
    for c in range(D_FF // FFN_CW):
        val, gate = conv(c * FFN_CW), conv(D_FF + c * FFN_CW)
        act = gate * jax.nn.sigmoid(gate) * val
        acc = acc + _bdot(act.astype(BF16), wdown_ref[c * FFN_CW:(c + 1) * FFN_CW, :])
    y_ref[...] = _layer_norm(DN_ALPHA * h + acc, l2g_ref[...], l2b_ref[...])


def _ffn_prompt_kernel(h_ref, wup_ref, dw_ref, wdown_ref, l2g_ref, l2b_ref, y_ref, tail_ref, up_s, *, ts):
    i = pl.program_id(1)

    @pl.when(i == 0)
    def _():
        up_s[0:SUBLANES] = jnp.zeros((SUBLANES, 2 * D_FF), F32)

    @pl.when(i > 0)
    def _():
        up_s[0:SUBLANES] = up_s[ts:ts + SUBLANES]

    h = h_ref[...]
    hb = h.astype(BF16)
    cw = 2 * FFN_CW
    for c in range(2 * D_FF // cw):
        up_s[SUBLANES:SUBLANES + ts, c * cw:(c + 1) * cw] = _bdot(hb, wup_ref[:, c * cw:(c + 1) * cw])
    tail_ref[0] = up_s[ts:ts + SUBLANES]

    def conv(col):
        sl = slice(col, col + FFN_CW)
        out = up_s[SUBLANES:SUBLANES + ts, sl] * dw_ref[FFN_K - 1:FFN_K, sl]
        for k in range(FFN_K - 1):
            off = SUBLANES - (FFN_K - 1) + k
            out = out + up_s[off:off + ts, sl] * dw_ref[k:k + 1, sl]
        return out

    _ffn_down(conv, h, wdown_ref, l2g_ref, l2b_ref, y_ref)


def _ffn_sample_kernel(h_ref, s0_ref, s1_ref, wup_ref, dw_ref, wdown_ref, l2g_ref, l2b_ref, y_ref, up_ref):
    h = h_ref[...]
    hb = h.astype(BF16)
    cw = 2 * FFN_CW
    for c in range(2 * D_FF // cw):
        up_ref[:, c * cw:(c + 1) * cw] = _bdot(hb, wup_ref[:, c * cw:(c + 1) * cw])

    def conv(col):
        sl = slice(col, col + FFN_CW)
        return s0_ref[:, sl] * dw_ref[0:1, sl] + s1_ref[:, sl] * dw_ref[1:2, sl] + up_ref[:, sl] * dw_ref[2:3, sl]

    _ffn_down(conv, h, wdown_ref, l2g_ref, l2b_ref, y_ref)


def _ffn_prompt(h2d, fw, *, n_seq, seq_len, ts):
    tps = seq_len // ts
    row = pl.BlockSpec((ts, D_MODEL), lambda n, i: (n * tps + i, 0))
    return pl.pallas_call(
        functools.partial(_ffn_prompt_kernel, ts=ts),
        grid=(n_seq, tps),
        in_specs=[row] + [_resident(a.shape) for a in fw],
        out_specs=[row, pl.BlockSpec((1, SUBLANES, 2 * D_FF), lambda n, i: (n, 0, 0))],
        out_shape=[jax.ShapeDtypeStruct((n_seq * seq_len, D_MODEL), F32),
                   jax.ShapeDtypeStruct((n_seq, SUBLANES, 2 * D_FF), F32)],
        scratch_shapes=[pltpu.VMEM((SUBLANES + ts, 2 * D_FF), F32)],
        compiler_params=_cparams(("arbitrary", "arbitrary")), name="ffn_prompt",
    )(h2d, *fw)


def _ffn_sample(h2d, s0, s1, fw):
    args = (h2d, s0, s1) + tuple(fw)
    full = lambda a: pl.BlockSpec(a.shape, lambda i, nd=a.ndim: (0,) * nd)
    return pl.pallas_call(
        _ffn_sample_kernel, grid=(1,),
        in_specs=[full(a) for a in args],
        out_specs=[full(h2d), full(s0)],
        out_shape=[jax.ShapeDtypeStruct(h2d.shape, F32), jax.ShapeDtypeStruct(s0.shape, F32)],
        compiler_params=_cparams(("arbitrary",)), name="ffn_sample",
    )(*args)


def _ffn_weights(ffn_w_up, ffn_dw_w, ffn_w_down, ln2_g, ln2_b):
    return (ffn_w_up.astype(BF16), ffn_dw_w, ffn_w_down.astype(BF16), ln2_g.reshape(1, -1), ln2_b.reshape(1, -1))


SLC_CHUNK = 2048


def _dot_tn(a, b):
    return _bdot(jnp.transpose(a).astype(BF16), b)


def _col_softmax_parts(parts, masks):
    masked = [jnp.where(mk, s, NEG_INF) for s, mk in zip(parts, masks)]
    m = masked[0].max(axis=0, keepdims=True)
    for s in masked[1:]:
        m = jnp.maximum(m, s.max(axis=0, keepdims=True))
    es = [jnp.exp(s - m) for s in masked]
    l = es[0].sum(axis=0, keepdims=True)
    for e in es[1:]:
        l = l + e.sum(axis=0, keepdims=True)
    return es, l


def _rows_from_lanes(l):
    return jnp.transpose(jnp.broadcast_to(l, (LANES, LANES)))


def _new_key_term(e, kv_row):
    er = _rows_from_lanes(e)
    return jnp.concatenate([er] * (KV_W // LANES), axis=1) * kv_row[0:1, KV_W:]


def _sample_nsa_kernel(pt_ref, cmp_ref, slc_ref, qbd_ref, gt_ref, nslc_ref, nwin_ref, swin_ref,
                       wcat_ref, w2_ref, pe_ref, ovlt_ref, rep_ref, o_ref, cbuf, sbuf, sc_s, *, n_pages):
    p = pl.program_id(1)
    off = pl.multiple_of(p * PAGE_SIZE, PAGE_SIZE)
    for pair in range(2 * KV_W // LANES):
        cbuf[pair, pl.ds(off, PAGE_SIZE), :] = cmp_ref[0, :, pair * LANES:(pair + 1) * LANES]
    sbuf[pl.ds(off, PAGE_SIZE), :] = slc_ref[0].astype(BF16)

    @pl.when(p == n_pages - 1)
    def _():
        past = n_pages * PAGE_SIZE
        t = past
        nh = past // D_CMP
        nbl = 2 * LANES
        qbd = qbd_ref[0]
        low = lax.broadcasted_iota(jnp.int32, (nh, LANES), 1) < HEAD_DIM
        row8 = lax.broadcasted_iota(jnp.int32, (SUBLANES, 1), 0)

        halves = []
        for pair in range(2 * KV_W // LANES):
            xl = lambda l, pair=pair: cbuf[pair, pl.ds(l, nh, stride=D_CMP), :]
            kv = pair // (KV_W // LANES)
            a, b = _compress_pair(xl, wcat_ref.at[kv], w2_ref.at[kv], pe_ref, nh)
            halves.append(jnp.where(low, a, pltpu.roll(b, HEAD_DIM, 1)).astype(BF16))
        kc = jnp.concatenate(halves[:2], axis=1)
        vc = jnp.concatenate(halves[2:], axis=1)

        crow = lax.broadcasted_iota(jnp.int32, (nh, 1), 0)
        vis = crow * D_CMP + (L_CMP - 1) <= t
        (e,), l = _col_softmax_parts([_bdot(kc, qbd)], [vis])
        pc = jnp.where(vis, e * (1.0 / l), 0.0)
        o_c = _dot_tn(pc, vc)

        psg = pc
        for h in range(1, HPG):
            psg = psg + pltpu.roll(pc, LANES - h, 1)
        hi = psg.astype(BF16)
        lo = (psg - hi.astype(F32)).astype(BF16)
        imp_t = _bdot(ovlt_ref[...], hi) + _bdot(ovlt_ref[...], lo)
        imp = jnp.concatenate([jnp.transpose(imp_t[j * LANES:(j + 1) * LANES]) for j in range(nbl // LANES)], axis=1)
        sel = _select_blocks(imp, jnp.full((LANES, 1), t // L_SLC, jnp.int32))
        sel_h = _bdot(rep_ref[...], sel.astype(BF16))
        bias_t = jnp.concatenate([jnp.transpose(sel_h[:, j * LANES:(j + 1) * LANES]) for j in range(nbl // LANES)],
                                 axis=0)
        bias_t = jnp.where(bias_t > 0.5, 0.0, NEG_INF)

        bpc = SLC_CHUNK // L_SLC
        m = jnp.full((1, LANES), NEG_INF, F32)
        for c in range(past // SLC_CHUNK):
            s = _bdot(sbuf[c * SLC_CHUNK:(c + 1) * SLC_CHUNK, 0:KV_W], qbd)
            b = bias_t[c * bpc:(c + 1) * bpc]
            s = (s.reshape(bpc, L_SLC, LANES) + b[:, None, :]).reshape(SLC_CHUNK, LANES)
            sc_s[c * SLC_CHUNK:(c + 1) * SLC_CHUNK, :] = s
            m = jnp.maximum(m, s.max(axis=0, keepdims=True))
        knew = nslc_ref[0]
        s_new = _bdot(knew[:, :KV_W].astype(BF16), qbd)[0:1] + bias_t[past // L_SLC:past // L_SLC + 1]
        m = jnp.maximum(m, s_new)
        l = jnp.exp(s_new - m)
        acc = _new_key_term(l, knew)
        for c in range(past // SLC_CHUNK):
            e = jnp.exp(sc_s[c * SLC_CHUNK:(c + 1) * SLC_CHUNK, :] - m)
            l = l + e.sum(axis=0, keepdims=True)
            acc = acc + _dot_tn(e, sbuf[c * SLC_CHUNK:(c + 1) * SLC_CHUNK, KV_W:])
        inv = 1.0 / _rows_from_lanes(l)
        o_s = acc * jnp.concatenate([inv, inv], axis=1)

        sw = swin_ref[0]
        w_eff = sw.shape[0]
        kpos = past - w_eff + lax.broadcasted_iota(jnp.int32, (w_eff, 1), 0)
        ok = (kpos > t - WINDOW) & (kpos >= 0)
        wnew = nwin_ref[0]
        (e1, e2), l = _col_softmax_parts([_bdot(sw[:, :KV_W].astype(BF16), qbd),
                                          _bdot(wnew[:, :KV_W].astype(BF16), qbd)[0:1]], [ok, True])
        inv = 1.0 / _rows_from_lanes(l)
        o_w = (_dot_tn(e1, sw[:, KV_W:].astype(BF16)) + _new_key_term(e2, wnew)) * jnp.concatenate([inv, inv], axis=1)

        gt = gt_ref[0]
        o = gt[:, 0:1] * o_c + gt[:, 1:2] * o_s + gt[:, 2:3] * o_w
        o_ref[0] = o[0:N_HEADS]


def _sample_nsa(page_table, cache_cmp2, cache_slc2, qbd, gt, nslc, nwin, swin, cw):
    wcat, w2, pe8 = cw
    ns, n_pages = page_table.shape
    past = n_pages * PAGE_SIZE
    nbl = 2 * LANES
    ovlt = jnp.transpose(_overlap_matrix(past // D_CMP, nbl))
    hh = jnp.arange(LANES)
    rep = ((hh[None, :] == (hh[:, None] // HPG) * HPG) & (hh[:, None] < N_HEADS)).astype(BF16)
    page = pl.BlockSpec((1, PAGE_SIZE, 2 * KV_W), lambda n, p, pt: (pt[n, p], 0, 0))
    per_n = lambda a: pl.BlockSpec((1,) + a.shape[1:], lambda n, p, pt, nd=a.ndim: (n,) + (0,) * (nd - 1))
    const = lambda a: pl.BlockSpec(a.shape, lambda n, p, pt, nd=a.ndim: (0,) * nd)
    grid_spec = pltpu.PrefetchScalarGridSpec(
        num_scalar_prefetch=1, grid=(ns, n_pages),
        in_specs=[page, page, per_n(qbd), per_n(gt), per_n(nslc), per_n(nwin), per_n(swin),
                  const(wcat), const(w2), const(pe8), const(ovlt), const(rep)],
        out_specs=pl.BlockSpec((1, N_HEADS, KV_W), lambda n, p, pt: (n, 0, 0)),
        scratch_shapes=[pltpu.VMEM((2 * KV_W // LANES, past, LANES), F32), pltpu.VMEM((past, 2 * KV_W), BF16),
                        pltpu.VMEM((past, LANES), F32)])
    return pl.pallas_call(
        functools.partial(_sample_nsa_kernel, n_pages=n_pages),
        grid_spec=grid_spec,
        out_shape=jax.ShapeDtypeStruct((ns, N_HEADS, KV_W), F32),
        compiler_params=_cparams(("arbitrary", "arbitrary")), name="sample_nsa",
    )(page_table, cache_cmp2, cache_slc2, qbd, gt, nslc, nwin, swin, wcat, w2, pe8, ovlt, rep)


def kernel(x_prompt, x_sample, cache_cmp, cache_slc, state_win, state_conv, state_ffn, page_table, w_in, cmp_pe,
           cmp_k_w1, cmp_k_w2, cmp_v_w1, cmp_v_w2, conv_dw_w, conv_dw_b, conv_ln_g, conv_ln_b, conv_w_pw, nsa_w_o,
           w_out, ln1_g, ln1_b, ffn_w_up, ffn_dw_w, ffn_w_down, ln2_g, ln2_b):
    assert w_in.shape[0] == DEPTH == 1
    nb, s, _ = x_prompt.shape
    ns = x_sample.shape[0]
    n_pages = page_table.shape[1]
    past = n_pages * PAGE_SIZE
    n_pool = cache_cmp.shape[1]

    w_r = _reorder_w_in(w_in[0])
    cw = _compress_weights(cmp_pe[0], cmp_k_w1[0], cmp_k_w2[0], cmp_v_w1[0], cmp_v_w2[0])
    mw = _mix_weights(conv_dw_b[0], conv_ln_g[0], conv_ln_b[0], conv_w_pw[0], nsa_w_o[0], w_out[0], ln1_g[0], ln1_b[0])
    fw = _ffn_weights(ffn_w_up[0], ffn_dw_w[0], ffn_w_down[0], ln2_g[0], ln2_b[0])
    dww = conv_dw_w[0]

    xp = x_prompt.reshape(nb * s, D_MODEL)
    u, q, cmp_p, slc_p, win_p, gm, gb, kaug, vs, kw, vw = _proj(
        xp, w_r, _rope_tables(jnp.arange(s)), tm=256, seq_len=s, n_seq=nb, emit_att=True)
    kcv = _compress_prompt(cmp_p, cw, n_seq=nb, seq_len=s)
    o = _attn_prompt(q, gb, kcv, kaug, vs, kw, vw, n_seq=nb, seq_len=s, tk=512)
    h = _mix_prompt(u, o, gm, xp, dww, mw, n_seq=nb, seq_len=s, ts=256)
    y_p, up_tail = _ffn_prompt(h, fw, n_seq=nb, seq_len=s, ts=256)

    kv_shape = lambda n, t: (1, n, t, 2, N_KV, HEAD_DIM)
    w_keep = min(WINDOW, s)
    y_prompt = y_p.reshape(nb, s, D_MODEL)
    cmp_prompt = cmp_p.reshape(kv_shape(nb, s))
    slc_prompt = slc_p.reshape(kv_shape(nb, s))
    win_prompt = win_p.reshape(nb, s, 2 * KV_W)[:, s - w_keep:].reshape(kv_shape(nb, w_keep))
    conv_prompt = u.reshape(nb, s, C_CONV)[:, s - (CONV_K - 1):][None]
    ffn_prompt = up_tail[:, SUBLANES - (FFN_K - 1):][None]

    xs = x_sample.reshape(ns, D_MODEL)
    u_s, q_s, cmp_s, slc_s, win_s, gm_s, gb_s = _proj(
        xs, w_r, _rope_tables(jnp.full((ns,), past)), tm=ns, seq_len=ns, n_seq=1, emit_att=False)
    qt = jnp.transpose(q_s.reshape(ns, N_KV, HPG, HEAD_DIM), (0, 1, 3, 2))
    qbd = qt[:, :, :, None, :] * jnp.eye(N_KV, dtype=BF16)[None, :, None, :, None]
    qbd = jnp.pad(qbd.reshape(ns, KV_W, N_HEADS), ((0, 0), (0, 0), (0, LANES - N_HEADS)))
    gt = jnp.transpose(gb_s.reshape(ns, 3, N_HEADS), (0, 2, 1))
    gt = jnp.pad(gt, ((0, 0), (0, LANES - N_HEADS), (0, LANES - 3)))
    pad_rows = lambda a: jnp.pad(a[:, None, :], ((0, 0), (0, SUBLANES - 1), (0, 0)))
    w_eff = state_win.shape[2]
    o_s = _sample_nsa(page_table, cache_cmp[0].reshape(n_pool, PAGE_SIZE, 2 * KV_W),
                      cache_slc[0].reshape(n_pool, PAGE_SIZE, 2 * KV_W), qbd, gt, pad_rows(slc_s), pad_rows(win_s),
                      state_win[0].reshape(ns, w_eff, 2 * KV_W), cw)
    o_s = o_s.reshape(ns, N_KV, HPG, N_KV, HEAD_DIM)
    o_s = jnp.stack([o_s[:, g, :, g] for g in range(N_KV)], axis=1).reshape(ns, N_HEADS * HEAD_DIM).astype(BF16)
    h_s = _mix_sample(u_s, jnp.transpose(state_conv[0], (1, 0, 2)), o_s, gm_s, xs, dww, mw)
    y_s, up_s = _ffn_sample(h_s, state_ffn[0][:, 0], state_ffn[0][:, 1], fw)

    y_sample = y_s.reshape(ns, 1, D_MODEL)
    cmp_sample = cmp_s.reshape(kv_shape(ns, 1))
    slc_sample = slc_s.reshape(kv_shape(ns, 1))
    win_all = jnp.concatenate([state_win[0], win_s.reshape(ns, 1, 2, N_KV, HEAD_DIM)], axis=1)
    win_sample = win_all[:, -w_eff:][None]
    conv_sample = jnp.concatenate([state_conv[0], u_s[:, None]], axis=1)[:, -(CONV_K - 1):][None]
    ffn_sample = jnp.concatenate([state_ffn[0], up_s[:, None]], axis=1)[:, -(FFN_K - 1):][None]
    return (y_prompt, y_sample, cmp_prompt, cmp_sample, slc_prompt, slc_sample, win_prompt, win_sample,
            conv_prompt, conv_sample, ffn_prompt, ffn_sample)
```

```python
import functools

import jax
import jax.numpy as jnp
from jax import lax
from jax.experimental import pallas as pl
from jax.experimental.pallas import tpu as pltpu

F32 = jnp.float32
BF16 = jnp.bfloat16

D_MODEL = 1024
N_HEADS = 16
HEAD_DIM = 64
N_KV = 4
HPG = N_HEADS // N_KV
KV_W = N_KV * HEAD_DIM
ROT_DIM = HEAD_DIM // 4
ROPE_THETA = 500000.0
L_CMP = 32
D_CMP = 16
L_SLC = 64
N_SEL = 16
N_LOCAL = 2
WINDOW = 512
CMP_HID = 4 * HEAD_DIM
Q_BLOCK = 128
C_CONV = D_MODEL
CONV_K = 31
D_FF = 2816
FFN_K = 3
DEPTH = 1
DN_ALPHA = (2 * DEPTH) ** 0.25
LN_EPS = 1e-5
NEG_INF = -1e30
PAGE_SIZE = 128

LANES = 128
SUBLANES = 8
VMEM_LIMIT = 56 * 1024 * 1024

O_A = 0
O_B = C_CONV
O_Q = 2 * C_CONV
O_KV = O_Q + N_HEADS * HEAD_DIM
O_GM = O_KV + 6 * KV_W
O_GB = O_GM + 2 * D_MODEL
N_GB = 3 * N_HEADS
N_PROJ = O_GB + LANES


def _cparams(sem):
    return pltpu.CompilerParams(dimension_semantics=sem, vmem_limit_bytes=VMEM_LIMIT)


def _resident(shape):
    nd = len(shape)
    return pl.BlockSpec(shape, lambda *_: (0,) * nd, pipeline_mode=pl.Buffered(1))


def _bdot(a, b):
    return jnp.dot(a, b, preferred_element_type=F32)


def _dot_nt(a, b):
    return lax.dot_general(a, b, (((1,), (1,)), ((), ())), preferred_element_type=F32)


def _rope_tables(pos):
    half = ROT_DIM // 2
    inv = ROPE_THETA ** (-jnp.arange(half, dtype=F32) * 2.0 / ROT_DIM)
    ang = pos.astype(F32)[:, None] * inv[None, :]
    cos, sin = jnp.cos(ang), jnp.sin(ang)
    r = pos.shape[0]
    one = jnp.ones((r, HEAD_DIM - ROT_DIM), F32)
    zero = jnp.zeros((r, HEAD_DIM - ROT_DIM), F32)
    zh = jnp.zeros((r, half), F32)
    c = jnp.concatenate([cos, cos, one], axis=1)
    sa = jnp.concatenate([-sin, zh, zero], axis=1)
    sb = jnp.concatenate([zh, sin, zero], axis=1)
    rep = LANES // HEAD_DIM
    return jnp.tile(c, (1, rep)), jnp.tile(sa, (1, rep)), jnp.tile(sb, (1, rep))


def _rope(x, c, sa, sb):
    return x * c + pltpu.roll(x, LANES - ROT_DIM // 2, 1) * sa + pltpu.roll(x, ROT_DIM // 2, 1) * sb


def _proj_kernel(x_ref, w_ref, c_ref, sa_ref, sb_ref,
                 u_ref, q_ref, cmp_ref, slc_ref, win_ref, gm_ref, gb_ref, *att_refs,
                 tm, tiles_per_seq):
    xb = x_ref[...].astype(BF16)
    c, sa, sb = c_ref[...], sa_ref[...], sb_ref[...]
    cw = 2 * LANES

    def z(col, width=cw):
        return _bdot(xb, w_ref[:, col:col + width])

    for j in range(C_CONV // cw):
        u_ref[:, j * cw:(j + 1) * cw] = z(O_A + j * cw) * jax.nn.sigmoid(z(O_B + j * cw))

    def roped(col):
        zz = z(col)
        return [_rope(zz[:, h * LANES:(h + 1) * LANES], c, sa, sb) for h in range(cw // LANES)]

    scale = HEAD_DIM ** -0.5
    for j in range(N_HEADS * HEAD_DIM // cw):
        halves = roped(O_Q + j * cw)
        for h, v in enumerate(halves):
            q_ref[:, j * cw + h * LANES:j * cw + (h + 1) * LANES] = (v * scale).astype(BF16)

    if att_refs:
        kaug_ref, vs_ref, kw_ref, vw_ref = att_refs
        lane = lax.broadcasted_iota(jnp.int32, (tm, LANES), 1)
        row = lax.broadcasted_iota(jnp.int32, (tm, LANES), 0)
        pos0 = (pl.program_id(0) % tiles_per_seq) * tm
        onehot = jnp.where(lane == (pos0 + row) // L_SLC, 1.0, 0.0).astype(BF16)
        low = lane < HEAD_DIM

        def per_group(halves):
            out = []
            for g in range(N_KV):
                v = halves[g // 2]
                if g % 2:
                    v = pltpu.roll(v, HEAD_DIM, 1)
                out.append(jnp.where(low, v, 0.0).astype(BF16))
            return out

    for br, dst in enumerate((cmp_ref, slc_ref, win_ref)):
        kh = roped(O_KV + br * 2 * KV_W)
        zv = z(O_KV + br * 2 * KV_W + KV_W)
        vh = [zv[:, h * LANES:(h + 1) * LANES] for h in range(cw // LANES)]
        for h in range(cw // LANES):
            dst[:, h * LANES:(h + 1) * LANES] = kh[h]
            dst[:, KV_W + h * LANES:KV_W + (h + 1) * LANES] = vh[h]
        if att_refs and br == 1:
            for g, (kg, vg) in enumerate(zip(per_group(kh), per_group(vh))):
                kaug_ref[0, g, :, 0:LANES] = onehot
                kaug_ref[0, g, :, LANES:2 * LANES] = kg
                vs_ref[0, g] = vg
        if att_refs and br == 2:
            for g, (kg, vg) in enumerate(zip(per_group(kh), per_group(vh))):
                kw_ref[0, g] = kg
                vw_ref[0, g] = vg

    for j in range(2 * D_MODEL // cw):
        gm_ref[:, j * cw:(j + 1) * cw] = jax.nn.sigmoid(z(O_GM + j * cw))
    gb_ref[...] = jax.nn.sigmoid(z(O_GB, LANES))[:, :N_GB]


def _proj(x2d, w_r, tabs, *, tm, seq_len, n_seq, emit_att):
    r = x2d.shape[0]
    tps = seq_len // tm
    grid = (r // tm,)
    row = lambda w: pl.BlockSpec((tm, w), lambda i: (i, 0))
    tab = pl.BlockSpec((tm, LANES), lambda i: (i % tps, 0))
    out_shape = [jax.ShapeDtypeStruct((r, C_CONV), F32), jax.ShapeDtypeStruct((r, N_HEADS * HEAD_DIM), BF16),
                 jax.ShapeDtypeStruct((r, 2 * KV_W), F32), jax.ShapeDtypeStruct((r, 2 * KV_W), F32),
                 jax.ShapeDtypeStruct((r, 2 * KV_W), F32), jax.ShapeDtypeStruct((r, 2 * D_MODEL), F32),
                 jax.ShapeDtypeStruct((r, N_GB), F32)]
    out_specs = [row(C_CONV), row(N_HEADS * HEAD_DIM), row(2 * KV_W), row(2 * KV_W), row(2 * KV_W),
                 row(2 * D_MODEL), row(N_GB)]
    if emit_att:
        for w in (2 * LANES, LANES, LANES, LANES):
            out_shape.append(jax.ShapeDtypeStruct((n_seq, N_KV, seq_len, w), BF16))
            out_specs.append(pl.BlockSpec((1, N_KV, tm, w), lambda i: (i // tps, 0, i % tps, 0)))
    return pl.pallas_call(
        functools.partial(_proj_kernel, tm=tm, tiles_per_seq=tps),
        grid=grid,
        in_specs=[row(D_MODEL), _resident(w_r.shape), tab, tab, tab],
        out_specs=out_specs, out_shape=out_shape,
        compiler_params=_cparams(("arbitrary",)), name="proj",
    )(x2d, w_r, *tabs)


def _reorder_w_in(w_in):
    pad = jnp.zeros((D_MODEL, LANES - N_GB), w_in.dtype)
    o3 = 2 * C_CONV + N_HEADS * HEAD_DIM + 6 * KV_W
    return jnp.concatenate([w_in[:, :o3], w_in[:, o3 + N_GB:], w_in[:, o3:o3 + N_GB], pad], axis=1).astype(BF16)


def _compress_pair(xl, wcat_ref, w2_ref, pe_ref, nh):
    lane = lax.broadcasted_iota(jnp.int32, (nh, LANES), 1)
    low = lane < HEAD_DIM
    even, odd = [], []
    for l in range(0, D_CMP, 2):
        a, b = xl(l), xl(l + 1)
        even.append(jnp.where(low, a, pltpu.roll(b, HEAD_DIM, 1)).astype(BF16))
        odd.append(jnp.where(low, pltpu.roll(a, HEAD_DIM, 1), b).astype(BF16))
    bias2 = _bdot(pe_ref[...], wcat_ref[...])
    bias = bias2[0:1, :CMP_HID] + bias2[1:2, CMP_HID:]
    outs = []
    for x_parts in (even, odd):
        ab = _bdot(jnp.concatenate(x_parts, axis=1), wcat_ref[...])
        h = ab[:, :CMP_HID] + pltpu.roll(ab[:, CMP_HID:], nh - 1, 0) + bias
        outs.append(_bdot(jax.nn.gelu(h).astype(BF16), w2_ref[...]))
    return outs


def _compress_kernel(x_ref, wcat_ref, w2_ref, pe_ref, o_ref, *, nh):
    xl = lambda l: x_ref[0, pl.ds(l, nh, stride=D_CMP), :]
    for j, v in enumerate(_compress_pair(xl, wcat_ref.at[0], w2_ref.at[0], pe_ref, nh)):
        o_ref[0, j] = v.astype(BF16)


def _compress_weights(cmp_pe, k_w1, k_w2, v_w1, v_w2):
    def cat(w1):
        return jnp.concatenate([w1[:D_CMP].reshape(D_CMP * HEAD_DIM, CMP_HID),
                                w1[D_CMP:].reshape(D_CMP * HEAD_DIM, CMP_HID)], axis=1)
    wcat = jnp.stack([cat(k_w1), cat(v_w1)]).astype(BF16)
    pad = jnp.zeros((CMP_HID, LANES - HEAD_DIM), F32)
    w2 = jnp.stack([jnp.concatenate([k_w2, pad], 1), jnp.concatenate([v_w2, pad], 1)]).astype(BF16)
    pe2 = cmp_pe.reshape(2, D_CMP * HEAD_DIM)
    pe8 = jnp.concatenate([pe2, jnp.zeros((SUBLANES - 2, D_CMP * HEAD_DIM), F32)], 0).astype(BF16)
    return wcat, w2, pe8


def _compress_prompt(cmp2d, cw, *, n_seq, seq_len):
    wcat, w2, pe8 = cw
    nh = seq_len // D_CMP
    x3 = cmp2d.reshape(n_seq, seq_len, 2 * KV_W)
    npair = 2 * KV_W // LANES
    return pl.pallas_call(
        functools.partial(_compress_kernel, nh=nh),
        grid=(n_seq, npair),
        in_specs=[pl.BlockSpec((1, seq_len, LANES), lambda n, p: (n, 0, p)),
                  pl.BlockSpec((1,) + wcat.shape[1:], lambda n, p: (p // (npair // 2), 0, 0)),
                  pl.BlockSpec((1,) + w2.shape[1:], lambda n, p: (p // (npair // 2), 0, 0)),
                  pl.BlockSpec(pe8.shape, lambda n, p: (0, 0))],
        out_specs=pl.BlockSpec((1, 2, nh, LANES), lambda n, p: (n, p, 0, 0)),
        out_shape=jax.ShapeDtypeStruct((n_seq, 2 * N_KV, nh, LANES), BF16),
        compiler_params=_cparams(("arbitrary", "arbitrary")), name="compress_prompt",
    )(x3, wcat, w2, pe8)


BIG = 1e30


def _overlap_matrix(ncp, nb_lanes):
    start = jnp.arange(ncp) * D_CMP
    bstart = jnp.arange(nb_lanes) * L_SLC
    ov = (start[:, None] < bstart[None, :] + L_SLC) & (start[:, None] + L_CMP > bstart[None, :])
    return ov.astype(BF16)


def _split_heads(x):
    r = x.shape[0]
    low = lax.broadcasted_iota(jnp.int32, (r, LANES), 1) < HEAD_DIM
    out = []
    for h in range(HPG):
        v = x[:, (h // 2) * LANES:(h // 2 + 1) * LANES]
        if h % 2:
            v = pltpu.roll(v, HEAD_DIM, 1)
        out.append(jnp.where(low, v, 0.0))
    return out


def _merge_heads(parts):
    r = parts[0].shape[0]
    low = lax.broadcasted_iota(jnp.int32, (r, LANES), 1) < HEAD_DIM
    return [jnp.where(low, parts[2 * p], pltpu.roll(parts[2 * p + 1], HEAD_DIM, 1)) for p in range(HPG // 2)]


def _softmax(s):
    m = jnp.max(s, axis=-1, keepdims=True)
    e = jnp.exp(s - m)
    return e * (1.0 / jnp.sum(e, axis=-1, keepdims=True))


def _select_blocks(imp, cur):
    r, nbl = imp.shape
    blk = lax.broadcasted_iota(jnp.int32, (r, nbl), 1)
    forced = (blk == 0) | ((blk <= cur) & (blk > cur - N_LOCAL))
    v = jnp.where(forced, BIG, imp)
    v = jnp.where(blk <= cur, v, -BIG)
    sel = jnp.zeros((r, nbl), F32)
    for _ in range(N_SEL):
        hit = blk == jnp.argmax(v, axis=-1, keepdims=True)
        sel = jnp.where(hit & (v > -BIG), 1.0, sel)
        v = jnp.where(hit, -BIG, v)
    return sel


def _importance(psum, ovl):
    hi = psum.astype(BF16)
    lo = (psum - hi.astype(F32)).astype(BF16)
    return _bdot(hi, ovl) + _bdot(lo, ovl)


def _head_gates(gb, g, br, h):
    col = lax.broadcasted_iota(jnp.int32, gb.shape, 1)
    return jnp.sum(jnp.where(col == br * N_HEADS + g * HPG + h, gb, 0.0), axis=-1, keepdims=True)


def _select_kernel(q_ref, gb_ref, ovl_ref, kcv_ref, ocg_ref, sb_ref, *, qs, ncp):
    i = pl.program_id(1)
    tq = i * qs + lax.broadcasted_iota(jnp.int32, (qs, 1), 0)
    tq4 = jnp.concatenate([tq] * HPG, axis=0)
    c_end = lax.broadcasted_iota(jnp.int32, (1, ncp), 1) * D_CMP + (L_CMP - 1)
    vis = c_end <= tq4
    gb = gb_ref[...]
    gw = HPG * HEAD_DIM
    imps = []
    for g in range(N_KV):
        qf = q_ref[:, g * gw:(g + 1) * gw].astype(F32)
        q4 = jnp.concatenate([v.astype(BF16) for v in _split_heads(qf)], axis=0)
        s = _dot_nt(q4, kcv_ref[0, g])
        p = jnp.where(vis, _softmax(jnp.where(vis, s, NEG_INF)), 0.0)
        o_c = _bdot(p.astype(BF16), kcv_ref[0, N_KV + g])
        psum = p[0:qs]
        for h in range(1, HPG):
            psum = psum + p[h * qs:(h + 1) * qs]
        imps.append(_importance(psum, ovl_ref[...]))
        outs = [_head_gates(gb, g, 0, h) * o_c[h * qs:(h + 1) * qs] for h in range(HPG)]
        for pr, v in enumerate(_merge_heads(outs)):
            ocg_ref[:, g * gw + pr * LANES:g * gw + (pr + 1) * LANES] = v
    sel = _select_blocks(jnp.concatenate(imps, axis=0), jnp.concatenate([tq // L_SLC] * N_KV, axis=0))
    sb = jnp.where(sel > 0.0, 0.0, NEG_INF).astype(BF16)
    for g in range(N_KV):
        sb_ref[0, g] = sb[g * qs:(g + 1) * qs]


def _select_prompt(q2d, gb2d, kcv, *, n_seq, seq_len, qs):
    ncp = seq_len // D_CMP
    nq = seq_len // qs
    ovl = _overlap_matrix(ncp, LANES)
    row = lambda w: pl.BlockSpec((qs, w), lambda n, i: (n * nq + i, 0))
    return pl.pallas_call(
        functools.partial(_select_kernel, qs=qs, ncp=ncp),
        grid=(n_seq, nq),
        in_specs=[row(N_HEADS * HEAD_DIM), row(N_GB), pl.BlockSpec(ovl.shape, lambda n, i: (0, 0)),
                  pl.BlockSpec((1,) + kcv.shape[1:], lambda n, i: (n, 0, 0, 0))],
        out_specs=[row(N_HEADS * HEAD_DIM), pl.BlockSpec((1, N_KV, qs, LANES), lambda n, i: (n, 0, i, 0))],
        out_shape=[jax.ShapeDtypeStruct((n_seq * seq_len, N_HEADS * HEAD_DIM), F32),
                   jax.ShapeDtypeStruct((n_seq, N_KV, seq_len, LANES), BF16)],
        compiler_params=_cparams(("arbitrary", "arbitrary")), name="select_prompt",
    )(q2d, gb2d, ovl, kcv)


def _attn_kernel(q_ref, gb_ref, ocg_ref, sb_ref, kaug_ref, vs_ref, kw_ref, vw_ref, o_ref, *, tk, nsplit):
    g = pl.program_id(1)
    i = pl.program_id(2)
    rows = HPG * Q_BLOCK
    q4 = jnp.concatenate([v.astype(BF16) for v in _split_heads(q_ref[...].astype(F32))], axis=0)
    tq = i * Q_BLOCK + lax.broadcasted_iota(jnp.int32, (Q_BLOCK, 1), 0)
    tq4 = jnp.concatenate([tq] * HPG, axis=0)
    q_aug = jnp.concatenate([jnp.concatenate([sb_ref[0, 0]] * HPG, axis=0), q4], axis=1)

    sub = tk // nsplit

    def tile(j, carry, causal):
        out = []
        for c, (m, l, acc) in enumerate(carry):
            off = pl.multiple_of(j * tk + c * sub, sub)
            s = _dot_nt(q_aug, kaug_ref[0, 0, pl.ds(off, sub), :])
            if causal:
                kpos = off + lax.broadcasted_iota(jnp.int32, (1, sub), 1)
                s = jnp.where(kpos <= tq4, s, NEG_INF)
            m_new = jnp.maximum(m, jnp.max(s, axis=-1, keepdims=True))
            a = jnp.exp(m - m_new)
            e = jnp.exp(s - m_new)
            out.append((m_new, a * l + jnp.sum(e, axis=-1, keepdims=True),
                        a * acc + _bdot(e.astype(BF16), vs_ref[0, 0, pl.ds(off, sub), :])))
        return tuple(out)

    init = tuple((jnp.full((rows, 1), NEG_INF, F32), jnp.zeros((rows, 1), F32), jnp.zeros((rows, LANES), F32))
                 for _ in range(nsplit))
    j_last = (i * Q_BLOCK) // tk
    chains = tile(j_last, lax.fori_loop(0, j_last, lambda j, c: tile(j, c, False), init), True)
    m = chains[0][0]
    for mc, _, _ in chains[1:]:
        m = jnp.maximum(m, mc)
    l = sum(jnp.exp(mc - m) * lc for mc, lc, _ in chains)
    acc = sum(jnp.exp(mc - m) * ac for mc, _, ac in chains)
    o_s = acc * (1.0 / l)

    wlen = WINDOW + Q_BLOCK
    start = pl.multiple_of(jnp.maximum(i - WINDOW // Q_BLOCK, 0) * Q_BLOCK, Q_BLOCK)
    s = _dot_nt(q4, kw_ref[0, 0, pl.ds(start, wlen), :])
    kpos = start + lax.broadcasted_iota(jnp.int32, (1, wlen), 1)
    ok = (kpos <= tq4) & (kpos > tq4 - WINDOW)
    o_w = _bdot(_softmax(jnp.where(ok, s, NEG_INF)).astype(BF16), vw_ref[0, 0, pl.ds(start, wlen), :])

    gb = gb_ref[...]
    outs = []
    for h in range(HPG):
        sl = slice(h * Q_BLOCK, (h + 1) * Q_BLOCK)
        outs.append(_head_gates(gb, g, 1, h) * o_s[sl] + _head_gates(gb, g, 2, h) * o_w[sl])
    for pr, v in enumerate(_merge_heads(outs)):
        o_ref[:, pr * LANES:(pr + 1) * LANES] = (ocg_ref[:, pr * LANES:(pr + 1) * LANES] + v).astype(BF16)


def _attn_prompt(q2d, gb2d, ocg, sb, kaug, vs, kw, vw, *, n_seq, seq_len, tk, nsplit):
    nq = seq_len // Q_BLOCK
    gw = HPG * HEAD_DIM
    per_ng = lambda a: pl.BlockSpec((1, 1) + a.shape[2:], lambda n, g, i: (n, g, 0, 0))
    qblk = pl.BlockSpec((Q_BLOCK, gw), lambda n, g, i: (n * nq + i, g))
    return pl.pallas_call(
        functools.partial(_attn_kernel, tk=tk, nsplit=nsplit),
        grid=(n_seq, N_KV, nq),
        in_specs=[qblk, pl.BlockSpec((Q_BLOCK, N_GB), lambda n, g, i: (n * nq + i, 0)), qblk,
                  pl.BlockSpec((1, 1, Q_BLOCK, LANES), lambda n, g, i: (n, g, i, 0)),
                  per_ng(kaug), per_ng(vs), per_ng(kw), per_ng(vw)],
        out_specs=qblk,
        out_shape=jax.ShapeDtypeStruct((n_seq * seq_len, N_HEADS * HEAD_DIM), BF16),
        compiler_params=_cparams(("arbitrary", "arbitrary", "arbitrary")), name="attn_prompt",
    )(q2d, gb2d, ocg, sb, kaug, vs, kw, vw)


def _layer_norm(x, g, b):
    mu = jnp.mean(x, axis=-1, keepdims=True)
    xc = x - mu
    var = jnp.mean(xc * xc, axis=-1, keepdims=True)
    return xc * lax.rsqrt(var + LN_EPS) * g + b


def _mix_tail(cv, o_ref, gm_ref, x_ref, w, h_ref):
    dwb, clg, clb, pw, wo, wout, l1g, l1b = w
    hc = _layer_norm(cv + dwb[...], clg[...], clb[...])
    act = hc * jax.nn.sigmoid(hc)
    yc = _bdot(act.astype(BF16), pw[...])
    yo = _bdot(o_ref[...], wo[...])
    merged = gm_ref[:, :D_MODEL] * yc + gm_ref[:, D_MODEL:] * yo
    hh = DN_ALPHA * x_ref[...] + _bdot(merged.astype(BF16), wout[...])
    h_ref[...] = _layer_norm(hh, l1g[...], l1b[...])


CONV_HALO = 32
CONV_ROWS = 32


def _mix_prompt_kernel(u_ref, halo_ref, o_ref, gm_ref, x_ref, dww_ref, *rest, ts):
    w, h_ref, win_s, cv_s = rest[:8], rest[8], rest[9], rest[10]
    i = pl.program_id(1)

    @pl.when(i == 0)
    def _():
        win_s[0:CONV_HALO] = jnp.zeros((CONV_HALO, C_CONV), F32)

    @pl.when(i > 0)
    def _():
        win_s[0:CONV_HALO] = halo_ref[0]

    win_s[CONV_HALO:CONV_HALO + ts] = u_ref[0]
    first = CONV_HALO - (CONV_K - 1)

    for c in range(ts // CONV_ROWS):
        base = c * CONV_ROWS + first
        acc = jnp.zeros((CONV_ROWS, C_CONV), F32)
        for k in range(CONV_K):
            acc = acc + win_s[base + k:base + k + CONV_ROWS, :] * dww_ref[k:k + 1, :]
        cv_s[c * CONV_ROWS:(c + 1) * CONV_ROWS, :] = acc
    _mix_tail(cv_s[...], o_ref, gm_ref, x_ref, w, h_ref)


def _mix_sample_kernel(u_ref, st_ref, o_ref, gm_ref, x_ref, dww_ref, *rest):
    w, h_ref = rest[:8], rest[8]
    cv = u_ref[...] * dww_ref[CONV_K - 1:CONV_K, :]
    for k in range(CONV_K - 1):
        cv = cv + st_ref[k] * dww_ref[k:k + 1, :]
    _mix_tail(cv, o_ref, gm_ref, x_ref, w, h_ref)


def _mix_weights(conv_dw_b, conv_ln_g, conv_ln_b, conv_w_pw, nsa_w_o, w_out, ln1_g, ln1_b):
    r = lambda v: v.reshape(1, -1)
    return (r(conv_dw_b), r(conv_ln_g), r(conv_ln_b), conv_w_pw.astype(BF16), nsa_w_o.astype(BF16),
            w_out.astype(BF16), r(ln1_g), r(ln1_b))


def _mix_prompt(u2d, o2d, gm2d, x2d, dww, mw, *, n_seq, seq_len, ts):
    tps = seq_len // ts
    hpt = ts // CONV_HALO
    u3 = u2d.reshape(n_seq, seq_len, C_CONV)
    row = lambda w: pl.BlockSpec((ts, w), lambda n, i: (n * tps + i, 0))
    return pl.pallas_call(
        functools.partial(_mix_prompt_kernel, ts=ts),
        grid=(n_seq, tps),
        in_specs=[pl.BlockSpec((1, ts, C_CONV), lambda n, i: (n, i, 0)),
                  pl.BlockSpec((1, CONV_HALO, C_CONV), lambda n, i: (n, jnp.maximum(i * hpt - 1, 0), 0)),
                  row(N_HEADS * HEAD_DIM), row(2 * D_MODEL), row(D_MODEL), _resident(dww.shape)]
                 + [_resident(a.shape) for a in mw],
        out_specs=row(D_MODEL),
        out_shape=jax.ShapeDtypeStruct((n_seq * seq_len, D_MODEL), F32),
        scratch_shapes=[pltpu.VMEM((CONV_HALO + ts, C_CONV), F32), pltpu.VMEM((ts, C_CONV), F32)],
        compiler_params=_cparams(("arbitrary", "arbitrary")), name="mix_prompt",
    )(u3, u3, o2d, gm2d, x2d, dww, *mw)


def _mix_sample(u2d, st_t, o2d, gm2d, x2d, dww, mw):
    args = (u2d, st_t, o2d, gm2d, x2d, dww) + tuple(mw)
    full = lambda a: pl.BlockSpec(a.shape, lambda i, nd=a.ndim: (0,) * nd)
    return pl.pallas_call(
        _mix_sample_kernel, grid=(1,),
        in_specs=[full(a) for a in args],
        out_specs=pl.BlockSpec(x2d.shape, lambda i: (0, 0)),
        out_shape=jax.ShapeDtypeStruct(x2d.shape, F32),
        compiler_params=_cparams(("arbitrary",)), name="mix_sample",
    )(*args)


FFN_CW = 256


def _ffn_down(conv, h, wdown_ref, l2g_ref, l2b_ref, y_ref):
    acc = jnp.zeros(h.shape, F32)
    for c in range(D_FF // FFN_CW):
        val, gate = conv(c * FFN_CW), conv(D_FF + c * FFN_CW)
        act = gate * jax.nn.sigmoid(gate) * val
        acc = acc + _bdot(act.astype(BF16), wdown_ref[c * FFN_CW:(c + 1) * FFN_CW, :])
    y_ref[...] = _layer_norm(DN_ALPHA * h + acc, l2g_ref[...], l2b_ref[...])


def _ffn_prompt_kernel(h_ref, wup_ref, dw_ref, wdown_ref, l2g_ref, l2b_ref, y_ref, tail_ref, up_s, *, ts):
    i = pl.program_id(1)

    @pl.when(i == 0)
    def _():
        up_s[0:SUBLANES] = jnp.zeros((SUBLANES, 2 * D_FF), F32)

    @pl.when(i > 0)
    def _():
        up_s[0:SUBLANES] = up_s[ts:ts + SUBLANES]

    h = h_ref[...]
    hb = h.astype(BF16)
    cw = 2 * FFN_CW
    for c in range(2 * D_FF // cw):
        up_s[SUBLANES:SUBLANES + ts, c * cw:(c + 1) * cw] = _bdot(hb, wup_ref[:, c * cw:(c + 1) * cw])
    tail_ref[0] = up_s[ts:ts + SUBLANES]

    def conv(col):
        sl = slice(col, col + FFN_CW)
        out = up_s[SUBLANES:SUBLANES + ts, sl] * dw_ref[FFN_K - 1:FFN_K, sl]
        for k in range(FFN_K - 1):
            off = SUBLANES - (FFN_K - 1) + k
            out = out + up_s[off:off + ts, sl] * dw_ref[k:k + 1, sl]
        return out

    _ffn_down(conv, h, wdown_ref, l2g_ref, l2b_ref, y_ref)


def _ffn_sample_kernel(h_ref, s0_ref, s1_ref, wup_ref, dw_ref, wdown_ref, l2g_ref, l2b_ref, y_ref, up_ref):
    h = h_ref[...]
    hb = h.astype(BF16)
    cw = 2 * FFN_CW
    for c in range(2 * D_FF // cw):
        up_ref[:, c * cw:(c + 1) * cw] = _bdot(hb, wup_ref[:, c * cw:(c + 1) * cw])

    def conv(col):
        sl = slice(col, col + FFN_CW)
        return s0_ref[:, sl] * dw_ref[0:1, sl] + s1_ref[:, sl] * dw_ref[1:2, sl] + up_ref[:, sl] * dw_ref[2:3, sl]

    _ffn_down(conv, h, wdown_ref, l2g_ref, l2b_ref, y_ref)


def _ffn_prompt(h2d, fw, *, n_seq, seq_len, ts):
    tps = seq_len // ts
    row = pl.BlockSpec((ts, D_MODEL), lambda n, i: (n * tps + i, 0))
    return pl.pallas_call(
        functools.partial(_ffn_prompt_kernel, ts=ts),
        grid=(n_seq, tps),
        in_specs=[row] + [_resident(a.shape) for a in fw],
        out_specs=[row, pl.BlockSpec((1, SUBLANES, 2 * D_FF), lambda n, i: (n, 0, 0))],
        out_shape=[jax.ShapeDtypeStruct((n_seq * seq_len, D_MODEL), F32),
                   jax.ShapeDtypeStruct((n_seq, SUBLANES, 2 * D_FF), F32)],
        scratch_shapes=[pltpu.VMEM((SUBLANES + ts, 2 * D_FF), F32)],
        compiler_params=_cparams(("arbitrary", "arbitrary")), name="ffn_prompt",
    )(h2d, *fw)


def _ffn_sample(h2d, s0, s1, fw):
    args = (h2d, s0, s1) + tuple(fw)
    full = lambda a: pl.BlockSpec(a.shape, lambda i, nd=a.ndim: (0,) * nd)
    return pl.pallas_call(
        _ffn_sample_kernel, grid=(1,),
        in_specs=[full(a) for a in args],
        out_specs=[full(h2d), full(s0)],
        out_shape=[jax.ShapeDtypeStruct(h2d.shape, F32), jax.ShapeDtypeStruct(s0.shape, F32)],
        compiler_params=_cparams(("arbitrary",)), name="ffn_sample",
    )(*args)


def _ffn_weights(ffn_w_up, ffn_dw_w, ffn_w_down, ln2_g, ln2_b):
    return (ffn_w_up.astype(BF16), ffn_dw_w, ffn_w_down.astype(BF16), ln2_g.reshape(1, -1), ln2_b.reshape(1, -1))


SLC_CHUNK = 2048


def _dot_tn(a, b):
    return _bdot(jnp.transpose(a).astype(BF16), b)


def _col_softmax_parts(parts, masks):
    masked = [jnp.where(mk, s, NEG_INF) for s, mk in zip(parts, masks)]
    m = masked[0].max(axis=0, keepdims=True)
    for s in masked[1:]:
        m = jnp.maximum(m, s.max(axis=0, keepdims=True))
    es = [jnp.exp(s - m) for s in masked]
    l = es[0].sum(axis=0, keepdims=True)
    for e in es[1:]:
        l = l + e.sum(axis=0, keepdims=True)
    return es, l


def _rows_from_lanes(l):
    return jnp.transpose(jnp.broadcast_to(l, (LANES, LANES)))


def _new_key_term(e, kv_row):
    er = _rows_from_lanes(e)
    return jnp.concatenate([er] * (KV_W // LANES), axis=1) * kv_row[0:1, KV_W:]


def _sample_nsa_kernel(pt_ref, cmp_ref, slc_ref, qbd_ref, gt_ref, nslc_ref, nwin_ref, swin_ref,
                       wcat_ref, w2_ref, pe_ref, ovlt_ref, rep_ref, o_ref, cbuf, sbuf, sc_s, *, n_pages):
    p = pl.program_id(1)
    off = pl.multiple_of(p * PAGE_SIZE, PAGE_SIZE)
    for pair in range(2 * KV_W // LANES):
        cbuf[pair, pl.ds(off, PAGE_SIZE), :] = cmp_ref[0, :, pair * LANES:(pair + 1) * LANES]
    sbuf[pl.ds(off, PAGE_SIZE), :] = slc_ref[0].astype(BF16)

    @pl.when(p == n_pages - 1)
    def _():
        past = n_pages * PAGE_SIZE
        t = past
        nh = past // D_CMP
        nbl = 2 * LANES
        qbd = qbd_ref[0]
        low = lax.broadcasted_iota(jnp.int32, (nh, LANES), 1) < HEAD_DIM
        row8 = lax.broadcasted_iota(jnp.int32, (SUBLANES, 1), 0)

        halves = []
        for pair in range(2 * KV_W // LANES):
            xl = lambda l, pair=pair: cbuf[pair, pl.ds(l, nh, stride=D_CMP), :]
            kv = pair // (KV_W // LANES)
            a, b = _compress_pair(xl, wcat_ref.at[kv], w2_ref.at[kv], pe_ref, nh)
            halves.append(jnp.where(low, a, pltpu.roll(b, HEAD_DIM, 1)).astype(BF16))
        kc = jnp.concatenate(halves[:2], axis=1)
        vc = jnp.concatenate(halves[2:], axis=1)

        crow = lax.broadcasted_iota(jnp.int32, (nh, 1), 0)
        vis = crow * D_CMP + (L_CMP - 1) <= t
        (e,), l = _col_softmax_parts([_bdot(kc, qbd)], [vis])
        pc = jnp.where(vis, e * (1.0 / l), 0.0)
        o_c = _dot_tn(pc, vc)

        psg = pc
        for h in range(1, HPG):
            psg = psg + pltpu.roll(pc, LANES - h, 1)
        hi = psg.astype(BF16)
        lo = (psg - hi.astype(F32)).astype(BF16)
        imp_t = _bdot(ovlt_ref[...], hi) + _bdot(ovlt_ref[...], lo)
        imp = jnp.concatenate([jnp.transpose(imp_t[j * LANES:(j + 1) * LANES]) for j in range(nbl // LANES)], axis=1)
        sel = _select_blocks(imp, jnp.full((LANES, 1), t // L_SLC, jnp.int32))
        sel_h = _bdot(rep_ref[...], sel.astype(BF16))
        bias_t = jnp.concatenate([jnp.transpose(sel_h[:, j * LANES:(j + 1) * LANES]) for j in range(nbl // LANES)],
                                 axis=0)
        bias_t = jnp.where(bias_t > 0.5, 0.0, NEG_INF)

        bpc = SLC_CHUNK // L_SLC
        m = jnp.full((1, LANES), NEG_INF, F32)
        for c in range(past // SLC_CHUNK):
            s = _bdot(sbuf[c * SLC_CHUNK:(c + 1) * SLC_CHUNK, 0:KV_W], qbd)
            b = bias_t[c * bpc:(c + 1) * bpc]
            s = (s.reshape(bpc, L_SLC, LANES) + b[:, None, :]).reshape(SLC_CHUNK, LANES)
            sc_s[c * SLC_CHUNK:(c + 1) * SLC_CHUNK, :] = s
            m = jnp.maximum(m, s.max(axis=0, keepdims=True))
        knew = nslc_ref[0]
        s_new = _bdot(knew[:, :KV_W].astype(BF16), qbd)[0:1] + bias_t[past // L_SLC:past // L_SLC + 1]
        m = jnp.maximum(m, s_new)
        l = jnp.exp(s_new - m)
        acc = _new_key_term(l, knew)
        for c in range(past // SLC_CHUNK):
            e = jnp.exp(sc_s[c * SLC_CHUNK:(c + 1) * SLC_CHUNK, :] - m)
            l = l + e.sum(axis=0, keepdims=True)
            acc = acc + _dot_tn(e, sbuf[c * SLC_CHUNK:(c + 1) * SLC_CHUNK, KV_W:])
        inv = 1.0 / _rows_from_lanes(l)
        o_s = acc * jnp.concatenate([inv, inv], axis=1)

        sw = swin_ref[0]
        w_eff = sw.shape[0]
        kpos = past - w_eff + lax.broadcasted_iota(jnp.int32, (w_eff, 1), 0)
        ok = (kpos > t - WINDOW) & (kpos >= 0)
        wnew = nwin_ref[0]
        (e1, e2), l = _col_softmax_parts([_bdot(sw[:, :KV_W].astype(BF16), qbd),
                                          _bdot(wnew[:, :KV_W].astype(BF16), qbd)[0:1]], [ok, True])
        inv = 1.0 / _rows_from_lanes(l)
        o_w = (_dot_tn(e1, sw[:, KV_W:].astype(BF16)) + _new_key_term(e2, wnew)) * jnp.concatenate([inv, inv], axis=1)

        gt = gt_ref[0]
        o = gt[:, 0:1] * o_c + gt[:, 1:2] * o_s + gt[:, 2:3] * o_w
        o_ref[0] = o[0:N_HEADS]


def _sample_nsa(page_table, cache_cmp2, cache_slc2, qbd, gt, nslc, nwin, swin, cw):
    wcat, w2, pe8 = cw
    ns, n_pages = page_table.shape
    past = n_pages * PAGE_SIZE
    nbl = 2 * LANES
    ovlt = jnp.transpose(_overlap_matrix(past // D_CMP, nbl))
    hh = jnp.arange(LANES)
    rep = ((hh[None, :] == (hh[:, None] // HPG) * HPG) & (hh[:, None] < N_HEADS)).astype(BF16)
    page = pl.BlockSpec((1, PAGE_SIZE, 2 * KV_W), lambda n, p, pt: (pt[n, p], 0, 0))
    per_n = lambda a: pl.BlockSpec((1,) + a.shape[1:], lambda n, p, pt, nd=a.ndim: (n,) + (0,) * (nd - 1))
    const = lambda a: pl.BlockSpec(a.shape, lambda n, p, pt, nd=a.ndim: (0,) * nd)
    grid_spec = pltpu.PrefetchScalarGridSpec(
        num_scalar_prefetch=1, grid=(ns, n_pages),
        in_specs=[page, page, per_n(qbd), per_n(gt), per_n(nslc), per_n(nwin), per_n(swin),
                  const(wcat), const(w2), const(pe8), const(ovlt), const(rep)],
        out_specs=pl.BlockSpec((1, N_HEADS, KV_W), lambda n, p, pt: (n, 0, 0)),
        scratch_shapes=[pltpu.VMEM((2 * KV_W // LANES, past, LANES), F32), pltpu.VMEM((past, 2 * KV_W), BF16),
                        pltpu.VMEM((past, LANES), F32)])
    return pl.pallas_call(
        functools.partial(_sample_nsa_kernel, n_pages=n_pages),
        grid_spec=grid_spec,
        out_shape=jax.ShapeDtypeStruct((ns, N_HEADS, KV_W), F32),
        compiler_params=_cparams(("arbitrary", "arbitrary")), name="sample_nsa",
    )(page_table, cache_cmp2, cache_slc2, qbd, gt, nslc, nwin, swin, wcat, w2, pe8, ovlt, rep)


def kernel(x_prompt, x_sample, cache_cmp, cache_slc, state_win, state_conv, state_ffn, page_table, w_in, cmp_pe,
           cmp_k_w1, cmp_k_w2, cmp_v_w1, cmp_v_w2, conv_dw_w, conv_dw_b, conv_ln_g, conv_ln_b, conv_w_pw, nsa_w_o,
           w_out, ln1_g, ln1_b, ffn_w_up, ffn_dw_w, ffn_w_down, ln2_g, ln2_b):
    assert w_in.shape[0] == DEPTH == 1
    nb, s, _ = x_prompt.shape
    ns = x_sample.shape[0]
    n_pages = page_table.shape[1]
    past = n_pages * PAGE_SIZE
    n_pool = cache_cmp.shape[1]

    w_r = _reorder_w_in(w_in[0])
    cw = _compress_weights(cmp_pe[0], cmp_k_w1[0], cmp_k_w2[0], cmp_v_w1[0], cmp_v_w2[0])
    mw = _mix_weights(conv_dw_b[0], conv_ln_g[0], conv_ln_b[0], conv_w_pw[0], nsa_w_o[0], w_out[0], ln1_g[0], ln1_b[0])
    fw = _ffn_weights(ffn_w_up[0], ffn_dw_w[0], ffn_w_down[0], ln2_g[0], ln2_b[0])
    dww = conv_dw_w[0]

    xp = x_prompt.reshape(nb * s, D_MODEL)
    u, q, cmp_p, slc_p, win_p, gm, gb, kaug, vs, kw, vw = _proj(
        xp, w_r, _rope_tables(jnp.arange(s)), tm=256, seq_len=s, n_seq=nb, emit_att=True)
    kcv = _compress_prompt(cmp_p, cw, n_seq=nb, seq_len=s)
    ocg, sb = _select_prompt(q, gb, kcv, n_seq=nb, seq_len=s, qs=256)
    o = _attn_prompt(q, gb, ocg, sb, kaug, vs, kw, vw, n_seq=nb, seq_len=s, tk=1024, nsplit=1)
    h = _mix_prompt(u, o, gm, xp, dww, mw, n_seq=nb, seq_len=s, ts=256)
    y_p, up_tail = _ffn_prompt(h, fw, n_seq=nb, seq_len=s, ts=256)

    kv_shape = lambda n, t: (1, n, t, 2, N_KV, HEAD_DIM)
    w_keep = min(WINDOW, s)
    y_prompt = y_p.reshape(nb, s, D_MODEL)
    cmp_prompt = cmp_p.reshape(kv_shape(nb, s))
    slc_prompt = slc_p.reshape(kv_shape(nb, s))
    win_prompt = win_p.reshape(nb, s, 2 * KV_W)[:, s - w_keep:].reshape(kv_shape(nb, w_keep))
    conv_prompt = u.reshape(nb, s, C_CONV)[:, s - (CONV_K - 1):][None]
    ffn_prompt = up_tail[:, SUBLANES - (FFN_K - 1):][None]

    xs = x_sample.reshape(ns, D_MODEL)
    u_s, q_s, cmp_s, slc_s, win_s, gm_s, gb_s = _proj(
        xs, w_r, _rope_tables(jnp.full((ns,), past)), tm=ns, seq_len=ns, n_seq=1, emit_att=False)
    qt = jnp.transpose(q_s.reshape(ns, N_KV, HPG, HEAD_DIM), (0, 1, 3, 2))
    qbd = qt[:, :, :, None, :] * jnp.eye(N_KV, dtype=BF16)[None, :, None, :, None]
    qbd = jnp.pad(qbd.reshape(ns, KV_W, N_HEADS), ((0, 0), (0, 0), (0, LANES - N_HEADS)))
    gt = jnp.transpose(gb_s.reshape(ns, 3, N_HEADS), (0, 2, 1))
    gt = jnp.pad(gt, ((0, 0), (0, LANES - N_HEADS), (0, LANES - 3)))
    pad_rows = lambda a: jnp.pad(a[:, None, :], ((0, 0), (0, SUBLANES - 1), (0, 0)))
    w_eff = state_win.shape[2]
    o_s = _sample_nsa(page_table, cache_cmp[0].reshape(n_pool, PAGE_SIZE, 2 * KV_W),
                      cache_slc[0].reshape(n_pool, PAGE_SIZE, 2 * KV_W), qbd, gt, pad_rows(slc_s), pad_rows(win_s),
                      state_win[0].reshape(ns, w_eff, 2 * KV_W), cw)
    o_s = o_s.reshape(ns, N_KV, HPG, N_KV, HEAD_DIM)
    o_s = jnp.stack([o_s[:, g, :, g] for g in range(N_KV)], axis=1).reshape(ns, N_HEADS * HEAD_DIM).astype(BF16)
    h_s = _mix_sample(u_s, jnp.transpose(state_conv[0], (1, 0, 2)), o_s, gm_s, xs, dww, mw)
    y_s, up_s = _ffn_sample(h_s, state_ffn[0][:, 0], state_ffn[0][:, 1], fw)

    y_sample = y_s.reshape(ns, 1, D_MODEL)
    cmp_sample = cmp_s.reshape(kv_shape(ns, 1))
    slc_sample = slc_s.reshape(kv_shape(ns, 1))
    win_all = jnp.concatenate([state_win[0], win_s.reshape(ns, 1, 2, N_KV, HEAD_DIM)], axis=1)
    win_sample = win_all[:, -w_eff:][None]
    conv_sample = jnp.concatenate([state_conv[0], u_s[:, None]], axis=1)[:, -(CONV_K - 1):][None]
    ffn_sample = jnp.concatenate([state_ffn[0], up_s[:, None]], axis=1)[:, -(FFN_K - 1):][None]
    return (y_prompt, y_sample, cmp_prompt, cmp_sample, slc_prompt, slc_sample, win_prompt, win_sample,
            conv_prompt, conv_sample, ffn_prompt, ffn_sample)
```

```python
import functools

import jax
import jax.numpy as jnp
from jax import lax
from jax.experimental import pallas as pl
from jax.experimental.pallas import tpu as pltpu

F32 = jnp.float32
BF16 = jnp.bfloat16

D_MODEL = 1024
N_HEADS = 16
HEAD_DIM = 64
N_KV = 4
HPG = N_HEADS // N_KV
KV_W = N_KV * HEAD_DIM
ROT_DIM = HEAD_DIM // 4
ROPE_THETA = 500000.0
L_CMP = 32
D_CMP = 16
L_SLC = 64
N_SEL = 16
N_LOCAL = 2
WINDOW = 512
CMP_HID = 4 * HEAD_DIM
Q_BLOCK = 128
C_CONV = D_MODEL
CONV_K = 31
D_FF = 2816
FFN_K = 3
DEPTH = 1
DN_ALPHA = (2 * DEPTH) ** 0.25
LN_EPS = 1e-5
NEG_INF = -1e30
PAGE_SIZE = 128
LOG2E = 1.4426950408889634

LANES = 128
SUBLANES = 8
VMEM_LIMIT = 56 * 1024 * 1024

O_A = 0
O_B = C_CONV
O_Q = 2 * C_CONV
O_KV = O_Q + N_HEADS * HEAD_DIM
O_GM = O_KV + 6 * KV_W
O_GB = O_GM + 2 * D_MODEL
N_GB = 3 * N_HEADS
N_PROJ = O_GB + LANES


def _cparams(sem):
    return pltpu.CompilerParams(dimension_semantics=sem, vmem_limit_bytes=VMEM_LIMIT)


def _resident(shape):
    nd = len(shape)
    return pl.BlockSpec(shape, lambda *_: (0,) * nd, pipeline_mode=pl.Buffered(1))


def _bdot(a, b):
    return jnp.dot(a, b, preferred_element_type=F32)


def _dot_nt(a, b):
    return lax.dot_general(a, b, (((1,), (1,)), ((), ())), preferred_element_type=F32)


def _rope_tables(pos):
    half = ROT_DIM // 2
    inv = ROPE_THETA ** (-jnp.arange(half, dtype=F32) * 2.0 / ROT_DIM)
    ang = pos.astype(F32)[:, None] * inv[None, :]
    cos, sin = jnp.cos(ang), jnp.sin(ang)
    r = pos.shape[0]
    one = jnp.ones((r, HEAD_DIM - ROT_DIM), F32)
    zero = jnp.zeros((r, HEAD_DIM - ROT_DIM), F32)
    zh = jnp.zeros((r, half), F32)
    c = jnp.concatenate([cos, cos, one], axis=1)
    sa = jnp.concatenate([-sin, zh, zero], axis=1)
    sb = jnp.concatenate([zh, sin, zero], axis=1)
    rep = LANES // HEAD_DIM
    return jnp.tile(c, (1, rep)), jnp.tile(sa, (1, rep)), jnp.tile(sb, (1, rep))


def _rope(x, c, sa, sb):
    return x * c + pltpu.roll(x, LANES - ROT_DIM // 2, 1) * sa + pltpu.roll(x, ROT_DIM // 2, 1) * sb


def _proj_kernel(x_ref, w_ref, c_ref, sa_ref, sb_ref,
                 u_ref, q_ref, cmp_ref, slc_ref, win_ref, gm_ref, gb_ref, *att_refs,
                 tm, tiles_per_seq):
    xb = x_ref[...].astype(BF16)
    c, sa, sb = c_ref[...], sa_ref[...], sb_ref[...]
    cw = 2 * LANES

    def z(col, width=cw):
        return _bdot(xb, w_ref[:, col:col + width])

    for j in range(C_CONV // cw):
        u_ref[:, j * cw:(j + 1) * cw] = z(O_A + j * cw) * jax.nn.sigmoid(z(O_B + j * cw))

    def roped(col):
        zz = z(col)
        return [_rope(zz[:, h * LANES:(h + 1) * LANES], c, sa, sb) for h in range(cw // LANES)]

    scale = HEAD_DIM ** -0.5 * LOG2E
    for j in range(N_HEADS * HEAD_DIM // cw):
        halves = roped(O_Q + j * cw)
        for h, v in enumerate(halves):
            q_ref[:, j * cw + h * LANES:j * cw + (h + 1) * LANES] = (v * scale).astype(BF16)

    if att_refs:
        kaug_ref, vs_ref, kw_ref, vw_ref = att_refs
        lane = lax.broadcasted_iota(jnp.int32, (tm, LANES), 1)
        row = lax.broadcasted_iota(jnp.int32, (tm, LANES), 0)
        pos0 = (pl.program_id(0) % tiles_per_seq) * tm
        onehot = jnp.where(lane == (pos0 + row) // L_SLC, 1.0, 0.0).astype(BF16)
        low = lane < HEAD_DIM

        def per_group(halves, fill=0.0):
            out = []
            for g in range(N_KV):
                v = halves[g // 2]
                if g % 2:
                    v = pltpu.roll(v, HEAD_DIM, 1)
                out.append(jnp.where(low, v, fill).astype(BF16))
            return out

        ones_col = jnp.where(lane == HEAD_DIM, 1.0, 0.0)

    for br, dst in enumerate((cmp_ref, slc_ref, win_ref)):
        kh = roped(O_KV + br * 2 * KV_W)
        zv = z(O_KV + br * 2 * KV_W + KV_W)
        vh = [zv[:, h * LANES:(h + 1) * LANES] for h in range(cw // LANES)]
        for h in range(cw // LANES):
            dst[:, h * LANES:(h + 1) * LANES] = kh[h]
            dst[:, KV_W + h * LANES:KV_W + (h + 1) * LANES] = vh[h]
        if att_refs and br == 1:
            for g, (kg, vg) in enumerate(zip(per_group(kh), per_group(vh, ones_col))):
                kaug_ref[0, g, :, 0:LANES] = onehot
                kaug_ref[0, g, :, LANES:2 * LANES] = kg
                vs_ref[0, g] = vg
        if att_refs and br == 2:
            for g, (kg, vg) in enumerate(zip(per_group(kh), per_group(vh))):
                kw_ref[0, g] = kg
                vw_ref[0, g] = vg

    for j in range(2 * D_MODEL // cw):
        gm_ref[:, j * cw:(j + 1) * cw] = jax.nn.sigmoid(z(O_GM + j * cw))
    gb_ref[...] = jax.nn.sigmoid(z(O_GB, LANES))[:, :N_GB]


def _proj(x2d, w_r, tabs, *, tm, seq_len, n_seq, emit_att):
    r = x2d.shape[0]
    tps = seq_len // tm
    grid = (r // tm,)
    row = lambda w: pl.BlockSpec((tm, w), lambda i: (i, 0))
    tab = pl.BlockSpec((tm, LANES), lambda i: (i % tps, 0))
    out_shape = [jax.ShapeDtypeStruct((r, C_CONV), F32), jax.ShapeDtypeStruct((r, N_HEADS * HEAD_DIM), BF16),
                 jax.ShapeDtypeStruct((r, 2 * KV_W), F32), jax.ShapeDtypeStruct((r, 2 * KV_W), F32),
                 jax.ShapeDtypeStruct((r, 2 * KV_W), F32), jax.ShapeDtypeStruct((r, 2 * D_MODEL), F32),
                 jax.ShapeDtypeStruct((r, N_GB), F32)]
    out_specs = [row(C_CONV), row(N_HEADS * HEAD_DIM), row(2 * KV_W), row(2 * KV_W), row(2 * KV_W),
                 row(2 * D_MODEL), row(N_GB)]
    if emit_att:
        for w in (2 * LANES, LANES, LANES, LANES):
            out_shape.append(jax.ShapeDtypeStruct((n_seq, N_KV, seq_len, w), BF16))
            out_specs.append(pl.BlockSpec((1, N_KV, tm, w), lambda i: (i // tps, 0, i % tps, 0)))
    return pl.pallas_call(
        functools.partial(_proj_kernel, tm=tm, tiles_per_seq=tps),
        grid=grid,
        in_specs=[row(D_MODEL), _resident(w_r.shape), tab, tab, tab],
        out_specs=out_specs, out_shape=out_shape,
        compiler_params=_cparams(("arbitrary",)), name="proj",
    )(x2d, w_r, *tabs)


def _reorder_w_in(w_in):
    pad = jnp.zeros((D_MODEL, LANES - N_GB), w_in.dtype)
    o3 = 2 * C_CONV + N_HEADS * HEAD_DIM + 6 * KV_W
    return jnp.concatenate([w_in[:, :o3], w_in[:, o3 + N_GB:], w_in[:, o3:o3 + N_GB], pad], axis=1).astype(BF16)


def _compress_pair(xl, wcat_ref, w2_ref, pe_ref, nh):
    lane = lax.broadcasted_iota(jnp.int32, (nh, LANES), 1)
    low = lane < HEAD_DIM
    even, odd = [], []
    for l in range(0, D_CMP, 2):
        a, b = xl(l), xl(l + 1)
        even.append(jnp.where(low, a, pltpu.roll(b, HEAD_DIM, 1)).astype(BF16))
        odd.append(jnp.where(low, pltpu.roll(a, HEAD_DIM, 1), b).astype(BF16))
    bias2 = _bdot(pe_ref[...], wcat_ref[...])
    bias = bias2[0:1, :CMP_HID] + bias2[1:2, CMP_HID:]
    outs = []
    for x_parts in (even, odd):
        ab = _bdot(jnp.concatenate(x_parts, axis=1), wcat_ref[...])
        h = ab[:, :CMP_HID] + pltpu.roll(ab[:, CMP_HID:], nh - 1, 0) + bias
        outs.append(_bdot(jax.nn.gelu(h).astype(BF16), w2_ref[...]))
    return outs


def _compress_kernel(x_ref, wcat_ref, w2_ref, pe_ref, o_ref, *, nh):
    xl = lambda l: x_ref[0, pl.ds(l, nh, stride=D_CMP), :]
    for j, v in enumerate(_compress_pair(xl, wcat_ref.at[0], w2_ref.at[0], pe_ref, nh)):
        o_ref[0, j] = v.astype(BF16)


def _compress_weights(cmp_pe, k_w1, k_w2, v_w1, v_w2):
    def cat(w1):
        return jnp.concatenate([w1[:D_CMP].reshape(D_CMP * HEAD_DIM, CMP_HID),
                                w1[D_CMP:].reshape(D_CMP * HEAD_DIM, CMP_HID)], axis=1)
    wcat = jnp.stack([cat(k_w1), cat(v_w1)]).astype(BF16)
    pad = jnp.zeros((CMP_HID, LANES - HEAD_DIM), F32)
    w2 = jnp.stack([jnp.concatenate([k_w2, pad], 1), jnp.concatenate([v_w2, pad], 1)]).astype(BF16)
    pe2 = cmp_pe.reshape(2, D_CMP * HEAD_DIM)
    pe8 = jnp.concatenate([pe2, jnp.zeros((SUBLANES - 2, D_CMP * HEAD_DIM), F32)], 0).astype(BF16)
    return wcat, w2, pe8


def _compress_prompt(cmp2d, cw, *, n_seq, seq_len):
    wcat, w2, pe8 = cw
    nh = seq_len // D_CMP
    x3 = cmp2d.reshape(n_seq, seq_len, 2 * KV_W)
    npair = 2 * KV_W // LANES
    return pl.pallas_call(
        functools.partial(_compress_kernel, nh=nh),
        grid=(n_seq, npair),
        in_specs=[pl.BlockSpec((1, seq_len, LANES), lambda n, p: (n, 0, p)),
                  pl.BlockSpec((1,) + wcat.shape[1:], lambda n, p: (p // (npair // 2), 0, 0)),
                  pl.BlockSpec((1,) + w2.shape[1:], lambda n, p: (p // (npair // 2), 0, 0)),
                  pl.BlockSpec(pe8.shape, lambda n, p: (0, 0))],
        out_specs=pl.BlockSpec((1, 2, nh, LANES), lambda n, p: (n, p, 0, 0)),
        out_shape=jax.ShapeDtypeStruct((n_seq, 2 * N_KV, nh, LANES), BF16),
        compiler_params=_cparams(("arbitrary", "arbitrary")), name="compress_prompt",
    )(x3, wcat, w2, pe8)


BIG = 1e30


def _overlap_matrix(ncp, nb_lanes):
    start = jnp.arange(ncp) * D_CMP
    bstart = jnp.arange(nb_lanes) * L_SLC
    ov = (start[:, None] < bstart[None, :] + L_SLC) & (start[:, None] + L_CMP > bstart[None, :])
    return ov.astype(BF16)


def _split_heads(x):
    r = x.shape[0]
    low = lax.broadcasted_iota(jnp.int32, (r, LANES), 1) < HEAD_DIM
    out = []
    for h in range(HPG):
        v = x[:, (h // 2) * LANES:(h // 2 + 1) * LANES]
        if h % 2:
            v = pltpu.roll(v, HEAD_DIM, 1)
        out.append(jnp.where(low, v, 0.0))
    return out


def _merge_heads(parts):
    r = parts[0].shape[0]
    low = lax.broadcasted_iota(jnp.int32, (r, LANES), 1) < HEAD_DIM
    return [jnp.where(low, parts[2 * p], pltpu.roll(parts[2 * p + 1], HEAD_DIM, 1)) for p in range(HPG // 2)]


def _softmax(s):
    m = jnp.max(s, axis=-1, keepdims=True)
    e = jnp.exp2(s - m)
    return e * (1.0 / jnp.sum(e, axis=-1, keepdims=True))


def _select_blocks(imp, cur):
    r, nbl = imp.shape
    blk = lax.broadcasted_iota(jnp.int32, (r, nbl), 1)
    blkf = blk.astype(F32)
    forced = (blk == 0) | ((blk <= cur) & (blk > cur - N_LOCAL))
    v = jnp.where(forced, BIG, imp)
    v = jnp.where(blk <= cur, v, -BIG)
    sel = jnp.zeros((r, nbl), F32)
    for _ in range(N_SEL):
        m = jnp.max(v, axis=-1, keepdims=True)
        idx = jnp.min(jnp.where(v == m, blkf, float(nbl)), axis=-1, keepdims=True)
        hit = blkf == idx
        sel = jnp.where(hit & (v > -BIG), 1.0, sel)
        v = jnp.where(hit, -BIG, v)
    return sel


def _importance(psum, ovl):
    hi = psum.astype(BF16)
    lo = (psum - hi.astype(F32)).astype(BF16)
    return _bdot(hi, ovl) + _bdot(lo, ovl)


def _head_gates(gb, g, br, h):
    col = lax.broadcasted_iota(jnp.int32, gb.shape, 1)
    return jnp.sum(jnp.where(col == br * N_HEADS + g * HPG + h, gb, 0.0), axis=-1, keepdims=True)


def _select_kernel(q_ref, gb_ref, ovl_ref, kcv_ref, ocg_ref, sb_ref, *, qs, ncp):
    i = pl.program_id(1)
    tq = i * qs + lax.broadcasted_iota(jnp.int32, (qs, 1), 0)
    tq4 = jnp.concatenate([tq] * HPG, axis=0)
    c_end = lax.broadcasted_iota(jnp.int32, (1, ncp), 1) * D_CMP + (L_CMP - 1)
    vis = c_end <= tq4
    gb = gb_ref[...]
    gw = HPG * HEAD_DIM
    imps = []
    for g in range(N_KV):
        qf = q_ref[:, g * gw:(g + 1) * gw].astype(F32)
        q4 = jnp.concatenate([v.astype(BF16) for v in _split_heads(qf)], axis=0)
        s = _dot_nt(q4, kcv_ref[0, g])
        p = jnp.where(vis, _softmax(jnp.where(vis, s, NEG_INF)), 0.0)
        o_c = _bdot(p.astype(BF16), kcv_ref[0, N_KV + g])
        psum = p[0:qs]
        for h in range(1, HPG):
            psum = psum + p[h * qs:(h + 1) * qs]
        imps.append(_importance(psum, ovl_ref[...]))
        outs = [_head_gates(gb, g, 0, h) * o_c[h * qs:(h + 1) * qs] for h in range(HPG)]
        for pr, v in enumerate(_merge_heads(outs)):
            ocg_ref[:, g * gw + pr * LANES:g * gw + (pr + 1) * LANES] = v
    sel = _select_blocks(jnp.concatenate(imps, axis=0), jnp.concatenate([tq // L_SLC] * N_KV, axis=0))
    sb = jnp.where(sel > 0.0, 0.0, NEG_INF).astype(BF16)
    for g in range(N_KV):
        sb_ref[0, g] = sb[g * qs:(g + 1) * qs]


def _select_prompt(q2d, gb2d, kcv, *, n_seq, seq_len, qs):
    ncp = seq_len // D_CMP
    nq = seq_len // qs
    ovl = _overlap_matrix(ncp, LANES)
    row = lambda w: pl.BlockSpec((qs, w), lambda n, i: (n * nq + i, 0))
    return pl.pallas_call(
        functools.partial(_select_kernel, qs=qs, ncp=ncp),
        grid=(n_seq, nq),
        in_specs=[row(N_HEADS * HEAD_DIM), row(N_GB), pl.BlockSpec(ovl.shape, lambda n, i: (0, 0)),
                  pl.BlockSpec((1,) + kcv.shape[1:], lambda n, i: (n, 0, 0, 0))],
        out_specs=[row(N_HEADS * HEAD_DIM), pl.BlockSpec((1, N_KV, qs, LANES), lambda n, i: (n, 0, i, 0))],
        out_shape=[jax.ShapeDtypeStruct((n_seq * seq_len, N_HEADS * HEAD_DIM), F32),
                   jax.ShapeDtypeStruct((n_seq, N_KV, seq_len, LANES), BF16)],
        compiler_params=_cparams(("arbitrary", "arbitrary")), name="select_prompt",
    )(q2d, gb2d, ovl, kcv)


def _attn_kernel(q_ref, gb_ref, ocg_ref, sb_ref, kaug_ref, vs_ref, kw_ref, vw_ref, o_ref, *, tk):
    g = pl.program_id(1)
    i = pl.program_id(2)
    rows = HPG * Q_BLOCK
    q4 = jnp.concatenate([v.astype(BF16) for v in _split_heads(q_ref[...].astype(F32))], axis=0)
    tq = i * Q_BLOCK + lax.broadcasted_iota(jnp.int32, (Q_BLOCK, 1), 0)
    tq4 = jnp.concatenate([tq] * HPG, axis=0)
    q_aug = jnp.concatenate([jnp.concatenate([sb_ref[0, 0]] * HPG, axis=0), q4], axis=1)

    def tile(j, carry, causal):
        m, acc = carry
        off = pl.multiple_of(j * tk, tk)
        s = _dot_nt(q_aug, kaug_ref[0, 0, pl.ds(off, tk), :])
        if causal:
            kpos = off + lax.broadcasted_iota(jnp.int32, (1, tk), 1)
            s = jnp.where(kpos <= tq4, s, NEG_INF)
        m_new = jnp.maximum(m, jnp.max(s, axis=-1, keepdims=True))
        e = jnp.exp2((s - m_new).astype(BF16))
        return m_new, jnp.exp2(m - m_new) * acc + _bdot(e, vs_ref[0, 0, pl.ds(off, tk), :])

    init = (jnp.full((rows, 1), NEG_INF, F32), jnp.zeros((rows, LANES), F32))
    j_last = (i * Q_BLOCK) // tk
    _, acc = tile(j_last, lax.fori_loop(0, j_last, lambda j, c: tile(j, c, False), init), True)
    o_s = acc * (1.0 / acc[:, HEAD_DIM:HEAD_DIM + 1])

    wlen = WINDOW + Q_BLOCK
    start = pl.multiple_of(jnp.maximum(i - WINDOW // Q_BLOCK, 0) * Q_BLOCK, Q_BLOCK)
    s = _dot_nt(q4, kw_ref[0, 0, pl.ds(start, wlen), :])
    kpos = start + lax.broadcasted_iota(jnp.int32, (1, wlen), 1)
    ok = (kpos <= tq4) & (kpos > tq4 - WINDOW)
    o_w = _bdot(_softmax(jnp.where(ok, s, NEG_INF)).astype(BF16), vw_ref[0, 0, pl.ds(start, wlen), :])

    gb = gb_ref[...]
    outs = []
    for h in range(HPG):
        sl = slice(h * Q_BLOCK, (h + 1) * Q_BLOCK)
        outs.append(_head_gates(gb, g, 1, h) * o_s[sl] + _head_gates(gb, g, 2, h) * o_w[sl])
    for pr, v in enumerate(_merge_heads(outs)):
        o_ref[:, pr * LANES:(pr + 1) * LANES] = (ocg_ref[:, pr * LANES:(pr + 1) * LANES] + v).astype(BF16)


def _attn_prompt(q2d, gb2d, ocg, sb, kaug, vs, kw, vw, *, n_seq, seq_len, tk):
    nq = seq_len // Q_BLOCK
    gw = HPG * HEAD_DIM
    per_ng = lambda a: pl.BlockSpec((1, 1) + a.shape[2:], lambda n, g, i: (n, g, 0, 0))
    qblk = pl.BlockSpec((Q_BLOCK, gw), lambda n, g, i: (n * nq + i, g))
    return pl.pallas_call(
        functools.partial(_attn_kernel, tk=tk),
        grid=(n_seq, N_KV, nq),
        in_specs=[qblk, pl.BlockSpec((Q_BLOCK, N_GB), lambda n, g, i: (n * nq + i, 0)), qblk,
                  pl.BlockSpec((1, 1, Q_BLOCK, LANES), lambda n, g, i: (n, g, i, 0)),
                  per_ng(kaug), per_ng(vs), per_ng(kw), per_ng(vw)],
        out_specs=qblk,
        out_shape=jax.ShapeDtypeStruct((n_seq * seq_len, N_HEADS * HEAD_DIM), BF16),
        compiler_params=_cparams(("arbitrary", "arbitrary", "arbitrary")), name="attn_prompt",
    )(q2d, gb2d, ocg, sb, kaug, vs, kw, vw)


def _layer_norm(x, g, b):
    mu = jnp.mean(x, axis=-1, keepdims=True)
    xc = x - mu
    var = jnp.mean(xc * xc, axis=-1, keepdims=True)
    return xc * lax.rsqrt(var + LN_EPS) * g + b


def _mix_tail(cv, o_ref, gm_ref, x_ref, w, h_ref):
    dwb, clg, clb, pw, wo, wout, l1g, l1b = w
    hc = _layer_norm(cv + dwb[...], clg[...], clb[...])
    act = hc * jax.nn.sigmoid(hc)
    yc = _bdot(act.astype(BF16), pw[...])
    yo = _bdot(o_ref[...], wo[...])
    merged = gm_ref[:, :D_MODEL] * yc + gm_ref[:, D_MODEL:] * yo
    hh = DN_ALPHA * x_ref[...] + _bdot(merged.astype(BF16), wout[...])
    h_ref[...] = _layer_norm(hh, l1g[...], l1b[...])


CONV_HALO = 32
CONV_ROWS = 32


def _mix_prompt_kernel(u_ref, halo_ref, o_ref, gm_ref, x_ref, dww_ref, *rest, ts):
    w, h_ref, win_s, cv_s = rest[:8], rest[8], rest[9], rest[10]
    i = pl.program_id(1)

    @pl.when(i == 0)
    def _():
        win_s[0, 0:CONV_HALO] = jnp.zeros((CONV_HALO, C_CONV), F32)

    @pl.when(i > 0)
    def _():
        win_s[0, 0:CONV_HALO] = halo_ref[0]

    win_s[0, CONV_HALO:CONV_HALO + ts] = u_ref[0]
    span = ts + CONV_HALO - SUBLANES
    for r in range(1, SUBLANES):
        win_s[r, 0:span] = win_s[0, r:r + span]
    first = CONV_HALO - (CONV_K - 1)

    for c in range(ts // CONV_ROWS):
        acc = jnp.zeros((CONV_ROWS, C_CONV), F32)
        for k in range(CONV_K):
            r, a = (first + k) % SUBLANES, (first + k) // SUBLANES
            base = c * CONV_ROWS + a * SUBLANES
            acc = acc + win_s[r, base:base + CONV_ROWS, :] * dww_ref[k:k + 1, :]
        cv_s[c * CONV_ROWS:(c + 1) * CONV_ROWS, :] = acc
    _mix_tail(cv_s[...], o_ref, gm_ref, x_ref, w, h_ref)


def _mix_sample_kernel(u_ref, st_ref, o_ref, gm_ref, x_ref, dww_ref, *rest):
    w, h_ref = rest[:8], rest[8]
    cv = u_ref[...] * dww_ref[CONV_K - 1:CONV_K, :]
    for k in range(CONV_K - 1):
        cv = cv + st_ref[k] * dww_ref[k:k + 1, :]
    _mix_tail(cv, o_ref, gm_ref, x_ref, w, h_ref)


def _mix_weights(conv_dw_b, conv_ln_g, conv_ln_b, conv_w_pw, nsa_w_o, w_out, ln1_g, ln1_b):
    r = lambda v: v.reshape(1, -1)
    return (r(conv_dw_b), r(conv_ln_g), r(conv_ln_b), conv_w_pw.astype(BF16), nsa_w_o.astype(BF16),
            w_out.astype(BF16), r(ln1_g), r(ln1_b))


def _mix_prompt(u2d, o2d, gm2d, x2d, dww, mw, *, n_seq, seq_len, ts):
    tps = seq_len // ts
    hpt = ts // CONV_HALO
    u3 = u2d.reshape(n_seq, seq_len, C_CONV)
    row = lambda w: pl.BlockSpec((ts, w), lambda n, i: (n * tps + i, 0))
    return pl.pallas_call(
        functools.partial(_mix_prompt_kernel, ts=ts),
        grid=(n_seq, tps),
        in_specs=[pl.BlockSpec((1, ts, C_CONV), lambda n, i: (n, i, 0)),
                  pl.BlockSpec((1, CONV_HALO, C_CONV), lambda n, i: (n, jnp.maximum(i * hpt - 1, 0), 0)),
                  row(N_HEADS * HEAD_DIM), row(2 * D_MODEL), row(D_MODEL), _resident(dww.shape)]
                 + [_resident(a.shape) for a in mw],
        out_specs=row(D_MODEL),
        out_shape=jax.ShapeDtypeStruct((n_seq * seq_len, D_MODEL), F32),
        scratch_shapes=[pltpu.VMEM((SUBLANES, CONV_HALO + ts, C_CONV), F32), pltpu.VMEM((ts, C_CONV), F32)],
        compiler_params=_cparams(("arbitrary", "arbitrary")), name="mix_prompt",
    )(u3, u3, o2d, gm2d, x2d, dww, *mw)


def _mix_sample(u2d, st_t, o2d, gm2d, x2d, dww, mw):
    args = (u2d, st_t, o2d, gm2d, x2d, dww) + tuple(mw)
    full = lambda a: pl.BlockSpec(a.shape, lambda i, nd=a.ndim: (0,) * nd)
    return pl.pallas_call(
        _mix_sample_kernel, grid=(1,),
        in_specs=[full(a) for a in args],
        out_specs=pl.BlockSpec(x2d.shape, lambda i: (0, 0)),
        out_shape=jax.ShapeDtypeStruct(x2d.shape, F32),
        compiler_params=_cparams(("arbitrary",)), name="mix_sample",
    )(*args)


FFN_CW = 256


def _ffn_down(conv, h, wdown_ref, l2g_ref, l2b_ref, y_ref):
    acc = jnp.zeros(h.shape, F32)
    for c in range(D_FF // FFN_CW):
        val, gate = conv(c * FFN_CW), conv(D_FF + c * FFN_CW)
        act = gate * jax.nn.sigmoid(gate) * val
        acc = acc + _bdot(act.astype(BF16), wdown_ref[c * FFN_CW:(c + 1) * FFN_CW, :])
    y_ref[...] = _layer_norm(DN_ALPHA * h + acc, l2g_ref[...], l2b_ref[...])


def _ffn_prompt_kernel(h_ref, wup_ref, dw_ref, wdown_ref, l2g_ref, l2b_ref, y_ref, tail_ref, up_s, *, ts):
    i = pl.program_id(1)

    @pl.when(i == 0)
    def _():
        up_s[0:SUBLANES] = jnp.zeros((SUBLANES, 2 * D_FF), F32)

    @pl.when(i > 0)
    def _():
        up_s[0:SUBLANES] = up_s[ts:ts + SUBLANES]

    h = h_ref[...]
    hb = h.astype(BF16)
    cw = 2 * FFN_CW
    for c in range(2 * D_FF // cw):
        up_s[SUBLANES:SUBLANES + ts, c * cw:(c + 1) * cw] = _bdot(hb, wup_ref[:, c * cw:(c + 1) * cw])
    tail_ref[0] = up_s[ts:ts + SUBLANES]

    def conv(col):
        sl = slice(col, col + FFN_CW)
        out = up_s[SUBLANES:SUBLANES + ts, sl] * dw_ref[FFN_K - 1:FFN_K, sl]
        for k in range(FFN_K - 1):
            off = SUBLANES - (FFN_K - 1) + k
            out = out + up_s[off:off + ts, sl] * dw_ref[k:k + 1, sl]
        return out

    _ffn_down(conv, h, wdown_ref, l2g_ref, l2b_ref, y_ref)


def _ffn_sample_kernel(h_ref, s0_ref, s1_ref, wup_ref, dw_ref, wdown_ref, l2g_ref, l2b_ref, y_ref, up_ref):
    h = h_ref[...]
    hb = h.astype(BF16)
    cw = 2 * FFN_CW
    for c in range(2 * D_FF // cw):
        up_ref[:, c * cw:(c + 1) * cw] = _bdot(hb, wup_ref[:, c * cw:(c + 1) * cw])

    def conv(col):
        sl = slice(col, col + FFN_CW)
        return s0_ref[:, sl] * dw_ref[0:1, sl] + s1_ref[:, sl] * dw_ref[1:2, sl] + up_ref[:, sl] * dw_ref[2:3, sl]

    _ffn_down(conv, h, wdown_ref, l2g_ref, l2b_ref, y_ref)


def _ffn_prompt(h2d, fw, *, n_seq, seq_len, ts):
    tps = seq_len // ts
    row = pl.BlockSpec((ts, D_MODEL), lambda n, i: (n * tps + i, 0))
    return pl.pallas_call(
        functools.partial(_ffn_prompt_kernel, ts=ts),
        grid=(n_seq, tps),
        in_specs=[row] + [_resident(a.shape) for a in fw],
        out_specs=[row, pl.BlockSpec((1, SUBLANES, 2 * D_FF), lambda n, i: (n, 0, 0))],
        out_shape=[jax.ShapeDtypeStruct((n_seq * seq_len, D_MODEL), F32),
                   jax.ShapeDtypeStruct((n_seq, SUBLANES, 2 * D_FF), F32)],
        scratch_shapes=[pltpu.VMEM((SUBLANES + ts, 2 * D_FF), F32)],
        compiler_params=_cparams(("arbitrary", "arbitrary")), name="ffn_prompt",
    )(h2d, *fw)


def _ffn_sample(h2d, s0, s1, fw):
    args = (h2d, s0, s1) + tuple(fw)
    full = lambda a: pl.BlockSpec(a.shape, lambda i, nd=a.ndim: (0,) * nd)
    return pl.pallas_call(
        _ffn_sample_kernel, grid=(1,),
        in_specs=[full(a) for a in args],
        out_specs=[full(h2d), full(s0)],
        out_shape=[jax.ShapeDtypeStruct(h2d.shape, F32), jax.ShapeDtypeStruct(s0.shape, F32)],
        compiler_params=_cparams(("arbitrary",)), name="ffn_sample",
    )(*args)


def _ffn_weights(ffn_w_up, ffn_dw_w, ffn_w_down, ln2_g, ln2_b):
    return (ffn_w_up.astype(BF16), ffn_dw_w, ffn_w_down.astype(BF16), ln2_g.reshape(1, -1), ln2_b.reshape(1, -1))


SLC_CHUNK = 2048


def _dot_tn(a, b):
    return _bdot(jnp.transpose(a).astype(BF16), b)


def _col_softmax_parts(parts, masks):
    masked = [jnp.where(mk, s, NEG_INF) for s, mk in zip(parts, masks)]
    m = masked[0].max(axis=0, keepdims=True)
    for s in masked[1:]:
        m = jnp.maximum(m, s.max(axis=0, keepdims=True))
    es = [jnp.exp2(s - m) for s in masked]
    l = es[0].sum(axis=0, keepdims=True)
    for e in es[1:]:
        l = l + e.sum(axis=0, keepdims=True)
    return es, l


def _rows_from_lanes(l):
    return jnp.transpose(jnp.broadcast_to(l, (LANES, LANES)))


def _new_key_term(e, kv_row):
    er = _rows_from_lanes(e)
    return jnp.concatenate([er] * (KV_W // LANES), axis=1) * kv_row[0:1, KV_W:]


PAGES_PER_STEP = 8


def _sample_nsa_kernel(pt_ref, *refs, n_pages):
    pps = PAGES_PER_STEP
    cmp_refs, slc_refs = refs[:pps], refs[pps:2 * pps]
    (qbd_ref, gt_ref, nslc_ref, nwin_ref, swin_ref, wcat_ref, w2_ref, pe_ref, ovlt_ref, rep_ref,
     o_ref, cbuf, sbuf, sc_s) = refs[2 * pps:]
    p = pl.program_id(1)
    for k in range(pps):
        off = pl.multiple_of((p * pps + k) * PAGE_SIZE, PAGE_SIZE)
        for pair in range(2 * KV_W // LANES):
            cbuf[pair, pl.ds(off, PAGE_SIZE), :] = cmp_refs[k][0, :, pair * LANES:(pair + 1) * LANES]
        sbuf[pl.ds(off, PAGE_SIZE), :] = slc_refs[k][0].astype(BF16)

    @pl.when(p == n_pages // pps - 1)
    def _():
        past = n_pages * PAGE_SIZE
        t = past
        nh = past // D_CMP
        nbl = 2 * LANES
        qbd = qbd_ref[0]
        low = lax.broadcasted_iota(jnp.int32, (nh, LANES), 1) < HEAD_DIM
        row8 = lax.broadcasted_iota(jnp.int32, (SUBLANES, 1), 0)

        halves = []
        for pair in range(2 * KV_W // LANES):
            xl = lambda l, pair=pair: cbuf[pair, pl.ds(l, nh, stride=D_CMP), :]
            kv = pair // (KV_W // LANES)
            a, b = _compress_pair(xl, wcat_ref.at[kv], w2_ref.at[kv], pe_ref, nh)
            halves.append(jnp.where(low, a, pltpu.roll(b, HEAD_DIM, 1)).astype(BF16))
        kc = jnp.concatenate(halves[:2], axis=1)
        vc = jnp.concatenate(halves[2:], axis=1)

        crow = lax.broadcasted_iota(jnp.int32, (nh, 1), 0)
        vis = crow * D_CMP + (L_CMP - 1) <= t
        (e,), l = _col_softmax_parts([_bdot(kc, qbd)], [vis])
        pc = jnp.where(vis, e * (1.0 / l), 0.0)
        o_c = _dot_tn(pc, vc)

        psg = pc
        for h in range(1, HPG):
            psg = psg + pltpu.roll(pc, LANES - h, 1)
        hi = psg.astype(BF16)
        lo = (psg - hi.astype(F32)).astype(BF16)
        imp_t = _bdot(ovlt_ref[...], hi) + _bdot(ovlt_ref[...], lo)
        imp = jnp.concatenate([jnp.transpose(imp_t[j * LANES:(j + 1) * LANES]) for j in range(nbl // LANES)], axis=1)
        sel = _select_blocks(imp, jnp.full((LANES, 1), t // L_SLC, jnp.int32))
        sel_h = _bdot(rep_ref[...], sel.astype(BF16))
        bias_t = jnp.concatenate([jnp.transpose(sel_h[:, j * LANES:(j + 1) * LANES]) for j in range(nbl // LANES)],
                                 axis=0)
        bias_t = jnp.where(bias_t > 0.5, 0.0, NEG_INF)

        bpc = SLC_CHUNK // L_SLC
        m = jnp.full((1, LANES), NEG_INF, F32)
        for c in range(past // SLC_CHUNK):
            s = _bdot(sbuf[c * SLC_CHUNK:(c + 1) * SLC_CHUNK, 0:KV_W], qbd)
            b = bias_t[c * bpc:(c + 1) * bpc]
            s = (s.reshape(bpc, L_SLC, LANES) + b[:, None, :]).reshape(SLC_CHUNK, LANES)
            sc_s[c * SLC_CHUNK:(c + 1) * SLC_CHUNK, :] = s
            m = jnp.maximum(m, s.max(axis=0, keepdims=True))
        knew = nslc_ref[0]
        s_new = _bdot(knew[:, :KV_W].astype(BF16), qbd)[0:1] + bias_t[past // L_SLC:past // L_SLC + 1]
        m = jnp.maximum(m, s_new)
        l = jnp.exp2(s_new - m)
        acc = _new_key_term(l, knew)
        for c in range(past // SLC_CHUNK):
            e = jnp.exp2(sc_s[c * SLC_CHUNK:(c + 1) * SLC_CHUNK, :] - m)
            l = l + e.sum(axis=0, keepdims=True)
            acc = acc + _dot_tn(e, sbuf[c * SLC_CHUNK:(c + 1) * SLC_CHUNK, KV_W:])
        inv = 1.0 / _rows_from_lanes(l)
        o_s = acc * jnp.concatenate([inv, inv], axis=1)

        sw = swin_ref[0]
        w_eff = sw.shape[0]
        kpos = past - w_eff + lax.broadcasted_iota(jnp.int32, (w_eff, 1), 0)
        ok = (kpos > t - WINDOW) & (kpos >= 0)
        wnew = nwin_ref[0]
        (e1, e2), l = _col_softmax_parts([_bdot(sw[:, :KV_W].astype(BF16), qbd),
                                          _bdot(wnew[:, :KV_W].astype(BF16), qbd)[0:1]], [ok, True])
        inv = 1.0 / _rows_from_lanes(l)
        o_w = (_dot_tn(e1, sw[:, KV_W:].astype(BF16)) + _new_key_term(e2, wnew)) * jnp.concatenate([inv, inv], axis=1)

        gt = gt_ref[0]
        o = gt[:, 0:1] * o_c + gt[:, 1:2] * o_s + gt[:, 2:3] * o_w
        o_ref[0] = o[0:N_HEADS]


def _sample_nsa(page_table, cache_cmp2, cache_slc2, qbd, gt, nslc, nwin, swin, cw):
    wcat, w2, pe8 = cw
    ns, n_pages = page_table.shape
    past = n_pages * PAGE_SIZE
    nbl = 2 * LANES
    ovlt = jnp.transpose(_overlap_matrix(past // D_CMP, nbl))
    hh = jnp.arange(LANES)
    rep = ((hh[None, :] == (hh[:, None] // HPG) * HPG) & (hh[:, None] < N_HEADS)).astype(BF16)
    pps = PAGES_PER_STEP
    assert n_pages % pps == 0
    pages = [pl.BlockSpec((1, PAGE_SIZE, 2 * KV_W), lambda n, p, pt, k=k: (pt[n, p * pps + k], 0, 0))
             for k in range(pps)]
    per_n = lambda a: pl.BlockSpec((1,) + a.shape[1:], lambda n, p, pt, nd=a.ndim: (n,) + (0,) * (nd - 1))
    const = lambda a: pl.BlockSpec(a.shape, lambda n, p, pt, nd=a.ndim: (0,) * nd)
    grid_spec = pltpu.PrefetchScalarGridSpec(
        num_scalar_prefetch=1, grid=(ns, n_pages // pps),
        in_specs=pages + pages + [per_n(qbd), per_n(gt), per_n(nslc), per_n(nwin), per_n(swin),
                                  const(wcat), const(w2), const(pe8), const(ovlt), const(rep)],
        out_specs=pl.BlockSpec((1, N_HEADS, KV_W), lambda n, p, pt: (n, 0, 0)),
        scratch_shapes=[pltpu.VMEM((2 * KV_W // LANES, past, LANES), F32), pltpu.VMEM((past, 2 * KV_W), BF16),
                        pltpu.VMEM((past, LANES), F32)])
    return pl.pallas_call(
        functools.partial(_sample_nsa_kernel, n_pages=n_pages),
        grid_spec=grid_spec,
        out_shape=jax.ShapeDtypeStruct((ns, N_HEADS, KV_W), F32),
        compiler_params=_cparams(("arbitrary", "arbitrary")), name="sample_nsa",
    )(page_table, *([cache_cmp2] * pps), *([cache_slc2] * pps), qbd, gt, nslc, nwin, swin, wcat, w2, pe8, ovlt, rep)


def kernel(x_prompt, x_sample, cache_cmp, cache_slc, state_win, state_conv, state_ffn, page_table, w_in, cmp_pe,
           cmp_k_w1, cmp_k_w2, cmp_v_w1, cmp_v_w2, conv_dw_w, conv_dw_b, conv_ln_g, conv_ln_b, conv_w_pw, nsa_w_o,
           w_out, ln1_g, ln1_b, ffn_w_up, ffn_dw_w, ffn_w_down, ln2_g, ln2_b):
    assert w_in.shape[0] == DEPTH == 1
    nb, s, _ = x_prompt.shape
    ns = x_sample.shape[0]
    n_pages = page_table.shape[1]
    past = n_pages * PAGE_SIZE
    n_pool = cache_cmp.shape[1]

    w_r = _reorder_w_in(w_in[0])
    cw = _compress_weights(cmp_pe[0], cmp_k_w1[0], cmp_k_w2[0], cmp_v_w1[0], cmp_v_w2[0])
    mw = _mix_weights(conv_dw_b[0], conv_ln_g[0], conv_ln_b[0], conv_w_pw[0], nsa_w_o[0], w_out[0], ln1_g[0], ln1_b[0])
    fw = _ffn_weights(ffn_w_up[0], ffn_dw_w[0], ffn_w_down[0], ln2_g[0], ln2_b[0])
    dww = conv_dw_w[0]

    xp = x_prompt.reshape(nb * s, D_MODEL)
    u, q, cmp_p, slc_p, win_p, gm, gb, kaug, vs, kw, vw = _proj(
        xp, w_r, _rope_tables(jnp.arange(s)), tm=256, seq_len=s, n_seq=nb, emit_att=True)
    kcv = _compress_prompt(cmp_p, cw, n_seq=nb, seq_len=s)
    ocg, sb = _select_prompt(q, gb, kcv, n_seq=nb, seq_len=s, qs=256)
    o = _attn_prompt(q, gb, ocg, sb, kaug, vs, kw, vw, n_seq=nb, seq_len=s, tk=2048)
    h = _mix_prompt(u, o, gm, xp, dww, mw, n_seq=nb, seq_len=s, ts=256)
    y_p, up_tail = _ffn_prompt(h, fw, n_seq=nb, seq_len=s, ts=256)

    kv_shape = lambda n, t: (1, n, t, 2, N_KV, HEAD_DIM)
    w_keep = min(WINDOW, s)
    y_prompt = y_p.reshape(nb, s, D_MODEL)
    cmp_prompt = cmp_p.reshape(kv_shape(nb, s))
    slc_prompt = slc_p.reshape(kv_shape(nb, s))
    win_prompt = win_p.reshape(nb, s, 2 * KV_W)[:, s - w_keep:].reshape(kv_shape(nb, w_keep))
    conv_prompt = u.reshape(nb, s, C_CONV)[:, s - (CONV_K - 1):][None]
    ffn_prompt = up_tail[:, SUBLANES - (FFN_K - 1):][None]

    xs = x_sample.reshape(ns, D_MODEL)
    u_s, q_s, cmp_s, slc_s, win_s, gm_s, gb_s = _proj(
        xs, w_r, _rope_tables(jnp.full((ns,), past)), tm=ns, seq_len=ns, n_seq=1, emit_att=False)
    qt = jnp.transpose(q_s.reshape(ns, N_KV, HPG, HEAD_DIM), (0, 1, 3, 2))
    qbd = qt[:, :, :, None, :] * jnp.eye(N_KV, dtype=BF16)[None, :, None, :, None]
    qbd = jnp.pad(qbd.reshape(ns, KV_W, N_HEADS), ((0, 0), (0, 0), (0, LANES - N_HEADS)))
    gt = jnp.transpose(gb_s.reshape(ns, 3, N_HEADS), (0, 2, 1))
    gt = jnp.pad(gt, ((0, 0), (0, LANES - N_HEADS), (0, LANES - 3)))
    pad_rows = lambda a: jnp.pad(a[:, None, :], ((0, 0), (0, SUBLANES - 1), (0, 0)))
    w_eff = state_win.shape[2]
    o_s = _sample_nsa(page_table, cache_cmp[0].reshape(n_pool, PAGE_SIZE, 2 * KV_W),
                      cache_slc[0].reshape(n_pool, PAGE_SIZE, 2 * KV_W), qbd, gt, pad_rows(slc_s), pad_rows(win_s),
                      state_win[0].reshape(ns, w_eff, 2 * KV_W), cw)
    o_s = o_s.reshape(ns, N_KV, HPG, N_KV, HEAD_DIM)
    o_s = jnp.stack([o_s[:, g, :, g] for g in range(N_KV)], axis=1).reshape(ns, N_HEADS * HEAD_DIM).astype(BF16)
    h_s = _mix_sample(u_s, jnp.transpose(state_conv[0], (1, 0, 2)), o_s, gm_s, xs, dww, mw)
    y_s, up_s = _ffn_sample(h_s, state_ffn[0][:, 0], state_ffn[0][:, 1], fw)

    y_sample = y_s.reshape(ns, 1, D_MODEL)
    cmp_sample = cmp_s.reshape(kv_shape(ns, 1))
    slc_sample = slc_s.reshape(kv_shape(ns, 1))
    win_all = jnp.concatenate([state_win[0], win_s.reshape(ns, 1, 2, N_KV, HEAD_DIM)], axis=1)
    win_sample = win_all[:, -w_eff:][None]
    conv_sample = jnp.concatenate([state_conv[0], u_s[:, None]], axis=1)[:, -(CONV_K - 1):][None]
    ffn_sample = jnp.concatenate([state_ffn[0], up_s[:, None]], axis=1)[:, -(FFN_K - 1):][None]
    return (y_prompt, y_sample, cmp_prompt, cmp_sample, slc_prompt, slc_sample, win_prompt, win_sample,
            conv_prompt, conv_sample, ffn_prompt, ffn_sample)
```

```python
import functools

import jax
import jax.numpy as jnp
from jax import lax
from jax.experimental import pallas as pl
from jax.experimental.pallas import tpu as pltpu

F32 = jnp.float32
BF16 = jnp.bfloat16

D_MODEL = 1024
N_HEADS = 16
HEAD_DIM = 64
N_KV = 4
HPG = N_HEADS // N_KV
KV_W = N_KV * HEAD_DIM
ROT_DIM = HEAD_DIM // 4
ROPE_THETA = 500000.0
L_CMP = 32
D_CMP = 16
L_SLC = 64
N_SEL = 16
N_LOCAL = 2
WINDOW = 512
CMP_HID = 4 * HEAD_DIM
Q_BLOCK = 128
C_CONV = D_MODEL
CONV_K = 31
D_FF = 2816
FFN_K = 3
DEPTH = 1
DN_ALPHA = (2 * DEPTH) ** 0.25
LN_EPS = 1e-5
NEG_INF = -1e30
PAGE_SIZE = 128
LOG2E = 1.4426950408889634

LANES = 128
SUBLANES = 8
VMEM_LIMIT = 56 * 1024 * 1024

O_A = 0
O_B = C_CONV
O_Q = 2 * C_CONV
O_KV = O_Q + N_HEADS * HEAD_DIM
O_GM = O_KV + 6 * KV_W
O_GB = O_GM + 2 * D_MODEL
N_GB = 3 * N_HEADS
N_PROJ = O_GB + LANES


def _cparams(sem):
    return pltpu.CompilerParams(dimension_semantics=sem, vmem_limit_bytes=VMEM_LIMIT)


def _resident(shape):
    nd = len(shape)
    return pl.BlockSpec(shape, lambda *_: (0,) * nd, pipeline_mode=pl.Buffered(1))


def _bdot(a, b):
    return jnp.dot(a, b, preferred_element_type=F32)


def _dot_nt(a, b):
    return lax.dot_general(a, b, (((1,), (1,)), ((), ())), preferred_element_type=F32)


def _rope_tables(pos):
    half = ROT_DIM // 2
    inv = ROPE_THETA ** (-jnp.arange(half, dtype=F32) * 2.0 / ROT_DIM)
    ang = pos.astype(F32)[:, None] * inv[None, :]
    cos, sin = jnp.cos(ang), jnp.sin(ang)
    r = pos.shape[0]
    one = jnp.ones((r, HEAD_DIM - ROT_DIM), F32)
    zero = jnp.zeros((r, HEAD_DIM - ROT_DIM), F32)
    zh = jnp.zeros((r, half), F32)
    c = jnp.concatenate([cos, cos, one], axis=1)
    sa = jnp.concatenate([-sin, zh, zero], axis=1)
    sb = jnp.concatenate([zh, sin, zero], axis=1)
    rep = LANES // HEAD_DIM
    return jnp.tile(c, (1, rep)), jnp.tile(sa, (1, rep)), jnp.tile(sb, (1, rep))


def _rope(x, c, sa, sb):
    return x * c + pltpu.roll(x, LANES - ROT_DIM // 2, 1) * sa + pltpu.roll(x, ROT_DIM // 2, 1) * sb


def _proj_kernel(x_ref, w_ref, c_ref, sa_ref, sb_ref,
                 u_ref, q_ref, cmp_ref, slc_ref, win_ref, gm_ref, gb_ref, *att_refs,
                 tm, tiles_per_seq):
    xb = x_ref[...].astype(BF16)
    c, sa, sb = c_ref[...], sa_ref[...], sb_ref[...]
    cw = 2 * LANES

    def z(col, width=cw):
        return _bdot(xb, w_ref[:, col:col + width])

    for j in range(C_CONV // cw):
        u_ref[:, j * cw:(j + 1) * cw] = z(O_A + j * cw) * jax.nn.sigmoid(z(O_B + j * cw))

    def roped(col):
        zz = z(col)
        return [_rope(zz[:, h * LANES:(h + 1) * LANES], c, sa, sb) for h in range(cw // LANES)]

    scale = HEAD_DIM ** -0.5 * LOG2E
    for j in range(N_HEADS * HEAD_DIM // cw):
        halves = roped(O_Q + j * cw)
        for h, v in enumerate(halves):
            q_ref[:, j * cw + h * LANES:j * cw + (h + 1) * LANES] = (v * scale).astype(BF16)

    if att_refs:
        kaug_ref, vs_ref, kw_ref, vw_ref = att_refs
        lane = lax.broadcasted_iota(jnp.int32, (tm, LANES), 1)
        row = lax.broadcasted_iota(jnp.int32, (tm, LANES), 0)
        pos0 = (pl.program_id(0) % tiles_per_seq) * tm
        onehot = jnp.where(lane == (pos0 + row) // L_SLC, 1.0, 0.0).astype(BF16)
        low = lane < HEAD_DIM

        def per_group(halves, fill=0.0):
            out = []
            for g in range(N_KV):
                v = halves[g // 2]
                if g % 2:
                    v = pltpu.roll(v, HEAD_DIM, 1)
                out.append(jnp.where(low, v, fill).astype(BF16))
            return out

        ones_col = jnp.where(lane == HEAD_DIM, 1.0, 0.0)

    for br, dst in enumerate((cmp_ref, slc_ref, win_ref)):
        kh = roped(O_KV + br * 2 * KV_W)
        zv = z(O_KV + br * 2 * KV_W + KV_W)
        vh = [zv[:, h * LANES:(h + 1) * LANES] for h in range(cw // LANES)]
        for h in range(cw // LANES):
            dst[:, h * LANES:(h + 1) * LANES] = kh[h]
            dst[:, KV_W + h * LANES:KV_W + (h + 1) * LANES] = vh[h]
        if att_refs and br == 1:
            for g, (kg, vg) in enumerate(zip(per_group(kh), per_group(vh, ones_col))):
                kaug_ref[0, g, :, 0:LANES] = onehot
                kaug_ref[0, g, :, LANES:2 * LANES] = kg
                vs_ref[0, g] = vg
        if att_refs and br == 2:
            for g, (kg, vg) in enumerate(zip(per_group(kh), per_group(vh))):
                kw_ref[0, g] = kg
                vw_ref[0, g] = vg

    for j in range(2 * D_MODEL // cw):
        gm_ref[:, j * cw:(j + 1) * cw] = jax.nn.sigmoid(z(O_GM + j * cw))
    gb_ref[...] = jax.nn.sigmoid(z(O_GB, LANES))[:, :N_GB]


def _proj(x2d, w_r, tabs, *, tm, seq_len, n_seq, emit_att):
    r = x2d.shape[0]
    tps = seq_len // tm
    grid = (r // tm,)
    row = lambda w: pl.BlockSpec((tm, w), lambda i: (i, 0))
    tab = pl.BlockSpec((tm, LANES), lambda i: (i % tps, 0))
    out_shape = [jax.ShapeDtypeStruct((r, C_CONV), F32), jax.ShapeDtypeStruct((r, N_HEADS * HEAD_DIM), BF16),
                 jax.ShapeDtypeStruct((r, 2 * KV_W), F32), jax.ShapeDtypeStruct((r, 2 * KV_W), F32),
                 jax.ShapeDtypeStruct((r, 2 * KV_W), F32), jax.ShapeDtypeStruct((r, 2 * D_MODEL), F32),
                 jax.ShapeDtypeStruct((r, N_GB), F32)]
    out_specs = [row(C_CONV), row(N_HEADS * HEAD_DIM), row(2 * KV_W), row(2 * KV_W), row(2 * KV_W),
                 row(2 * D_MODEL), row(N_GB)]
    if emit_att:
        for w in (2 * LANES, LANES, LANES, LANES):
            out_shape.append(jax.ShapeDtypeStruct((n_seq, N_KV, seq_len, w), BF16))
            out_specs.append(pl.BlockSpec((1, N_KV, tm, w), lambda i: (i // tps, 0, i % tps, 0)))
    return pl.pallas_call(
        functools.partial(_proj_kernel, tm=tm, tiles_per_seq=tps),
        grid=grid,
        in_specs=[row(D_MODEL), _resident(w_r.shape), tab, tab, tab],
        out_specs=out_specs, out_shape=out_shape,
        compiler_params=_cparams(("arbitrary",)), name="proj",
    )(x2d, w_r, *tabs)


def _reorder_w_in(w_in):
    pad = jnp.zeros((D_MODEL, LANES - N_GB), w_in.dtype)
    o3 = 2 * C_CONV + N_HEADS * HEAD_DIM + 6 * KV_W
    return jnp.concatenate([w_in[:, :o3], w_in[:, o3 + N_GB:], w_in[:, o3:o3 + N_GB], pad], axis=1).astype(BF16)


def _compress_pair(xl, wcat_ref, w2_ref, pe_ref, nh):
    lane = lax.broadcasted_iota(jnp.int32, (nh, LANES), 1)
    low = lane < HEAD_DIM
    even, odd = [], []
    for l in range(0, D_CMP, 2):
        a, b = xl(l), xl(l + 1)
        even.append(jnp.where(low, a, pltpu.roll(b, HEAD_DIM, 1)).astype(BF16))
        odd.append(jnp.where(low, pltpu.roll(a, HEAD_DIM, 1), b).astype(BF16))
    bias2 = _bdot(pe_ref[...], wcat_ref[...])
    bias = bias2[0:1, :CMP_HID] + bias2[1:2, CMP_HID:]
    outs = []
    for x_parts in (even, odd):
        ab = _bdot(jnp.concatenate(x_parts, axis=1), wcat_ref[...])
        h = ab[:, :CMP_HID] + pltpu.roll(ab[:, CMP_HID:], nh - 1, 0) + bias
        outs.append(_bdot(jax.nn.gelu(h).astype(BF16), w2_ref[...]))
    return outs


def _compress_kernel(x_ref, wcat_ref, w2_ref, pe_ref, o_ref, *, nh):
    xl = lambda l: x_ref[0, pl.ds(l, nh, stride=D_CMP), :]
    for j, v in enumerate(_compress_pair(xl, wcat_ref.at[0], w2_ref.at[0], pe_ref, nh)):
        o_ref[0, j] = v.astype(BF16)


def _compress_weights(cmp_pe, k_w1, k_w2, v_w1, v_w2):
    def cat(w1):
        return jnp.concatenate([w1[:D_CMP].reshape(D_CMP * HEAD_DIM, CMP_HID),
                                w1[D_CMP:].reshape(D_CMP * HEAD_DIM, CMP_HID)], axis=1)
    wcat = jnp.stack([cat(k_w1), cat(v_w1)]).astype(BF16)
    pad = jnp.zeros((CMP_HID, LANES - HEAD_DIM), F32)
    w2 = jnp.stack([jnp.concatenate([k_w2, pad], 1), jnp.concatenate([v_w2, pad], 1)]).astype(BF16)
    pe2 = cmp_pe.reshape(2, D_CMP * HEAD_DIM)
    pe8 = jnp.concatenate([pe2, jnp.zeros((SUBLANES - 2, D_CMP * HEAD_DIM), F32)], 0).astype(BF16)
    return wcat, w2, pe8


def _compress_prompt(cmp2d, cw, *, n_seq, seq_len):
    wcat, w2, pe8 = cw
    nh = seq_len // D_CMP
    x3 = cmp2d.reshape(n_seq, seq_len, 2 * KV_W)
    npair = 2 * KV_W // LANES
    return pl.pallas_call(
        functools.partial(_compress_kernel, nh=nh),
        grid=(n_seq, npair),
        in_specs=[pl.BlockSpec((1, seq_len, LANES), lambda n, p: (n, 0, p)),
                  pl.BlockSpec((1,) + wcat.shape[1:], lambda n, p: (p // (npair // 2), 0, 0)),
                  pl.BlockSpec((1,) + w2.shape[1:], lambda n, p: (p // (npair // 2), 0, 0)),
                  pl.BlockSpec(pe8.shape, lambda n, p: (0, 0))],
        out_specs=pl.BlockSpec((1, 2, nh, LANES), lambda n, p: (n, p, 0, 0)),
        out_shape=jax.ShapeDtypeStruct((n_seq, 2 * N_KV, nh, LANES), BF16),
        compiler_params=_cparams(("arbitrary", "arbitrary")), name="compress_prompt",
    )(x3, wcat, w2, pe8)


BIG = 1e30


def _overlap_matrix(ncp, nb_lanes):
    start = jnp.arange(ncp) * D_CMP
    bstart = jnp.arange(nb_lanes) * L_SLC
    ov = (start[:, None] < bstart[None, :] + L_SLC) & (start[:, None] + L_CMP > bstart[None, :])
    return ov.astype(BF16)


def _split_heads(x):
    r = x.shape[0]
    low = lax.broadcasted_iota(jnp.int32, (r, LANES), 1) < HEAD_DIM
    out = []
    for h in range(HPG):
        v = x[:, (h // 2) * LANES:(h // 2 + 1) * LANES]
        if h % 2:
            v = pltpu.roll(v, HEAD_DIM, 1)
        out.append(jnp.where(low, v, 0.0))
    return out


def _merge_heads(parts):
    r = parts[0].shape[0]
    low = lax.broadcasted_iota(jnp.int32, (r, LANES), 1) < HEAD_DIM
    return [jnp.where(low, parts[2 * p], pltpu.roll(parts[2 * p + 1], HEAD_DIM, 1)) for p in range(HPG // 2)]


def _softmax(s):
    m = jnp.max(s, axis=-1, keepdims=True)
    e = jnp.exp2(s - m)
    return e * (1.0 / jnp.sum(e, axis=-1, keepdims=True))


def _select_blocks(imp, cur):
    r, nbl = imp.shape
    blk = lax.broadcasted_iota(jnp.int32, (r, nbl), 1)
    blkf = blk.astype(F32)
    forced = (blk == 0) | ((blk <= cur) & (blk > cur - N_LOCAL))
    v = jnp.where(forced, BIG, imp)
    v = jnp.where(blk <= cur, v, -BIG)
    sel = jnp.zeros((r, nbl), F32)
    for _ in range(N_SEL):
        m = jnp.max(v, axis=-1, keepdims=True)
        idx = jnp.min(jnp.where(v == m, blkf, float(nbl)), axis=-1, keepdims=True)
        hit = blkf == idx
        sel = jnp.where(hit & (v > -BIG), 1.0, sel)
        v = jnp.where(hit, -BIG, v)
    return sel


def _importance(psum, ovl):
    hi = psum.astype(BF16)
    lo = (psum - hi.astype(F32)).astype(BF16)
    return _bdot(hi, ovl) + _bdot(lo, ovl)


def _head_gates(gb, g, br, h):
    col = lax.broadcasted_iota(jnp.int32, gb.shape, 1)
    return jnp.sum(jnp.where(col == br * N_HEADS + g * HPG + h, gb, 0.0), axis=-1, keepdims=True)


def _select_kernel(q_ref, gb_ref, ovl_ref, kcv_ref, ocg_ref, sb_ref, *, qs, ncp):
    i = pl.program_id(1)
    tq = i * qs + lax.broadcasted_iota(jnp.int32, (qs, 1), 0)
    tq4 = jnp.concatenate([tq] * HPG, axis=0)
    c_end = lax.broadcasted_iota(jnp.int32, (1, ncp), 1) * D_CMP + (L_CMP - 1)
    vis = c_end <= tq4
    gb = gb_ref[...]
    gw = HPG * HEAD_DIM
    imps = []
    for g in range(N_KV):
        qf = q_ref[:, g * gw:(g + 1) * gw].astype(F32)
        q4 = jnp.concatenate([v.astype(BF16) for v in _split_heads(qf)], axis=0)
        s = _dot_nt(q4, kcv_ref[0, g])
        p = jnp.where(vis, _softmax(jnp.where(vis, s, NEG_INF)), 0.0)
        o_c = _bdot(p.astype(BF16), kcv_ref[0, N_KV + g])
        psum = p[0:qs]
        for h in range(1, HPG):
            psum = psum + p[h * qs:(h + 1) * qs]
        imps.append(_importance(psum, ovl_ref[...]))
        outs = [_head_gates(gb, g, 0, h) * o_c[h * qs:(h + 1) * qs] for h in range(HPG)]
        for pr, v in enumerate(_merge_heads(outs)):
            ocg_ref[:, g * gw + pr * LANES:g * gw + (pr + 1) * LANES] = v
    sel = _select_blocks(jnp.concatenate(imps, axis=0), jnp.concatenate([tq // L_SLC] * N_KV, axis=0))
    sb = jnp.where(sel > 0.0, 0.0, NEG_INF).astype(BF16)
    for g in range(N_KV):
        sb_ref[0, g] = sb[g * qs:(g + 1) * qs]


def _select_prompt(q2d, gb2d, kcv, *, n_seq, seq_len, qs):
    ncp = seq_len // D_CMP
    nq = seq_len // qs
    ovl = _overlap_matrix(ncp, LANES)
    row = lambda w: pl.BlockSpec((qs, w), lambda n, i: (n * nq + i, 0))
    return pl.pallas_call(
        functools.partial(_select_kernel, qs=qs, ncp=ncp),
        grid=(n_seq, nq),
        in_specs=[row(N_HEADS * HEAD_DIM), row(N_GB), pl.BlockSpec(ovl.shape, lambda n, i: (0, 0)),
                  pl.BlockSpec((1,) + kcv.shape[1:], lambda n, i: (n, 0, 0, 0))],
        out_specs=[row(N_HEADS * HEAD_DIM), pl.BlockSpec((1, N_KV, qs, LANES), lambda n, i: (n, 0, i, 0))],
        out_shape=[jax.ShapeDtypeStruct((n_seq * seq_len, N_HEADS * HEAD_DIM), F32),
                   jax.ShapeDtypeStruct((n_seq, N_KV, seq_len, LANES), BF16)],
        compiler_params=_cparams(("arbitrary", "arbitrary")), name="select_prompt",
    )(q2d, gb2d, ovl, kcv)


def _attn_kernel(q_ref, gb_ref, ocg_ref, sb_ref, kaug_ref, vs_ref, kw_ref, vw_ref, o_ref, *, tk):
    g = pl.program_id(1)
    i = pl.program_id(2)
    rows = HPG * Q_BLOCK
    q4 = jnp.concatenate([v.astype(BF16) for v in _split_heads(q_ref[...].astype(F32))], axis=0)
    tq = i * Q_BLOCK + lax.broadcasted_iota(jnp.int32, (Q_BLOCK, 1), 0)
    tq4 = jnp.concatenate([tq] * HPG, axis=0)
    q_aug = jnp.concatenate([jnp.concatenate([sb_ref[0, 0]] * HPG, axis=0), q4], axis=1)

    def tile(j, carry, causal):
        m, acc = carry
        off = pl.multiple_of(j * tk, tk)
        s = _dot_nt(q_aug, kaug_ref[0, 0, pl.ds(off, tk), :])
        if causal:
            kpos = off + lax.broadcasted_iota(jnp.int32, (1, tk), 1)
            s = jnp.where(kpos <= tq4, s, NEG_INF)
        m_new = jnp.maximum(m, jnp.max(s, axis=-1, keepdims=True))
        e = jnp.exp2((s - m_new).astype(BF16))
        return m_new, jnp.exp2(m - m_new) * acc + _bdot(e, vs_ref[0, 0, pl.ds(off, tk), :])

    init = (jnp.full((rows, 1), NEG_INF, F32), jnp.zeros((rows, LANES), F32))
    j_last = (i * Q_BLOCK) // tk
    _, acc = tile(j_last, lax.fori_loop(0, j_last, lambda j, c: tile(j, c, False), init), True)
    o_s = acc * (1.0 / acc[:, HEAD_DIM:HEAD_DIM + 1])

    wlen = WINDOW + Q_BLOCK
    start = pl.multiple_of(jnp.maximum(i - WINDOW // Q_BLOCK, 0) * Q_BLOCK, Q_BLOCK)
    s = _dot_nt(q4, kw_ref[0, 0, pl.ds(start, wlen), :])
    kpos = start + lax.broadcasted_iota(jnp.int32, (1, wlen), 1)
    ok = (kpos <= tq4) & (kpos > tq4 - WINDOW)
    o_w = _bdot(_softmax(jnp.where(ok, s, NEG_INF)).astype(BF16), vw_ref[0, 0, pl.ds(start, wlen), :])

    gb = gb_ref[...]
    outs = []
    for h in range(HPG):
        sl = slice(h * Q_BLOCK, (h + 1) * Q_BLOCK)
        outs.append(_head_gates(gb, g, 1, h) * o_s[sl] + _head_gates(gb, g, 2, h) * o_w[sl])
    for pr, v in enumerate(_merge_heads(outs)):
        o_ref[:, pr * LANES:(pr + 1) * LANES] = (ocg_ref[:, pr * LANES:(pr + 1) * LANES] + v).astype(BF16)


def _attn_prompt(q2d, gb2d, ocg, sb, kaug, vs, kw, vw, *, n_seq, seq_len, tk):
    nq = seq_len // Q_BLOCK
    gw = HPG * HEAD_DIM
    per_ng = lambda a: pl.BlockSpec((1, 1) + a.shape[2:], lambda n, g, i: (n, g, 0, 0))
    qblk = pl.BlockSpec((Q_BLOCK, gw), lambda n, g, i: (n * nq + i, g))
    return pl.pallas_call(
        functools.partial(_attn_kernel, tk=tk),
        grid=(n_seq, N_KV, nq),
        in_specs=[qblk, pl.BlockSpec((Q_BLOCK, N_GB), lambda n, g, i: (n * nq + i, 0)), qblk,
                  pl.BlockSpec((1, 1, Q_BLOCK, LANES), lambda n, g, i: (n, g, i, 0)),
                  per_ng(kaug), per_ng(vs), per_ng(kw), per_ng(vw)],
        out_specs=qblk,
        out_shape=jax.ShapeDtypeStruct((n_seq * seq_len, N_HEADS * HEAD_DIM), BF16),
        compiler_params=_cparams(("arbitrary", "arbitrary", "arbitrary")), name="attn_prompt",
    )(q2d, gb2d, ocg, sb, kaug, vs, kw, vw)


def _layer_norm(x, g, b):
    mu = jnp.mean(x, axis=-1, keepdims=True)
    xc = x - mu
    var = jnp.mean(xc * xc, axis=-1, keepdims=True)
    return xc * lax.rsqrt(var + LN_EPS) * g + b


def _mix_tail(cv, o_ref, gm_ref, x_ref, w, h_ref):
    dwb, clg, clb, pw, wo, wout, l1g, l1b = w
    hc = _layer_norm(cv + dwb[...], clg[...], clb[...])
    act = hc * jax.nn.sigmoid(hc)
    yc = _bdot(act.astype(BF16), pw[...])
    yo = _bdot(o_ref[...], wo[...])
    merged = gm_ref[:, :D_MODEL] * yc + gm_ref[:, D_MODEL:] * yo
    hh = DN_ALPHA * x_ref[...] + _bdot(merged.astype(BF16), wout[...])
    h_ref[...] = _layer_norm(hh, l1g[...], l1b[...])


CONV_HALO = 32
CONV_ROWS = 32


def _mix_prompt_kernel(u_ref, halo_ref, o_ref, gm_ref, x_ref, dww_ref, *rest, ts):
    w, h_ref, win_s, cv_s = rest[:8], rest[8], rest[9], rest[10]
    i = pl.program_id(1)

    @pl.when(i == 0)
    def _():
        win_s[0, 0:CONV_HALO] = jnp.zeros((CONV_HALO, C_CONV), F32)

    @pl.when(i > 0)
    def _():
        win_s[0, 0:CONV_HALO] = halo_ref[0]

    win_s[0, CONV_HALO:CONV_HALO + ts] = u_ref[0]
    span = ts + CONV_HALO - SUBLANES
    for r in range(1, SUBLANES):
        win_s[r, 0:span] = win_s[0, r:r + span]
    first = CONV_HALO - (CONV_K - 1)

    for c in range(ts // CONV_ROWS):
        acc = jnp.zeros((CONV_ROWS, C_CONV), F32)
        for k in range(CONV_K):
            r, a = (first + k) % SUBLANES, (first + k) // SUBLANES
            base = c * CONV_ROWS + a * SUBLANES
            acc = acc + win_s[r, base:base + CONV_ROWS, :] * dww_ref[k:k + 1, :]
        cv_s[c * CONV_ROWS:(c + 1) * CONV_ROWS, :] = acc
    _mix_tail(cv_s[...], o_ref, gm_ref, x_ref, w, h_ref)


def _mix_sample_kernel(u_ref, st_ref, o_ref, gm_ref, x_ref, dww_ref, *rest):
    w, h_ref = rest[:8], rest[8]
    cv = u_ref[...] * dww_ref[CONV_K - 1:CONV_K, :]
    for k in range(CONV_K - 1):
        cv = cv + st_ref[k] * dww_ref[k:k + 1, :]
    _mix_tail(cv, o_ref, gm_ref, x_ref, w, h_ref)


def _mix_weights(conv_dw_b, conv_ln_g, conv_ln_b, conv_w_pw, nsa_w_o, w_out, ln1_g, ln1_b):
    r = lambda v: v.reshape(1, -1)
    return (r(conv_dw_b), r(conv_ln_g), r(conv_ln_b), conv_w_pw.astype(BF16), nsa_w_o.astype(BF16),
            w_out.astype(BF16), r(ln1_g), r(ln1_b))


def _mix_prompt(u2d, o2d, gm2d, x2d, dww, mw, *, n_seq, seq_len, ts):
    tps = seq_len // ts
    hpt = ts // CONV_HALO
    u3 = u2d.reshape(n_seq, seq_len, C_CONV)
    row = lambda w: pl.BlockSpec((ts, w), lambda n, i: (n * tps + i, 0))
    return pl.pallas_call(
        functools.partial(_mix_prompt_kernel, ts=ts),
        grid=(n_seq, tps),
        in_specs=[pl.BlockSpec((1, ts, C_CONV), lambda n, i: (n, i, 0)),
                  pl.BlockSpec((1, CONV_HALO, C_CONV), lambda n, i: (n, jnp.maximum(i * hpt - 1, 0), 0)),
                  row(N_HEADS * HEAD_DIM), row(2 * D_MODEL), row(D_MODEL), _resident(dww.shape)]
                 + [_resident(a.shape) for a in mw],
        out_specs=row(D_MODEL),
        out_shape=jax.ShapeDtypeStruct((n_seq * seq_len, D_MODEL), F32),
        scratch_shapes=[pltpu.VMEM((SUBLANES, CONV_HALO + ts, C_CONV), F32), pltpu.VMEM((ts, C_CONV), F32)],
        compiler_params=_cparams(("arbitrary", "arbitrary")), name="mix_prompt",
    )(u3, u3, o2d, gm2d, x2d, dww, *mw)


def _mix_sample(u2d, st_t, o2d, gm2d, x2d, dww, mw):
    args = (u2d, st_t, o2d, gm2d, x2d, dww) + tuple(mw)
    full = lambda a: pl.BlockSpec(a.shape, lambda i, nd=a.ndim: (0,) * nd)
    return pl.pallas_call(
        _mix_sample_kernel, grid=(1,),
        in_specs=[full(a) for a in args],
        out_specs=pl.BlockSpec(x2d.shape, lambda i: (0, 0)),
        out_shape=jax.ShapeDtypeStruct(x2d.shape, F32),
        compiler_params=_cparams(("arbitrary",)), name="mix_sample",
    )(*args)


FFN_CW = 256


def _ffn_down(conv, h, wdown_ref, l2g_ref, l2b_ref, y_ref):
    acc = jnp.zeros(h.shape, F32)
    for c in range(D_FF // FFN_CW):
        val, gate = conv(c * FFN_CW), conv(D_FF + c * FFN_CW)
        act = gate * jax.nn.sigmoid(gate) * val
        acc = acc + _bdot(act.astype(BF16), wdown_ref[c * FFN_CW:(c + 1) * FFN_CW, :])
    y_ref[...] = _layer_norm(DN_ALPHA * h + acc, l2g_ref[...], l2b_ref[...])


def _ffn_prompt_kernel(h_ref, wup_ref, dw_ref, wdown_ref, l2g_ref, l2b_ref, y_ref, tail_ref, up_s, *, ts):
    i = pl.program_id(1)

    @pl.when(i == 0)
    def _():
        up_s[0:SUBLANES] = jnp.zeros((SUBLANES, 2 * D_FF), F32)

    @pl.when(i > 0)
    def _():
        up_s[0:SUBLANES] = up_s[ts:ts + SUBLANES]

    h = h_ref[...]
    hb = h.astype(BF16)
    cw = 2 * FFN_CW
    for c in range(2 * D_FF // cw):
        up_s[SUBLANES:SUBLANES + ts, c * cw:(c + 1) * cw] = _bdot(hb, wup_ref[:, c * cw:(c + 1) * cw])
    tail_ref[0] = up_s[ts:ts + SUBLANES]

    def conv(col):
        sl = slice(col, col + FFN_CW)
        out = up_s[SUBLANES:SUBLANES + ts, sl] * dw_ref[FFN_K - 1:FFN_K, sl]
        for k in range(FFN_K - 1):
            off = SUBLANES - (FFN_K - 1) + k
            out = out + up_s[off:off + ts, sl] * dw_ref[k:k + 1, sl]
        return out

    _ffn_down(conv, h, wdown_ref, l2g_ref, l2b_ref, y_ref)


def _ffn_sample_kernel(h_ref, s0_ref, s1_ref, wup_ref, dw_ref, wdown_ref, l2g_ref, l2b_ref, y_ref, up_ref):
    h = h_ref[...]
    hb = h.astype(BF16)
    cw = 2 * FFN_CW
    for c in range(2 * D_FF // cw):
        up_ref[:, c * cw:(c + 1) * cw] = _bdot(hb, wup_ref[:, c * cw:(c + 1) * cw])

    def conv(col):
        sl = slice(col, col + FFN_CW)
        return s0_ref[:, sl] * dw_ref[0:1, sl] + s1_ref[:, sl] * dw_ref[1:2, sl] + up_ref[:, sl] * dw_ref[2:3, sl]

    _ffn_down(conv, h, wdown_ref, l2g_ref, l2b_ref, y_ref)


def _ffn_prompt(h2d, fw, *, n_seq, seq_len, ts):
    tps = seq_len // ts
    row = pl.BlockSpec((ts, D_MODEL), lambda n, i: (n * tps + i, 0))
    return pl.pallas_call(
        functools.partial(_ffn_prompt_kernel, ts=ts),
        grid=(n_seq, tps),
        in_specs=[row] + [_resident(a.shape) for a in fw],
        out_specs=[row, pl.BlockSpec((1, SUBLANES, 2 * D_FF), lambda n, i: (n, 0, 0))],
        out_shape=[jax.ShapeDtypeStruct((n_seq * seq_len, D_MODEL), F32),
                   jax.ShapeDtypeStruct((n_seq, SUBLANES, 2 * D_FF), F32)],
        scratch_shapes=[pltpu.VMEM((SUBLANES + ts, 2 * D_FF), F32)],
        compiler_params=_cparams(("arbitrary", "arbitrary")), name="ffn_prompt",
    )(h2d, *fw)


def _ffn_sample(h2d, s0, s1, fw):
    args = (h2d, s0, s1) + tuple(fw)
    full = lambda a: pl.BlockSpec(a.shape, lambda i, nd=a.ndim: (0,) * nd)
    return pl.pallas_call(
        _ffn_sample_kernel, grid=(1,),
        in_specs=[full(a) for a in args],
        out_specs=[full(h2d), full(s0)],
        out_shape=[jax.ShapeDtypeStruct(h2d.shape, F32), jax.ShapeDtypeStruct(s0.shape, F32)],
        compiler_params=_cparams(("arbitrary",)), name="ffn_sample",
    )(*args)


def _ffn_weights(ffn_w_up, ffn_dw_w, ffn_w_down, ln2_g, ln2_b):
    return (ffn_w_up.astype(BF16), ffn_dw_w, ffn_w_down.astype(BF16), ln2_g.reshape(1, -1), ln2_b.reshape(1, -1))


PAGES_PER_STEP = 8


def _attend_new(q16, q16f, kt, vt, mask, new_row, new_ok):
    s = [jnp.where(mk, _bdot(q16, k), NEG_INF) for k, mk in zip(kt, mask)]
    s_new = jnp.where(new_ok, jnp.sum(q16f * new_row[:, :KV_W], axis=-1, keepdims=True), NEG_INF)
    m = s_new
    for x in s:
        m = jnp.maximum(m, jnp.max(x, axis=-1, keepdims=True))
    e_new = jnp.exp2(s_new - m)
    l = e_new
    acc = e_new * new_row[:, KV_W:]
    for x, v in zip(s, vt):
        e = jnp.exp2(x - m)
        l = l + jnp.sum(e, axis=-1, keepdims=True)
        acc = acc + _dot_nt(e.astype(BF16), v)
    return acc * (1.0 / l)


def _sample_nsa_kernel(pt_ref, *refs, n_pages):
    pps = PAGES_PER_STEP
    cmp_refs, slc_refs = refs[:pps], refs[pps:2 * pps]
    (q_ref, gt_ref, nslc_ref, nwin_ref, swt_ref, wcat_ref, w2_ref, pe_ref, ovl_ref, oh_ref,
     o_ref, cbuf, ktb, vtb) = refs[2 * pps:]
    p = pl.program_id(1)
    for k in range(pps):
        off = pl.multiple_of((p * pps + k) * PAGE_SIZE, PAGE_SIZE)
        for pair in range(2 * KV_W // LANES):
            cbuf[pair, pl.ds(off, PAGE_SIZE), :] = jnp.transpose(cmp_refs[k][0, pair * LANES:(pair + 1) * LANES, :])
        ktb[p, :, k * PAGE_SIZE:(k + 1) * PAGE_SIZE] = slc_refs[k][0, 0:KV_W, :].astype(BF16)
        vtb[p, :, k * PAGE_SIZE:(k + 1) * PAGE_SIZE] = slc_refs[k][0, KV_W:, :].astype(BF16)

    @pl.when(p == n_pages // pps - 1)
    def _():
        past = n_pages * PAGE_SIZE
        t = past
        nh = past // D_CMP
        nb_past = past // L_SLC
        q16 = q_ref[0]
        q16f = q16.astype(F32)
        low = lax.broadcasted_iota(jnp.int32, (nh, LANES), 1) < HEAD_DIM

        halves = []
        for pair in range(2 * KV_W // LANES):
            xl = lambda l, pair=pair: cbuf[pair, pl.ds(l, nh, stride=D_CMP), :]
            kv = pair // (KV_W // LANES)
            a, b = _compress_pair(xl, wcat_ref.at[kv], w2_ref.at[kv], pe_ref, nh)
            halves.append(jnp.where(low, a, pltpu.roll(b, HEAD_DIM, 1)).astype(BF16))
        kc = jnp.concatenate(halves[:2], axis=1)
        vc = jnp.concatenate(halves[2:], axis=1)

        c_end = lax.broadcasted_iota(jnp.int32, (1, nh), 1) * D_CMP + (L_CMP - 1)
        vis = c_end <= t
        pc = jnp.where(vis, _softmax(jnp.where(vis, _dot_nt(q16, kc), NEG_INF)), 0.0)
        o_c = _bdot(pc.astype(BF16), vc)

        imp16 = _importance(pc, ovl_ref[...])
        impg = imp16
        for h in range(1, HPG):
            impg = impg + pltpu.roll(imp16, N_HEADS - h, 0)
        sel = _select_blocks(impg, jnp.full((N_HEADS, 1), t // L_SLC, jnp.int32))
        first = lax.broadcasted_iota(jnp.int32, (N_HEADS, 1), 0) % HPG == 0
        sel4 = jnp.where(first, sel, 0.0)
        sel16 = sel4
        for h in range(1, HPG):
            sel16 = sel16 + pltpu.roll(sel4, h, 0)

        nchunk = n_pages // pps
        cw = pps * PAGE_SIZE
        selb = sel16[:, :nb_past].astype(BF16)
        masks = [_bdot(selb, oh_ref[:, c * cw:(c + 1) * cw]) > 0.5 for c in range(nchunk)]
        o_s = _attend_new(q16, q16f, [ktb[c] for c in range(nchunk)], [vtb[c] for c in range(nchunk)], masks,
                          nslc_ref[0][0:1], sel16[:, nb_past:nb_past + 1] > 0.5)

        swt = swt_ref[0]
        w_eff = swt.shape[1]
        kpos = past - w_eff + lax.broadcasted_iota(jnp.int32, (1, w_eff), 1)
        ok = (kpos > t - WINDOW) & (kpos >= 0)
        o_w = _attend_new(q16, q16f, [swt[:KV_W].astype(BF16)], [swt[KV_W:].astype(BF16)], [ok], nwin_ref[0][0:1], True)

        gt = gt_ref[0]
        o_ref[0] = gt[:, 0:1] * o_c + gt[:, 1:2] * o_s + gt[:, 2:3] * o_w


def _sample_nsa(page_table, cmp_t, slc_t, q16, gt, nslc, nwin, swt, cw):
    wcat, w2, pe8 = cw
    ns, n_pages = page_table.shape
    past = n_pages * PAGE_SIZE
    nbl = 2 * LANES
    nb_past = past // L_SLC
    ovl = _overlap_matrix(past // D_CMP, nbl)
    onehot = (jnp.arange(past)[None, :] // L_SLC == jnp.arange(nb_past)[:, None]).astype(BF16)
    pps = PAGES_PER_STEP
    assert n_pages % pps == 0 and nb_past + 1 <= nbl
    pages = [pl.BlockSpec((1, 2 * KV_W, PAGE_SIZE), lambda n, p, pt, k=k: (pt[n, p * pps + k], 0, 0))
             for k in range(pps)]
    per_n = lambda a: pl.BlockSpec((1,) + a.shape[1:], lambda n, p, pt, nd=a.ndim: (n,) + (0,) * (nd - 1))
    const = lambda a: pl.BlockSpec(a.shape, lambda n, p, pt, nd=a.ndim: (0,) * nd)
    grid_spec = pltpu.PrefetchScalarGridSpec(
        num_scalar_prefetch=1, grid=(ns, n_pages // pps),
        in_specs=pages + pages + [per_n(q16), per_n(gt), per_n(nslc), per_n(nwin), per_n(swt),
                                  const(wcat), const(w2), const(pe8), const(ovl), const(onehot)],
        out_specs=pl.BlockSpec((1, N_HEADS, KV_W), lambda n, p, pt: (n, 0, 0)),
        scratch_shapes=[pltpu.VMEM((2 * KV_W // LANES, past, LANES), F32),
                        pltpu.VMEM((n_pages // pps, KV_W, pps * PAGE_SIZE), BF16),
                        pltpu.VMEM((n_pages // pps, KV_W, pps * PAGE_SIZE), BF16)])
    return pl.pallas_call(
        functools.partial(_sample_nsa_kernel, n_pages=n_pages),
        grid_spec=grid_spec,
        out_shape=jax.ShapeDtypeStruct((ns, N_HEADS, KV_W), F32),
        compiler_params=_cparams(("arbitrary", "arbitrary")), name="sample_nsa",
    )(page_table, *([cmp_t] * pps), *([slc_t] * pps), q16, gt, nslc, nwin, swt, wcat, w2, pe8, ovl, onehot)


def kernel(x_prompt, x_sample, cache_cmp, cache_slc, state_win, state_conv, state_ffn, page_table, w_in, cmp_pe,
           cmp_k_w1, cmp_k_w2, cmp_v_w1, cmp_v_w2, conv_dw_w, conv_dw_b, conv_ln_g, conv_ln_b, conv_w_pw, nsa_w_o,
           w_out, ln1_g, ln1_b, ffn_w_up, ffn_dw_w, ffn_w_down, ln2_g, ln2_b):
    assert w_in.shape[0] == DEPTH == 1
    nb, s, _ = x_prompt.shape
    ns = x_sample.shape[0]
    n_pages = page_table.shape[1]
    past = n_pages * PAGE_SIZE
    n_pool = cache_cmp.shape[1]

    w_r = _reorder_w_in(w_in[0])
    cw = _compress_weights(cmp_pe[0], cmp_k_w1[0], cmp_k_w2[0], cmp_v_w1[0], cmp_v_w2[0])
    mw = _mix_weights(conv_dw_b[0], conv_ln_g[0], conv_ln_b[0], conv_w_pw[0], nsa_w_o[0], w_out[0], ln1_g[0], ln1_b[0])
    fw = _ffn_weights(ffn_w_up[0], ffn_dw_w[0], ffn_w_down[0], ln2_g[0], ln2_b[0])
    dww = conv_dw_w[0]

    xp = x_prompt.reshape(nb * s, D_MODEL)
    u, q, cmp_p, slc_p, win_p, gm, gb, kaug, vs, kw, vw = _proj(
        xp, w_r, _rope_tables(jnp.arange(s)), tm=256, seq_len=s, n_seq=nb, emit_att=True)
    kcv = _compress_prompt(cmp_p, cw, n_seq=nb, seq_len=s)
    ocg, sb = _select_prompt(q, gb, kcv, n_seq=nb, seq_len=s, qs=256)
    o = _attn_prompt(q, gb, ocg, sb, kaug, vs, kw, vw, n_seq=nb, seq_len=s, tk=2048)
    h = _mix_prompt(u, o, gm, xp, dww, mw, n_seq=nb, seq_len=s, ts=256)
    y_p, up_tail = _ffn_prompt(h, fw, n_seq=nb, seq_len=s, ts=256)

    kv_shape = lambda n, t: (1, n, t, 2, N_KV, HEAD_DIM)
    w_keep = min(WINDOW, s)
    y_prompt = y_p.reshape(nb, s, D_MODEL)
    cmp_prompt = cmp_p.reshape(kv_shape(nb, s))
    slc_prompt = slc_p.reshape(kv_shape(nb, s))
    win_prompt = win_p.reshape(nb, s, 2 * KV_W)[:, s - w_keep:].reshape(kv_shape(nb, w_keep))
    conv_prompt = u.reshape(nb, s, C_CONV)[:, s - (CONV_K - 1):][None]
    ffn_prompt = up_tail[:, SUBLANES - (FFN_K - 1):][None]

    xs = x_sample.reshape(ns, D_MODEL)
    u_s, q_s, cmp_s, slc_s, win_s, gm_s, gb_s = _proj(
        xs, w_r, _rope_tables(jnp.full((ns,), past)), tm=ns, seq_len=ns, n_seq=1, emit_att=False)
    q4 = q_s.reshape(ns, N_KV, HPG, 1, HEAD_DIM) * jnp.eye(N_KV, dtype=BF16)[None, :, None, :, None]
    q16 = q4.reshape(ns, N_HEADS, KV_W)
    gt = jnp.transpose(gb_s.reshape(ns, 3, N_HEADS), (0, 2, 1))
    gt = jnp.pad(gt, ((0, 0), (0, 0), (0, LANES - 3)))
    pad_rows = lambda a: jnp.pad(a[:, None, :], ((0, 0), (0, SUBLANES - 1), (0, 0)))
    w_eff = state_win.shape[2]
    feat_major = lambda a: jnp.transpose(a, (0, 2, 3, 4, 1)).reshape(a.shape[0], 2 * KV_W, a.shape[1])
    o_s = _sample_nsa(page_table, feat_major(cache_cmp[0]), feat_major(cache_slc[0]), q16, gt, pad_rows(slc_s),
                      pad_rows(win_s), feat_major(state_win[0]), cw)
    o_s = o_s.reshape(ns, N_KV, HPG, N_KV, HEAD_DIM)
    o_s = jnp.stack([o_s[:, g, :, g] for g in range(N_KV)], axis=1).reshape(ns, N_HEADS * HEAD_DIM).astype(BF16)
    h_s = _mix_sample(u_s, jnp.transpose(state_conv[0], (1, 0, 2)), o_s, gm_s, xs, dww, mw)
    y_s, up_s = _ffn_sample(h_s, state_ffn[0][:, 0], state_ffn[0][:, 1], fw)

    y_sample = y_s.reshape(ns, 1, D_MODEL)
    cmp_sample = cmp_s.reshape(kv_shape(ns, 1))
    slc_sample = slc_s.reshape(kv_shape(ns, 1))
    win_all = jnp.concatenate([state_win[0], win_s.reshape(ns, 1, 2, N_KV, HEAD_DIM)], axis=1)
    win_sample = win_all[:, -w_eff:][None]
    conv_sample = jnp.concatenate([state_conv[0], u_s[:, None]], axis=1)[:, -(CONV_K - 1):][None]
    ffn_sample = jnp.concatenate([state_ffn[0], up_s[:, None]], axis=1)[:, -(FFN_K - 1):][None]
    return (y_prompt, y_sample, cmp_prompt, cmp_sample, slc_prompt, slc_sample, win_prompt, win_sample,
            conv_prompt, conv_sample, ffn_prompt, ffn_sample)
```

```python
import functools

import jax
import jax.numpy as jnp
from jax import lax
from jax.experimental import pallas as pl
from jax.experimental.pallas import tpu as pltpu

F32 = jnp.float32
BF16 = jnp.bfloat16

D_MODEL = 1024
N_HEADS = 16
HEAD_DIM = 64
N_KV = 4
HPG = N_HEADS // N_KV
KV_W = N_KV * HEAD_DIM
ROT_DIM = HEAD_DIM // 4
ROPE_THETA = 500000.0
L_CMP = 32
D_CMP = 16
L_SLC = 64
N_SEL = 16
N_LOCAL = 2
WINDOW = 512
CMP_HID = 4 * HEAD_DIM
Q_BLOCK = 128
C_CONV = D_MODEL
CONV_K = 31
D_FF = 2816
FFN_K = 3
DEPTH = 1
DN_ALPHA = (2 * DEPTH) ** 0.25
LN_EPS = 1e-5
NEG_INF = -1e30
PAGE_SIZE = 128
LOG2E = 1.4426950408889634

LANES = 128
SUBLANES = 8
VMEM_LIMIT = 56 * 1024 * 1024

O_A = 0
O_B = C_CONV
O_Q = 2 * C_CONV
O_KV = O_Q + N_HEADS * HEAD_DIM
O_GM = O_KV + 6 * KV_W
O_GB = O_GM + 2 * D_MODEL
N_GB = 3 * N_HEADS
N_PROJ = O_GB + LANES


def _cparams(sem):
    return pltpu.CompilerParams(dimension_semantics=sem, vmem_limit_bytes=VMEM_LIMIT)


def _resident(shape):
    nd = len(shape)
    return pl.BlockSpec(shape, lambda *_: (0,) * nd, pipeline_mode=pl.Buffered(1))


def _bdot(a, b):
    return jnp.dot(a, b, preferred_element_type=F32)


def _dot_nt(a, b):
    return lax.dot_general(a, b, (((1,), (1,)), ((), ())), preferred_element_type=F32)


def _rope_tables(pos):
    half = ROT_DIM // 2
    inv = ROPE_THETA ** (-jnp.arange(half, dtype=F32) * 2.0 / ROT_DIM)
    ang = pos.astype(F32)[:, None] * inv[None, :]
    cos, sin = jnp.cos(ang), jnp.sin(ang)
    r = pos.shape[0]
    one = jnp.ones((r, HEAD_DIM - ROT_DIM), F32)
    zero = jnp.zeros((r, HEAD_DIM - ROT_DIM), F32)
    zh = jnp.zeros((r, half), F32)
    c = jnp.concatenate([cos, cos, one], axis=1)
    sa = jnp.concatenate([-sin, zh, zero], axis=1)
    sb = jnp.concatenate([zh, sin, zero], axis=1)
    rep = LANES // HEAD_DIM
    return jnp.tile(c, (1, rep)), jnp.tile(sa, (1, rep)), jnp.tile(sb, (1, rep))


def _rope(x, c, sa, sb):
    return x * c + pltpu.roll(x, LANES - ROT_DIM // 2, 1) * sa + pltpu.roll(x, ROT_DIM // 2, 1) * sb


def _proj_kernel(x_ref, w_ref, c_ref, sa_ref, sb_ref,
                 u_ref, q_ref, cmp_ref, slc_ref, win_ref, gm_ref, gb_ref, *att_refs,
                 tm, tiles_per_seq):
    xb = x_ref[...].astype(BF16)
    c, sa, sb = c_ref[...], sa_ref[...], sb_ref[...]
    cw = 2 * LANES

    def z(col, width=cw):
        return _bdot(xb, w_ref[:, col:col + width])

    for j in range(C_CONV // cw):
        u_ref[:, j * cw:(j + 1) * cw] = z(O_A + j * cw) * jax.nn.sigmoid(z(O_B + j * cw))

    def roped(col):
        zz = z(col)
        return [_rope(zz[:, h * LANES:(h + 1) * LANES], c, sa, sb) for h in range(cw // LANES)]

    scale = HEAD_DIM ** -0.5 * LOG2E
    for j in range(N_HEADS * HEAD_DIM // cw):
        halves = roped(O_Q + j * cw)
        for h, v in enumerate(halves):
            q_ref[:, j * cw + h * LANES:j * cw + (h + 1) * LANES] = (v * scale).astype(BF16)

    if att_refs:
        kaug_ref, vs_ref, kw_ref, vw_ref = att_refs[:4]
        feat_refs = att_refs[4:]
        lane = lax.broadcasted_iota(jnp.int32, (tm, LANES), 1)
        row = lax.broadcasted_iota(jnp.int32, (tm, LANES), 0)
        pos0 = (pl.program_id(0) % tiles_per_seq) * tm
        onehot = jnp.where(lane == (pos0 + row) // L_SLC, 1.0, 0.0).astype(BF16)
        low = lane < HEAD_DIM

        def per_group(halves, fill=0.0):
            out = []
            for g in range(N_KV):
                v = halves[g // 2]
                if g % 2:
                    v = pltpu.roll(v, HEAD_DIM, 1)
                out.append(jnp.where(low, v, fill).astype(BF16))
            return out

        ones_col = jnp.where(lane == HEAD_DIM, 1.0, 0.0)

    for br, dst in enumerate((cmp_ref, slc_ref, win_ref)):
        kh = roped(O_KV + br * 2 * KV_W)
        zv = z(O_KV + br * 2 * KV_W + KV_W)
        vh = [zv[:, h * LANES:(h + 1) * LANES] for h in range(cw // LANES)]
        for h in range(cw // LANES):
            dst[:, h * LANES:(h + 1) * LANES] = kh[h]
            dst[:, KV_W + h * LANES:KV_W + (h + 1) * LANES] = vh[h]
        if att_refs and br < 2:
            for h in range(cw // LANES):
                feat_refs[br][0, h * LANES:(h + 1) * LANES, :] = jnp.transpose(kh[h])
                feat_refs[br][0, KV_W + h * LANES:KV_W + (h + 1) * LANES, :] = jnp.transpose(vh[h])
        if att_refs and br == 1:
            for g, (kg, vg) in enumerate(zip(per_group(kh), per_group(vh, ones_col))):
                kaug_ref[0, g, :, 0:LANES] = onehot
                kaug_ref[0, g, :, LANES:2 * LANES] = kg
                vs_ref[0, g] = vg
        if att_refs and br == 2:
            for g, (kg, vg) in enumerate(zip(per_group(kh), per_group(vh, ones_col))):
                kw_ref[0, g] = kg
                vw_ref[0, g] = vg

    for j in range(2 * D_MODEL // cw):
        gm_ref[:, j * cw:(j + 1) * cw] = jax.nn.sigmoid(z(O_GM + j * cw))
    gb_ref[...] = jax.nn.sigmoid(z(O_GB, LANES))[:, :N_GB]


def _proj(x2d, w_r, tabs, *, tm, seq_len, n_seq, emit_att):
    r = x2d.shape[0]
    tps = seq_len // tm
    grid = (r // tm,)
    row = lambda w: pl.BlockSpec((tm, w), lambda i: (i, 0))
    tab = pl.BlockSpec((tm, LANES), lambda i: (i % tps, 0))
    out_shape = [jax.ShapeDtypeStruct((r, C_CONV), F32), jax.ShapeDtypeStruct((r, N_HEADS * HEAD_DIM), BF16),
                 jax.ShapeDtypeStruct((r, 2 * KV_W), F32), jax.ShapeDtypeStruct((r, 2 * KV_W), F32),
                 jax.ShapeDtypeStruct((r, 2 * KV_W), F32), jax.ShapeDtypeStruct((r, 2 * D_MODEL), F32),
                 jax.ShapeDtypeStruct((r, N_GB), F32)]
    out_specs = [row(C_CONV), row(N_HEADS * HEAD_DIM), row(2 * KV_W), row(2 * KV_W), row(2 * KV_W),
                 row(2 * D_MODEL), row(N_GB)]
    if emit_att:
        for w in (2 * LANES, LANES, LANES, LANES):
            out_shape.append(jax.ShapeDtypeStruct((n_seq, N_KV, seq_len, w), BF16))
            out_specs.append(pl.BlockSpec((1, N_KV, tm, w), lambda i: (i // tps, 0, i % tps, 0)))
        for _ in range(2):
            out_shape.append(jax.ShapeDtypeStruct((n_seq, 2 * KV_W, seq_len), F32))
            out_specs.append(pl.BlockSpec((1, 2 * KV_W, tm), lambda i: (i // tps, 0, i % tps)))
    return pl.pallas_call(
        functools.partial(_proj_kernel, tm=tm, tiles_per_seq=tps),
        grid=grid,
        in_specs=[row(D_MODEL), _resident(w_r.shape), tab, tab, tab],
        out_specs=out_specs, out_shape=out_shape,
        compiler_params=_cparams(("arbitrary",)), name="proj",
    )(x2d, w_r, *tabs)


def _reorder_w_in(w_in):
    pad = jnp.zeros((D_MODEL, LANES - N_GB), w_in.dtype)
    o3 = 2 * C_CONV + N_HEADS * HEAD_DIM + 6 * KV_W
    return jnp.concatenate([w_in[:, :o3], w_in[:, o3 + N_GB:], w_in[:, o3:o3 + N_GB], pad], axis=1).astype(BF16)


def _compress_pair(xl, wcat_ref, w2_ref, pe_ref, nh):
    lane = lax.broadcasted_iota(jnp.int32, (nh, LANES), 1)
    low = lane < HEAD_DIM
    even, odd = [], []
    for l in range(0, D_CMP, 2):
        a, b = xl(l), xl(l + 1)
        even.append(jnp.where(low, a, pltpu.roll(b, HEAD_DIM, 1)).astype(BF16))
        odd.append(jnp.where(low, pltpu.roll(a, HEAD_DIM, 1), b).astype(BF16))
    bias2 = _bdot(pe_ref[...], wcat_ref[...])
    bias = bias2[0:1, :CMP_HID] + bias2[1:2, CMP_HID:]
    outs = []
    for x_parts in (even, odd):
        ab = _bdot(jnp.concatenate(x_parts, axis=1), wcat_ref[...])
        h = ab[:, :CMP_HID] + pltpu.roll(ab[:, CMP_HID:], nh - 1, 0) + bias
        outs.append(_bdot(jax.nn.gelu(h).astype(BF16), w2_ref[...]))
    return outs


def _compress_kernel(x_ref, wcat_ref, w2_ref, pe_ref, o_ref, *, nh):
    xl = lambda l: x_ref[0, pl.ds(l, nh, stride=D_CMP), :]
    for j, v in enumerate(_compress_pair(xl, wcat_ref.at[0], w2_ref.at[0], pe_ref, nh)):
        o_ref[0, j] = v.astype(BF16)


def _compress_weights(cmp_pe, k_w1, k_w2, v_w1, v_w2):
    def cat(w1):
        return jnp.concatenate([w1[:D_CMP].reshape(D_CMP * HEAD_DIM, CMP_HID),
                                w1[D_CMP:].reshape(D_CMP * HEAD_DIM, CMP_HID)], axis=1)
    wcat = jnp.stack([cat(k_w1), cat(v_w1)]).astype(BF16)
    pad = jnp.zeros((CMP_HID, LANES - HEAD_DIM), F32)
    w2 = jnp.stack([jnp.concatenate([k_w2, pad], 1), jnp.concatenate([v_w2, pad], 1)]).astype(BF16)
    pe2 = cmp_pe.reshape(2, D_CMP * HEAD_DIM)
    pe8 = jnp.concatenate([pe2, jnp.zeros((SUBLANES - 2, D_CMP * HEAD_DIM), F32)], 0).astype(BF16)
    return wcat, w2, pe8


def _compress_prompt(cmp2d, cw, *, n_seq, seq_len):
    wcat, w2, pe8 = cw
    nh = seq_len // D_CMP
    x3 = cmp2d.reshape(n_seq, seq_len, 2 * KV_W)
    npair = 2 * KV_W // LANES
    return pl.pallas_call(
        functools.partial(_compress_kernel, nh=nh),
        grid=(n_seq, npair),
        in_specs=[pl.BlockSpec((1, seq_len, LANES), lambda n, p: (n, 0, p)),
                  pl.BlockSpec((1,) + wcat.shape[1:], lambda n, p: (p // (npair // 2), 0, 0)),
                  pl.BlockSpec((1,) + w2.shape[1:], lambda n, p: (p // (npair // 2), 0, 0)),
                  pl.BlockSpec(pe8.shape, lambda n, p: (0, 0))],
        out_specs=pl.BlockSpec((1, 2, nh, LANES), lambda n, p: (n, p, 0, 0)),
        out_shape=jax.ShapeDtypeStruct((n_seq, 2 * N_KV, nh, LANES), BF16),
        compiler_params=_cparams(("arbitrary", "arbitrary")), name="compress_prompt",
    )(x3, wcat, w2, pe8)


BIG = 1e30


def _overlap_matrix(ncp, nb_lanes):
    start = jnp.arange(ncp) * D_CMP
    bstart = jnp.arange(nb_lanes) * L_SLC
    ov = (start[:, None] < bstart[None, :] + L_SLC) & (start[:, None] + L_CMP > bstart[None, :])
    return ov.astype(BF16)


def _split_heads(x):
    r = x.shape[0]
    low = lax.broadcasted_iota(jnp.int32, (r, LANES), 1) < HEAD_DIM
    out = []
    for h in range(HPG):
        v = x[:, (h // 2) * LANES:(h // 2 + 1) * LANES]
        if h % 2:
            v = pltpu.roll(v, HEAD_DIM, 1)
        out.append(jnp.where(low, v, 0.0))
    return out


def _merge_heads(parts):
    r = parts[0].shape[0]
    low = lax.broadcasted_iota(jnp.int32, (r, LANES), 1) < HEAD_DIM
    return [jnp.where(low, parts[2 * p], pltpu.roll(parts[2 * p + 1], HEAD_DIM, 1)) for p in range(HPG // 2)]


def _softmax(s):
    m = jnp.max(s, axis=-1, keepdims=True)
    e = jnp.exp2(s - m)
    return e * (1.0 / jnp.sum(e, axis=-1, keepdims=True))


def _select_blocks(imp, cur):
    r, nbl = imp.shape
    blk = lax.broadcasted_iota(jnp.int32, (r, nbl), 1)
    blkf = blk.astype(F32)
    forced = (blk == 0) | ((blk <= cur) & (blk > cur - N_LOCAL))
    v = jnp.where(forced | (blk > cur), -BIG, imp)
    sel = jnp.where(forced, 1.0, 0.0)
    for _ in range(N_SEL - 1 - N_LOCAL):
        m = jnp.max(v, axis=-1, keepdims=True)
        idx = jnp.min(jnp.where(v == m, blkf, float(nbl)), axis=-1, keepdims=True)
        hit = blkf == idx
        sel = jnp.where(hit & (v > -BIG), 1.0, sel)
        v = jnp.where(hit, -BIG, v)
    return sel


def _importance(psum, ovl):
    hi = psum.astype(BF16)
    lo = (psum - hi.astype(F32)).astype(BF16)
    return _bdot(hi, ovl) + _bdot(lo, ovl)


def _head_gates(gb, g, br, h):
    col = lax.broadcasted_iota(jnp.int32, gb.shape, 1)
    return jnp.sum(jnp.where(col == br * N_HEADS + g * HPG + h, gb, 0.0), axis=-1, keepdims=True)


def _select_kernel(q_ref, gb_ref, ovl_ref, kcv_ref, ocg_ref, sb_ref, *, qs, ncp):
    i = pl.program_id(1)
    tq = i * qs + lax.broadcasted_iota(jnp.int32, (qs, 1), 0)
    tq4 = jnp.concatenate([tq] * HPG, axis=0)
    gb = gb_ref[...]
    gw = HPG * HEAD_DIM

    def cmp_branch(w):
        def run():
            c_end = lax.broadcasted_iota(jnp.int32, (1, w), 1) * D_CMP + (L_CMP - 1)
            vis = c_end <= tq4
            imps = []
            for g in range(N_KV):
                qf = q_ref[:, g * gw:(g + 1) * gw].astype(F32)
                q4 = jnp.concatenate([v.astype(BF16) for v in _split_heads(qf)], axis=0)
                s = _dot_nt(q4, kcv_ref[0, g, 0:w, :])
                p = jnp.where(vis, _softmax(jnp.where(vis, s, NEG_INF)), 0.0)
                o_c = _bdot(p.astype(BF16), kcv_ref[0, N_KV + g, 0:w, :])
                psum = p[0:qs]
                for h in range(1, HPG):
                    psum = psum + p[h * qs:(h + 1) * qs]
                imps.append(_importance(psum, ovl_ref[0:w, :]))
                outs = [_head_gates(gb, g, 0, h) * o_c[h * qs:(h + 1) * qs] for h in range(HPG)]
                for pr, v in enumerate(_merge_heads(outs)):
                    ocg_ref[:, g * gw + pr * LANES:g * gw + (pr + 1) * LANES] = v
            return jnp.concatenate(imps, axis=0)
        return run

    cstep = LANES
    widths = [min((b + 1) * cstep, ncp) for b in range(-(-ncp // cstep))]
    bucket = jnp.minimum(((i + 1) * (qs // D_CMP) - 1) // cstep, len(widths) - 1)
    imp = lax.switch(bucket, [cmp_branch(w) for w in widths])
    sel = _select_blocks(imp, jnp.concatenate([tq // L_SLC] * N_KV, axis=0))
    sb = jnp.where(sel > 0.0, 0.0, NEG_INF).astype(BF16)
    for g in range(N_KV):
        sb_ref[0, g] = sb[g * qs:(g + 1) * qs]


def _select_prompt(q2d, gb2d, kcv, *, n_seq, seq_len, qs):
    ncp = seq_len // D_CMP
    nq = seq_len // qs
    ovl = _overlap_matrix(ncp, LANES)
    row = lambda w: pl.BlockSpec((qs, w), lambda n, i: (n * nq + i, 0))
    return pl.pallas_call(
        functools.partial(_select_kernel, qs=qs, ncp=ncp),
        grid=(n_seq, nq),
        in_specs=[row(N_HEADS * HEAD_DIM), row(N_GB), pl.BlockSpec(ovl.shape, lambda n, i: (0, 0)),
                  pl.BlockSpec((1,) + kcv.shape[1:], lambda n, i: (n, 0, 0, 0))],
        out_specs=[row(N_HEADS * HEAD_DIM), pl.BlockSpec((1, N_KV, qs, LANES), lambda n, i: (n, 0, i, 0))],
        out_shape=[jax.ShapeDtypeStruct((n_seq * seq_len, N_HEADS * HEAD_DIM), F32),
                   jax.ShapeDtypeStruct((n_seq, N_KV, seq_len, LANES), BF16)],
        compiler_params=_cparams(("arbitrary", "arbitrary")), name="select_prompt",
    )(q2d, gb2d, ovl, kcv)


def _attn_kernel(q_ref, gb_ref, ocg_ref, sb_ref, kaug_ref, vs_ref, kw_ref, vw_ref, o_ref, *, tk, sub):
    g = pl.program_id(1)
    i = pl.program_id(2)
    rows = HPG * Q_BLOCK
    q4 = jnp.concatenate([v.astype(BF16) for v in _split_heads(q_ref[...].astype(F32))], axis=0)
    tq = i * Q_BLOCK + lax.broadcasted_iota(jnp.int32, (Q_BLOCK, 1), 0)
    tq4 = jnp.concatenate([tq] * HPG, axis=0)
    q_aug = jnp.concatenate([jnp.concatenate([sb_ref[0, 0]] * HPG, axis=0), q4], axis=1)

    def tile(off, carry, width, causal):
        m, acc = carry
        s = _dot_nt(q_aug, kaug_ref[0, 0, pl.ds(off, width), :])
        if causal:
            kpos = off + lax.broadcasted_iota(jnp.int32, (1, width), 1)
            s = jnp.where(kpos <= tq4, s, NEG_INF)
        m_new = jnp.maximum(m, jnp.max(s, axis=-1, keepdims=True))
        e = jnp.exp2((s - m_new).astype(BF16))
        return m_new, jnp.exp2(m - m_new) * acc + _bdot(e, vs_ref[0, 0, pl.ds(off, width), :])

    init = (jnp.full((rows, 1), NEG_INF, F32), jnp.zeros((rows, LANES), F32))
    j_last = (i * Q_BLOCK) // tk
    carry = lax.fori_loop(0, j_last, lambda j, c: tile(pl.multiple_of(j * tk, tk), c, tk, False), init)
    base = pl.multiple_of(j_last * tk, tk)
    n_sub = (i * Q_BLOCK + Q_BLOCK - base + sub - 1) // sub
    tails = [functools.partial(tile, base, width=(w + 1) * sub, causal=True) for w in range(tk // sub)]
    _, acc = lax.switch(n_sub - 1, tails, carry)
    o_s = acc * (1.0 / acc[:, HEAD_DIM:HEAD_DIM + 1])

    wlen = WINDOW + Q_BLOCK
    start = pl.multiple_of(jnp.maximum(i - WINDOW // Q_BLOCK, 0) * Q_BLOCK, Q_BLOCK)
    s = _dot_nt(q4, kw_ref[0, 0, pl.ds(start, wlen), :])
    kpos = start + lax.broadcasted_iota(jnp.int32, (1, wlen), 1)
    s = jnp.where((kpos <= tq4) & (kpos > tq4 - WINDOW), s, NEG_INF)
    e = jnp.exp2((s - jnp.max(s, axis=-1, keepdims=True)).astype(BF16))
    acc = _bdot(e, vw_ref[0, 0, pl.ds(start, wlen), :])
    o_w = acc * (1.0 / acc[:, HEAD_DIM:HEAD_DIM + 1])

    gb = gb_ref[...]
    outs = []
    for h in range(HPG):
        sl = slice(h * Q_BLOCK, (h + 1) * Q_BLOCK)
        outs.append(_head_gates(gb, g, 1, h) * o_s[sl] + _head_gates(gb, g, 2, h) * o_w[sl])
    for pr, v in enumerate(_merge_heads(outs)):
        o_ref[:, pr * LANES:(pr + 1) * LANES] = (ocg_ref[:, pr * LANES:(pr + 1) * LANES] + v).astype(BF16)


def _attn_prompt(q2d, gb2d, ocg, sb, kaug, vs, kw, vw, *, n_seq, seq_len, tk, sub):
    nq = seq_len // Q_BLOCK
    gw = HPG * HEAD_DIM
    per_ng = lambda a: pl.BlockSpec((1, 1) + a.shape[2:], lambda n, g, i: (n, g, 0, 0))
    qblk = pl.BlockSpec((Q_BLOCK, gw), lambda n, g, i: (n * nq + i, g))
    return pl.pallas_call(
        functools.partial(_attn_kernel, tk=tk, sub=sub),
        grid=(n_seq, N_KV, nq),
        in_specs=[qblk, pl.BlockSpec((Q_BLOCK, N_GB), lambda n, g, i: (n * nq + i, 0)), qblk,
                  pl.BlockSpec((1, 1, Q_BLOCK, LANES), lambda n, g, i: (n, g, i, 0)),
                  per_ng(kaug), per_ng(vs), per_ng(kw), per_ng(vw)],
        out_specs=qblk,
        out_shape=jax.ShapeDtypeStruct((n_seq * seq_len, N_HEADS * HEAD_DIM), BF16),
        compiler_params=_cparams(("arbitrary", "arbitrary", "arbitrary")), name="attn_prompt",
    )(q2d, gb2d, ocg, sb, kaug, vs, kw, vw)


def _layer_norm(x, g, b):
    mu = jnp.mean(x, axis=-1, keepdims=True)
    xc = x - mu
    var = jnp.mean(xc * xc, axis=-1, keepdims=True)
    return xc * lax.rsqrt(var + LN_EPS) * g + b


def _mix_tail(cv, o_ref, gm_ref, x_ref, w, h_ref):
    dwb, clg, clb, pw, wo, wout, l1g, l1b = w
    hc = _layer_norm(cv + dwb[...], clg[...], clb[...])
    act = hc * jax.nn.sigmoid(hc)
    yc = _bdot(act.astype(BF16), pw[...])
    yo = _bdot(o_ref[...], wo[...])
    merged = gm_ref[:, :D_MODEL] * yc + gm_ref[:, D_MODEL:] * yo
    hh = DN_ALPHA * x_ref[...] + _bdot(merged.astype(BF16), wout[...])
    h_ref[...] = _layer_norm(hh, l1g[...], l1b[...])


CONV_HALO = 32
CONV_ROWS = 32


def _mix_prompt_kernel(u_ref, halo_ref, o_ref, gm_ref, x_ref, dww_ref, *rest, ts):
    w, h_ref, win_s, cv_s = rest[:8], rest[8], rest[9], rest[10]
    i = pl.program_id(1)

    @pl.when(i == 0)
    def _():
        win_s[0, 0:CONV_HALO] = jnp.zeros((CONV_HALO, C_CONV), F32)

    @pl.when(i > 0)
    def _():
        win_s[0, 0:CONV_HALO] = halo_ref[0]

    win_s[0, CONV_HALO:CONV_HALO + ts] = u_ref[0]
    span = ts + CONV_HALO - SUBLANES
    for r in range(1, SUBLANES):
        win_s[r, 0:span] = win_s[0, r:r + span]
    first = CONV_HALO - (CONV_K - 1)

    for c in range(ts // CONV_ROWS):
        acc = jnp.zeros((CONV_ROWS, C_CONV), F32)
        for k in range(CONV_K):
            r, a = (first + k) % SUBLANES, (first + k) // SUBLANES
            base = c * CONV_ROWS + a * SUBLANES
            acc = acc + win_s[r, base:base + CONV_ROWS, :] * dww_ref[k:k + 1, :]
        cv_s[c * CONV_ROWS:(c + 1) * CONV_ROWS, :] = acc
    _mix_tail(cv_s[...], o_ref, gm_ref, x_ref, w, h_ref)


def _mix_sample_kernel(u_ref, st_ref, o_ref, gm_ref, x_ref, dww_ref, *rest):
    w, h_ref = rest[:8], rest[8]
    cv = u_ref[...] * dww_ref[CONV_K - 1:CONV_K, :]
    for k in range(CONV_K - 1):
        cv = cv + st_ref[k] * dww_ref[k:k + 1, :]
    _mix_tail(cv, o_ref, gm_ref, x_ref, w, h_ref)


def _mix_weights(conv_dw_b, conv_ln_g, conv_ln_b, conv_w_pw, nsa_w_o, w_out, ln1_g, ln1_b):
    r = lambda v: v.reshape(1, -1)
    return (r(conv_dw_b), r(conv_ln_g), r(conv_ln_b), conv_w_pw.astype(BF16), nsa_w_o.astype(BF16),
            w_out.astype(BF16), r(ln1_g), r(ln1_b))


def _mix_prompt(u2d, o2d, gm2d, x2d, dww, mw, *, n_seq, seq_len, ts):
    tps = seq_len // ts
    hpt = ts // CONV_HALO
    u3 = u2d.reshape(n_seq, seq_len, C_CONV)
    row = lambda w: pl.BlockSpec((ts, w), lambda n, i: (n * tps + i, 0))
    return pl.pallas_call(
        functools.partial(_mix_prompt_kernel, ts=ts),
        grid=(n_seq, tps),
        in_specs=[pl.BlockSpec((1, ts, C_CONV), lambda n, i: (n, i, 0)),
                  pl.BlockSpec((1, CONV_HALO, C_CONV), lambda n, i: (n, jnp.maximum(i * hpt - 1, 0), 0)),
                  row(N_HEADS * HEAD_DIM), row(2 * D_MODEL), row(D_MODEL), _resident(dww.shape)]
                 + [_resident(a.shape) for a in mw],
        out_specs=row(D_MODEL),
        out_shape=jax.ShapeDtypeStruct((n_seq * seq_len, D_MODEL), F32),
        scratch_shapes=[pltpu.VMEM((SUBLANES, CONV_HALO + ts, C_CONV), F32), pltpu.VMEM((ts, C_CONV), F32)],
        compiler_params=_cparams(("arbitrary", "arbitrary")), name="mix_prompt",
    )(u3, u3, o2d, gm2d, x2d, dww, *mw)


def _mix_sample(u2d, st_t, o2d, gm2d, x2d, dww, mw):
    args = (u2d, st_t, o2d, gm2d, x2d, dww) + tuple(mw)
    full = lambda a: pl.BlockSpec(a.shape, lambda i, nd=a.ndim: (0,) * nd)
    return pl.pallas_call(
        _mix_sample_kernel, grid=(1,),
        in_specs=[full(a) for a in args],
        out_specs=pl.BlockSpec(x2d.shape, lambda i: (0, 0)),
        out_shape=jax.ShapeDtypeStruct(x2d.shape, F32),
        compiler_params=_cparams(("arbitrary",)), name="mix_sample",
    )(*args)


FFN_CW = 256


def _ffn_down(conv, h, wdown_ref, l2g_ref, l2b_ref, y_ref):
    acc = jnp.zeros(h.shape, F32)
    for c in range(D_FF // FFN_CW):
        val, gate = conv(c * FFN_CW), conv(D_FF + c * FFN_CW)
        act = gate * jax.nn.sigmoid(gate) * val
        acc = acc + _bdot(act.astype(BF16), wdown_ref[c * FFN_CW:(c + 1) * FFN_CW, :])
    y_ref[...] = _layer_norm(DN_ALPHA * h + acc, l2g_ref[...], l2b_ref[...])


def _ffn_prompt_kernel(h_ref, wup_ref, dw_ref, wdown_ref, l2g_ref, l2b_ref, y_ref, tail_ref, up_s, *, ts):
    i = pl.program_id(1)

    @pl.when(i == 0)
    def _():
        up_s[0:SUBLANES] = jnp.zeros((SUBLANES, 2 * D_FF), F32)

    @pl.when(i > 0)
    def _():
        up_s[0:SUBLANES] = up_s[ts:ts + SUBLANES]

    h = h_ref[...]
    hb = h.astype(BF16)
    cw = 2 * FFN_CW
    for c in range(2 * D_FF // cw):
        up_s[SUBLANES:SUBLANES + ts, c * cw:(c + 1) * cw] = _bdot(hb, wup_ref[:, c * cw:(c + 1) * cw])
    tail_ref[0] = up_s[ts:ts + SUBLANES]

    def conv(col):
        sl = slice(col, col + FFN_CW)
        out = up_s[SUBLANES:SUBLANES + ts, sl] * dw_ref[FFN_K - 1:FFN_K, sl]
        for k in range(FFN_K - 1):
            off = SUBLANES - (FFN_K - 1) + k
            out = out + up_s[off:off + ts, sl] * dw_ref[k:k + 1, sl]
        return out

    _ffn_down(conv, h, wdown_ref, l2g_ref, l2b_ref, y_ref)


def _ffn_sample_kernel(h_ref, s0_ref, s1_ref, wup_ref, dw_ref, wdown_ref, l2g_ref, l2b_ref, y_ref, up_ref):
    h = h_ref[...]
    hb = h.astype(BF16)
    cw = 2 * FFN_CW
    for c in range(2 * D_FF // cw):
        up_ref[:, c * cw:(c + 1) * cw] = _bdot(hb, wup_ref[:, c * cw:(c + 1) * cw])

    def conv(col):
        sl = slice(col, col + FFN_CW)
        return s0_ref[:, sl] * dw_ref[0:1, sl] + s1_ref[:, sl] * dw_ref[1:2, sl] + up_ref[:, sl] * dw_ref[2:3, sl]

    _ffn_down(conv, h, wdown_ref, l2g_ref, l2b_ref, y_ref)


def _ffn_prompt(h2d, fw, *, n_seq, seq_len, ts):
    tps = seq_len // ts
    row = pl.BlockSpec((ts, D_MODEL), lambda n, i: (n * tps + i, 0))
    return pl.pallas_call(
        functools.partial(_ffn_prompt_kernel, ts=ts),
        grid=(n_seq, tps),
        in_specs=[row] + [_resident(a.shape) for a in fw],
        out_specs=[row, pl.BlockSpec((1, SUBLANES, 2 * D_FF), lambda n, i: (n, 0, 0))],
        out_shape=[jax.ShapeDtypeStruct((n_seq * seq_len, D_MODEL), F32),
                   jax.ShapeDtypeStruct((n_seq, SUBLANES, 2 * D_FF), F32)],
        scratch_shapes=[pltpu.VMEM((SUBLANES + ts, 2 * D_FF), F32)],
        compiler_params=_cparams(("arbitrary", "arbitrary")), name="ffn_prompt",
    )(h2d, *fw)


def _ffn_sample(h2d, s0, s1, fw):
    args = (h2d, s0, s1) + tuple(fw)
    full = lambda a: pl.BlockSpec(a.shape, lambda i, nd=a.ndim: (0,) * nd)
    return pl.pallas_call(
        _ffn_sample_kernel, grid=(1,),
        in_specs=[full(a) for a in args],
        out_specs=[full(h2d), full(s0)],
        out_shape=[jax.ShapeDtypeStruct(h2d.shape, F32), jax.ShapeDtypeStruct(s0.shape, F32)],
        compiler_params=_cparams(("arbitrary",)), name="ffn_sample",
    )(*args)


def _ffn_weights(ffn_w_up, ffn_dw_w, ffn_w_down, ln2_g, ln2_b):
    return (ffn_w_up.astype(BF16), ffn_dw_w, ffn_w_down.astype(BF16), ln2_g.reshape(1, -1), ln2_b.reshape(1, -1))


PAGES_PER_STEP = 8


def _attend_new(q16, q16f, kt, vt, mask, new_row, new_ok):
    s = [jnp.where(mk, _bdot(q16, k), NEG_INF) for k, mk in zip(kt, mask)]
    s_new = jnp.where(new_ok, jnp.sum(q16f * new_row[:, :KV_W], axis=-1, keepdims=True), NEG_INF)
    m = s_new
    for x in s:
        m = jnp.maximum(m, jnp.max(x, axis=-1, keepdims=True))
    e_new = jnp.exp2(s_new - m)
    l = e_new
    acc = e_new * new_row[:, KV_W:]
    for x, v in zip(s, vt):
        e = jnp.exp2(x - m)
        l = l + jnp.sum(e, axis=-1, keepdims=True)
        acc = acc + _dot_nt(e.astype(BF16), v)
    return acc * (1.0 / l)


def _sample_nsa_kernel(pt_ref, *refs, n_pages):
    pps = PAGES_PER_STEP
    cmp_refs, slc_refs = refs[:pps], refs[pps:2 * pps]
    (q_ref, gt_ref, nslc_ref, nwin_ref, swt_ref, wcat_ref, w2_ref, pe_ref, ovl_ref, oh_ref,
     o_ref, cbuf, ktb, vtb) = refs[2 * pps:]
    p = pl.program_id(1)
    for k in range(pps):
        off = pl.multiple_of((p * pps + k) * PAGE_SIZE, PAGE_SIZE)
        for pair in range(2 * KV_W // LANES):
            cbuf[pair, pl.ds(off, PAGE_SIZE), :] = jnp.transpose(cmp_refs[k][0, pair * LANES:(pair + 1) * LANES, :])
        ktb[p, :, k * PAGE_SIZE:(k + 1) * PAGE_SIZE] = slc_refs[k][0, 0:KV_W, :].astype(BF16)
        vtb[p, :, k * PAGE_SIZE:(k + 1) * PAGE_SIZE] = slc_refs[k][0, KV_W:, :].astype(BF16)

    @pl.when(p == n_pages // pps - 1)
    def _():
        past = n_pages * PAGE_SIZE
        t = past
        nh = past // D_CMP
        nb_past = past // L_SLC
        q16 = q_ref[0]
        q16f = q16.astype(F32)
        low = lax.broadcasted_iota(jnp.int32, (nh, LANES), 1) < HEAD_DIM

        halves = []
        for pair in range(2 * KV_W // LANES):
            xl = lambda l, pair=pair: cbuf[pair, pl.ds(l, nh, stride=D_CMP), :]
            kv = pair // (KV_W // LANES)
            a, b = _compress_pair(xl, wcat_ref.at[kv], w2_ref.at[kv], pe_ref, nh)
            halves.append(jnp.where(low, a, pltpu.roll(b, HEAD_DIM, 1)).astype(BF16))
        kc = jnp.concatenate(halves[:2], axis=1)
        vc = jnp.concatenate(halves[2:], axis=1)

        c_end = lax.broadcasted_iota(jnp.int32, (1, nh), 1) * D_CMP + (L_CMP - 1)
        vis = c_end <= t
        pc = jnp.where(vis, _softmax(jnp.where(vis, _dot_nt(q16, kc), NEG_INF)), 0.0)
        o_c = _bdot(pc.astype(BF16), vc)

        imp16 = _importance(pc, ovl_ref[...])
        impg = imp16
        for h in range(1, HPG):
            impg = impg + pltpu.roll(imp16, N_HEADS - h, 0)
        sel = _select_blocks(impg, jnp.full((N_HEADS, 1), t // L_SLC, jnp.int32))
        first = lax.broadcasted_iota(jnp.int32, (N_HEADS, 1), 0) % HPG == 0
        sel4 = jnp.where(first, sel, 0.0)
        sel16 = sel4
        for h in range(1, HPG):
            sel16 = sel16 + pltpu.roll(sel4, h, 0)

        nchunk = n_pages // pps
        cw = pps * PAGE_SIZE
        selb = sel16[:, :nb_past].astype(BF16)
        masks = [_bdot(selb, oh_ref[:, c * cw:(c + 1) * cw]) > 0.5 for c in range(nchunk)]
        o_s = _attend_new(q16, q16f, [ktb[c] for c in range(nchunk)], [vtb[c] for c in range(nchunk)], masks,
                          nslc_ref[0][0:1], sel16[:, nb_past:nb_past + 1] > 0.5)

        swt = swt_ref[0]
        w_eff = swt.shape[1]
        kpos = past - w_eff + lax.broadcasted_iota(jnp.int32, (1, w_eff), 1)
        ok = (kpos > t - WINDOW) & (kpos >= 0)
        o_w = _attend_new(q16, q16f, [swt[:KV_W].astype(BF16)], [swt[KV_W:].astype(BF16)], [ok], nwin_ref[0][0:1], True)

        gt = gt_ref[0]
        o_ref[0] = gt[:, 0:1] * o_c + gt[:, 1:2] * o_s + gt[:, 2:3] * o_w


def _sample_nsa(page_table, cmp_t, slc_t, q16, gt, nslc, nwin, swt, cw):
    wcat, w2, pe8 = cw
    ns, n_pages = page_table.shape
    past = n_pages * PAGE_SIZE
    nbl = 2 * LANES
    nb_past = past // L_SLC
    ovl = _overlap_matrix(past // D_CMP, nbl)
    onehot = (jnp.arange(past)[None, :] // L_SLC == jnp.arange(nb_past)[:, None]).astype(BF16)
    pps = PAGES_PER_STEP
    assert n_pages % pps == 0 and nb_past + 1 <= nbl
    pages = [pl.BlockSpec((1, 2 * KV_W, PAGE_SIZE), lambda n, p, pt, k=k: (pt[n, p * pps + k], 0, 0))
             for k in range(pps)]
    per_n = lambda a: pl.BlockSpec((1,) + a.shape[1:], lambda n, p, pt, nd=a.ndim: (n,) + (0,) * (nd - 1))
    const = lambda a: pl.BlockSpec(a.shape, lambda n, p, pt, nd=a.ndim: (0,) * nd)
    grid_spec = pltpu.PrefetchScalarGridSpec(
        num_scalar_prefetch=1, grid=(ns, n_pages // pps),
        in_specs=pages + pages + [per_n(q16), per_n(gt), per_n(nslc), per_n(nwin), per_n(swt),
                                  const(wcat), const(w2), const(pe8), const(ovl), const(onehot)],
        out_specs=pl.BlockSpec((1, N_HEADS, KV_W), lambda n, p, pt: (n, 0, 0)),
        scratch_shapes=[pltpu.VMEM((2 * KV_W // LANES, past, LANES), F32),
                        pltpu.VMEM((n_pages // pps, KV_W, pps * PAGE_SIZE), BF16),
                        pltpu.VMEM((n_pages // pps, KV_W, pps * PAGE_SIZE), BF16)])
    return pl.pallas_call(
        functools.partial(_sample_nsa_kernel, n_pages=n_pages),
        grid_spec=grid_spec,
        out_shape=jax.ShapeDtypeStruct((ns, N_HEADS, KV_W), F32),
        compiler_params=_cparams(("arbitrary", "arbitrary")), name="sample_nsa",
    )(page_table, *([cmp_t] * pps), *([slc_t] * pps), q16, gt, nslc, nwin, swt, wcat, w2, pe8, ovl, onehot)


def kernel(x_prompt, x_sample, cache_cmp, cache_slc, state_win, state_conv, state_ffn, page_table, w_in, cmp_pe,
           cmp_k_w1, cmp_k_w2, cmp_v_w1, cmp_v_w2, conv_dw_w, conv_dw_b, conv_ln_g, conv_ln_b, conv_w_pw, nsa_w_o,
           w_out, ln1_g, ln1_b, ffn_w_up, ffn_dw_w, ffn_w_down, ln2_g, ln2_b):
    assert w_in.shape[0] == DEPTH == 1
    nb, s, _ = x_prompt.shape
    ns = x_sample.shape[0]
    n_pages = page_table.shape[1]
    past = n_pages * PAGE_SIZE
    n_pool = cache_cmp.shape[1]

    w_r = _reorder_w_in(w_in[0])
    cw = _compress_weights(cmp_pe[0], cmp_k_w1[0], cmp_k_w2[0], cmp_v_w1[0], cmp_v_w2[0])
    mw = _mix_weights(conv_dw_b[0], conv_ln_g[0], conv_ln_b[0], conv_w_pw[0], nsa_w_o[0], w_out[0], ln1_g[0], ln1_b[0])
    fw = _ffn_weights(ffn_w_up[0], ffn_dw_w[0], ffn_w_down[0], ln2_g[0], ln2_b[0])
    dww = conv_dw_w[0]

    xp = x_prompt.reshape(nb * s, D_MODEL)
    u, q, cmp_p, _, win_p, gm, gb, kaug, vs, kw, vw, cmp_t, slc_t = _proj(
        xp, w_r, _rope_tables(jnp.arange(s)), tm=256, seq_len=s, n_seq=nb, emit_att=True)
    kcv = _compress_prompt(cmp_p, cw, n_seq=nb, seq_len=s)
    ocg, sb = _select_prompt(q, gb, kcv, n_seq=nb, seq_len=s, qs=256)
    o = _attn_prompt(q, gb, ocg, sb, kaug, vs, kw, vw, n_seq=nb, seq_len=s, tk=2048, sub=512)
    h = _mix_prompt(u, o, gm, xp, dww, mw, n_seq=nb, seq_len=s, ts=256)
    y_p, up_tail = _ffn_prompt(h, fw, n_seq=nb, seq_len=s, ts=256)

    kv_shape = lambda n, t: (1, n, t, 2, N_KV, HEAD_DIM)
    w_keep = min(WINDOW, s)
    y_prompt = y_p.reshape(nb, s, D_MODEL)
    from_feat = lambda a: jnp.transpose(a.reshape(nb, 2, N_KV, HEAD_DIM, s), (0, 4, 1, 2, 3))[None]
    cmp_prompt = from_feat(cmp_t)
    slc_prompt = from_feat(slc_t)
    win_prompt = win_p.reshape(nb, s, 2 * KV_W)[:, s - w_keep:].reshape(kv_shape(nb, w_keep))
    conv_prompt = u.reshape(nb, s, C_CONV)[:, s - (CONV_K - 1):][None]
    ffn_prompt = up_tail[:, SUBLANES - (FFN_K - 1):][None]

    xs = x_sample.reshape(ns, D_MODEL)
    u_s, q_s, cmp_s, slc_s, win_s, gm_s, gb_s = _proj(
        xs, w_r, _rope_tables(jnp.full((ns,), past)), tm=ns, seq_len=ns, n_seq=1, emit_att=False)
    q4 = q_s.reshape(ns, N_KV, HPG, 1, HEAD_DIM) * jnp.eye(N_KV, dtype=BF16)[None, :, None, :, None]
    q16 = q4.reshape(ns, N_HEADS, KV_W)
    gt = jnp.transpose(gb_s.reshape(ns, 3, N_HEADS), (0, 2, 1))
    gt = jnp.pad(gt, ((0, 0), (0, 0), (0, LANES - 3)))
    pad_rows = lambda a: jnp.pad(a[:, None, :], ((0, 0), (0, SUBLANES - 1), (0, 0)))
    w_eff = state_win.shape[2]
    feat_major = lambda a: jnp.transpose(a, (0, 2, 3, 4, 1)).reshape(a.shape[0], 2 * KV_W, a.shape[1])
    o_s = _sample_nsa(page_table, feat_major(cache_cmp[0]), feat_major(cache_slc[0]), q16, gt, pad_rows(slc_s),
                      pad_rows(win_s), feat_major(state_win[0]), cw)
    o_s = o_s.reshape(ns, N_KV, HPG, N_KV, HEAD_DIM)
    o_s = jnp.stack([o_s[:, g, :, g] for g in range(N_KV)], axis=1).reshape(ns, N_HEADS * HEAD_DIM).astype(BF16)
    h_s = _mix_sample(u_s, jnp.transpose(state_conv[0], (1, 0, 2)), o_s, gm_s, xs, dww, mw)
    y_s, up_s = _ffn_sample(h_s, state_ffn[0][:, 0], state_ffn[0][:, 1], fw)

    y_sample = y_s.reshape(ns, 1, D_MODEL)
    cmp_sample = cmp_s.reshape(kv_shape(ns, 1))
    slc_sample = slc_s.reshape(kv_shape(ns, 1))
    win_all = jnp.concatenate([state_win[0], win_s.reshape(ns, 1, 2, N_KV, HEAD_DIM)], axis=1)
    win_sample = win_all[:, -w_eff:][None]
    conv_sample = jnp.concatenate([state_conv[0], u_s[:, None]], axis=1)[:, -(CONV_K - 1):][None]
    ffn_sample = jnp.concatenate([state_ffn[0], up_s[:, None]], axis=1)[:, -(FFN_K - 1):][None]
    return (y_prompt, y_sample, cmp_prompt, cmp_sample, slc_prompt, slc_sample, win_prompt, win_sample,
            conv_prompt, conv_sample, ffn_prompt, ffn_sample)
```

```python
import functools

import jax
import jax.numpy as jnp
from jax import lax
from jax.experimental import pallas as pl
from jax.experimental.pallas import tpu as pltpu

F32 = jnp.float32
BF16 = jnp.bfloat16

D_MODEL = 1024
N_HEADS = 16
HEAD_DIM = 64
N_KV = 4
HPG = N_HEADS // N_KV
KV_W = N_KV * HEAD_DIM
ROT_DIM = HEAD_DIM // 4
ROPE_THETA = 500000.0
L_CMP = 32
D_CMP = 16
L_SLC = 64
N_SEL = 16
N_LOCAL = 2
WINDOW = 512
CMP_HID = 4 * HEAD_DIM
Q_BLOCK = 128
C_CONV = D_MODEL
CONV_K = 31
D_FF = 2816
FFN_K = 3
DEPTH = 1
DN_ALPHA = (2 * DEPTH) ** 0.25
LN_EPS = 1e-5
NEG_INF = -1e30
PAGE_SIZE = 128
LOG2E = 1.4426950408889634

LANES = 128
SUBLANES = 8
VMEM_LIMIT = 56 * 1024 * 1024

O_A = 0
O_B = C_CONV
O_Q = 2 * C_CONV
O_KV = O_Q + N_HEADS * HEAD_DIM
O_GM = O_KV + 6 * KV_W
O_GB = O_GM + 2 * D_MODEL
N_GB = 3 * N_HEADS
N_PROJ = O_GB + LANES


def _cparams(sem):
    return pltpu.CompilerParams(dimension_semantics=sem, vmem_limit_bytes=VMEM_LIMIT)


def _resident(shape):
    nd = len(shape)
    return pl.BlockSpec(shape, lambda *_: (0,) * nd, pipeline_mode=pl.Buffered(1))


def _bdot(a, b):
    return jnp.dot(a, b, preferred_element_type=F32)


def _dot_nt(a, b):
    return lax.dot_general(a, b, (((1,), (1,)), ((), ())), preferred_element_type=F32)


def _rope_tables(pos):
    half = ROT_DIM // 2
    inv = ROPE_THETA ** (-jnp.arange(half, dtype=F32) * 2.0 / ROT_DIM)
    ang = pos.astype(F32)[:, None] * inv[None, :]
    cos, sin = jnp.cos(ang), jnp.sin(ang)
    r = pos.shape[0]
    one = jnp.ones((r, HEAD_DIM - ROT_DIM), F32)
    zero = jnp.zeros((r, HEAD_DIM - ROT_DIM), F32)
    zh = jnp.zeros((r, half), F32)
    c = jnp.concatenate([cos, cos, one], axis=1)
    sa = jnp.concatenate([-sin, zh, zero], axis=1)
    sb = jnp.concatenate([zh, sin, zero], axis=1)
    rep = LANES // HEAD_DIM
    return jnp.tile(c, (1, rep)), jnp.tile(sa, (1, rep)), jnp.tile(sb, (1, rep))


def _rope(x, c, sa, sb):
    return x * c + pltpu.roll(x, LANES - ROT_DIM // 2, 1) * sa + pltpu.roll(x, ROT_DIM // 2, 1) * sb


def _proj_kernel(x_ref, w_ref, c_ref, sa_ref, sb_ref,
                 u_ref, q_ref, cmp_ref, slc_ref, win_ref, gm_ref, gb_ref, *att_refs,
                 tm, tiles_per_seq):
    xb = x_ref[...].astype(BF16)
    c, sa, sb = c_ref[...], sa_ref[...], sb_ref[...]
    cw = 2 * LANES

    def z(col, width=cw):
        return _bdot(xb, w_ref[:, col:col + width])

    for j in range(C_CONV // cw):
        u_ref[:, j * cw:(j + 1) * cw] = z(O_A + j * cw) * jax.nn.sigmoid(z(O_B + j * cw))

    def roped(col):
        zz = z(col)
        return [_rope(zz[:, h * LANES:(h + 1) * LANES], c, sa, sb) for h in range(cw // LANES)]

    scale = HEAD_DIM ** -0.5 * LOG2E
    for j in range(N_HEADS * HEAD_DIM // cw):
        halves = roped(O_Q + j * cw)
        for h, v in enumerate(halves):
            q_ref[:, j * cw + h * LANES:j * cw + (h + 1) * LANES] = (v * scale).astype(BF16)

    if att_refs:
        kaug_ref, vs_ref, kw_ref, vw_ref = att_refs[:4]
        feat_refs = att_refs[4:]
        lane = lax.broadcasted_iota(jnp.int32, (tm, LANES), 1)
        row = lax.broadcasted_iota(jnp.int32, (tm, LANES), 0)
        pos0 = (pl.program_id(0) % tiles_per_seq) * tm
        onehot = jnp.where(lane == (pos0 + row) // L_SLC, 1.0, 0.0).astype(BF16)
        low = lane < HEAD_DIM

        def per_group(halves, fill=0.0):
            out = []
            for g in range(N_KV):
                v = halves[g // 2]
                if g % 2:
                    v = pltpu.roll(v, HEAD_DIM, 1)
                out.append(jnp.where(low, v, fill).astype(BF16))
            return out

        ones_col = jnp.where(lane == HEAD_DIM, 1.0, 0.0)

    for br, dst in enumerate((cmp_ref, slc_ref, win_ref)):
        kh = roped(O_KV + br * 2 * KV_W)
        zv = z(O_KV + br * 2 * KV_W + KV_W)
        vh = [zv[:, h * LANES:(h + 1) * LANES] for h in range(cw // LANES)]
        for h in range(cw // LANES):
            dst[:, h * LANES:(h + 1) * LANES] = kh[h]
            dst[:, KV_W + h * LANES:KV_W + (h + 1) * LANES] = vh[h]
        if att_refs and br < 2:
            for h in range(cw // LANES):
                feat_refs[br][0, h * LANES:(h + 1) * LANES, :] = jnp.transpose(kh[h])
                feat_refs[br][0, KV_W + h * LANES:KV_W + (h + 1) * LANES, :] = jnp.transpose(vh[h])
        if att_refs and br == 1:
            for g, (kg, vg) in enumerate(zip(per_group(kh), per_group(vh, ones_col))):
                kaug_ref[0, g, :, 0:LANES] = onehot
                kaug_ref[0, g, :, LANES:2 * LANES] = kg
                vs_ref[0, g] = vg
        if att_refs and br == 2:
            for g, (kg, vg) in enumerate(zip(per_group(kh), per_group(vh, ones_col))):
                kw_ref[0, g] = kg
                vw_ref[0, g] = vg

    for j in range(2 * D_MODEL // cw):
        gm_ref[:, j * cw:(j + 1) * cw] = jax.nn.sigmoid(z(O_GM + j * cw))
    gb_ref[...] = jax.nn.sigmoid(z(O_GB, LANES))[:, :N_GB]


def _proj(x2d, w_r, tabs, *, tm, seq_len, n_seq, emit_att):
    r = x2d.shape[0]
    tps = seq_len // tm
    grid = (r // tm,)
    row = lambda w: pl.BlockSpec((tm, w), lambda i: (i, 0))
    tab = pl.BlockSpec((tm, LANES), lambda i: (i % tps, 0))
    out_shape = [jax.ShapeDtypeStruct((r, C_CONV), F32), jax.ShapeDtypeStruct((r, N_HEADS * HEAD_DIM), BF16),
                 jax.ShapeDtypeStruct((r, 2 * KV_W), F32), jax.ShapeDtypeStruct((r, 2 * KV_W), F32),
                 jax.ShapeDtypeStruct((r, 2 * KV_W), F32), jax.ShapeDtypeStruct((r, 2 * D_MODEL), F32),
                 jax.ShapeDtypeStruct((r, N_GB), F32)]
    out_specs = [row(C_CONV), row(N_HEADS * HEAD_DIM), row(2 * KV_W), row(2 * KV_W), row(2 * KV_W),
                 row(2 * D_MODEL), row(N_GB)]
    if emit_att:
        for w in (2 * LANES, LANES, LANES, LANES):
            out_shape.append(jax.ShapeDtypeStruct((n_seq, N_KV, seq_len, w), BF16))
            out_specs.append(pl.BlockSpec((1, N_KV, tm, w), lambda i: (i // tps, 0, i % tps, 0)))
        for _ in range(2):
            out_shape.append(jax.ShapeDtypeStruct((n_seq, 2 * KV_W, seq_len), F32))
            out_specs.append(pl.BlockSpec((1, 2 * KV_W, tm), lambda i: (i // tps, 0, i % tps)))
    return pl.pallas_call(
        functools.partial(_proj_kernel, tm=tm, tiles_per_seq=tps),
        grid=grid,
        in_specs=[row(D_MODEL), _resident(w_r.shape), tab, tab, tab],
        out_specs=out_specs, out_shape=out_shape,
        compiler_params=_cparams(("arbitrary",)), name="proj",
    )(x2d, w_r, *tabs)


def _reorder_w_in(w_in):
    pad = jnp.zeros((D_MODEL, LANES - N_GB), w_in.dtype)
    o3 = 2 * C_CONV + N_HEADS * HEAD_DIM + 6 * KV_W
    return jnp.concatenate([w_in[:, :o3], w_in[:, o3 + N_GB:], w_in[:, o3:o3 + N_GB], pad], axis=1).astype(BF16)


def _compress_pair(xl, wcat_ref, w2_ref, pe_ref, nh):
    lane = lax.broadcasted_iota(jnp.int32, (nh, LANES), 1)
    low = lane < HEAD_DIM
    even, odd = [], []
    for l in range(0, D_CMP, 2):
        a, b = xl(l), xl(l + 1)
        even.append(jnp.where(low, a, pltpu.roll(b, HEAD_DIM, 1)).astype(BF16))
        odd.append(jnp.where(low, pltpu.roll(a, HEAD_DIM, 1), b).astype(BF16))
    bias2 = _bdot(pe_ref[...], wcat_ref[...])
    bias = bias2[0:1, :CMP_HID] + bias2[1:2, CMP_HID:]
    outs = []
    for x_parts in (even, odd):
        ab = _bdot(jnp.concatenate(x_parts, axis=1), wcat_ref[...])
        h = ab[:, :CMP_HID] + pltpu.roll(ab[:, CMP_HID:], nh - 1, 0) + bias
        outs.append(_bdot(jax.nn.gelu(h).astype(BF16), w2_ref[...]))
    return outs


def _compress_kernel(x_ref, wcat_ref, w2_ref, pe_ref, o_ref, *, nh):
    xl = lambda l: x_ref[0, pl.ds(l, nh, stride=D_CMP), :]
    for j, v in enumerate(_compress_pair(xl, wcat_ref.at[0], w2_ref.at[0], pe_ref, nh)):
        o_ref[0, j] = v.astype(BF16)


def _compress_weights(cmp_pe, k_w1, k_w2, v_w1, v_w2):
    def cat(w1):
        return jnp.concatenate([w1[:D_CMP].reshape(D_CMP * HEAD_DIM, CMP_HID),
                                w1[D_CMP:].reshape(D_CMP * HEAD_DIM, CMP_HID)], axis=1)
    wcat = jnp.stack([cat(k_w1), cat(v_w1)]).astype(BF16)
    pad = jnp.zeros((CMP_HID, LANES - HEAD_DIM), F32)
    w2 = jnp.stack([jnp.concatenate([k_w2, pad], 1), jnp.concatenate([v_w2, pad], 1)]).astype(BF16)
    pe2 = cmp_pe.reshape(2, D_CMP * HEAD_DIM)
    pe8 = jnp.concatenate([pe2, jnp.zeros((SUBLANES - 2, D_CMP * HEAD_DIM), F32)], 0).astype(BF16)
    return wcat, w2, pe8


def _compress_prompt(cmp2d, cw, *, n_seq, seq_len):
    wcat, w2, pe8 = cw
    nh = seq_len // D_CMP
    x3 = cmp2d.reshape(n_seq, seq_len, 2 * KV_W)
    npair = 2 * KV_W // LANES
    return pl.pallas_call(
        functools.partial(_compress_kernel, nh=nh),
        grid=(n_seq, npair),
        in_specs=[pl.BlockSpec((1, seq_len, LANES), lambda n, p: (n, 0, p)),
                  pl.BlockSpec((1,) + wcat.shape[1:], lambda n, p: (p // (npair // 2), 0, 0)),
                  pl.BlockSpec((1,) + w2.shape[1:], lambda n, p: (p // (npair // 2), 0, 0)),
                  pl.BlockSpec(pe8.shape, lambda n, p: (0, 0))],
        out_specs=pl.BlockSpec((1, 2, nh, LANES), lambda n, p: (n, p, 0, 0)),
        out_shape=jax.ShapeDtypeStruct((n_seq, 2 * N_KV, nh, LANES), BF16),
        compiler_params=_cparams(("arbitrary", "arbitrary")), name="compress_prompt",
    )(x3, wcat, w2, pe8)


BIG = 1e30


def _overlap_matrix(ncp, nb_lanes):
    start = jnp.arange(ncp) * D_CMP
    bstart = jnp.arange(nb_lanes) * L_SLC
    ov = (start[:, None] < bstart[None, :] + L_SLC) & (start[:, None] + L_CMP > bstart[None, :])
    return ov.astype(BF16)


def _split_heads(x):
    r = x.shape[0]
    low = lax.broadcasted_iota(jnp.int32, (r, LANES), 1) < HEAD_DIM
    out = []
    for h in range(HPG):
        v = x[:, (h // 2) * LANES:(h // 2 + 1) * LANES]
        if h % 2:
            v = pltpu.roll(v, HEAD_DIM, 1)
        out.append(jnp.where(low, v, 0.0))
    return out


def _merge_heads(parts):
    r = parts[0].shape[0]
    low = lax.broadcasted_iota(jnp.int32, (r, LANES), 1) < HEAD_DIM
    return [jnp.where(low, parts[2 * p], pltpu.roll(parts[2 * p + 1], HEAD_DIM, 1)) for p in range(HPG // 2)]


def _softmax(s):
    m = jnp.max(s, axis=-1, keepdims=True)
    e = jnp.exp2(s - m)
    return e * (1.0 / jnp.sum(e, axis=-1, keepdims=True))


def _select_blocks(imp, cur, axis=1):
    nbl = imp.shape[axis]
    blk = lax.broadcasted_iota(jnp.int32, imp.shape, axis)
    blkf = blk.astype(F32)
    forced = (blk == 0) | ((blk <= cur) & (blk > cur - N_LOCAL))
    v = jnp.where(forced | (blk > cur), -BIG, imp)
    sel = jnp.where(forced, 1.0, 0.0)
    for _ in range(N_SEL - 1 - N_LOCAL):
        m = jnp.max(v, axis=axis, keepdims=True)
        idx = jnp.min(jnp.where(v == m, blkf, float(nbl)), axis=axis, keepdims=True)
        hit = blkf == idx
        sel = jnp.where(hit & (v > -BIG), 1.0, sel)
        v = jnp.where(hit, -BIG, v)
    return sel


def _importance(psum, ovl):
    hi = psum.astype(BF16)
    lo = (psum - hi.astype(F32)).astype(BF16)
    return _bdot(hi, ovl) + _bdot(lo, ovl)


def _head_gates(gb, g, br, h):
    col = lax.broadcasted_iota(jnp.int32, gb.shape, 1)
    return jnp.sum(jnp.where(col == br * N_HEADS + g * HPG + h, gb, 0.0), axis=-1, keepdims=True)


def _select_kernel(q_ref, gb_ref, ovl_ref, kcv_ref, ocg_ref, sb_ref, *, qs, ncp):
    i = pl.program_id(1)
    tq = i * qs + lax.broadcasted_iota(jnp.int32, (qs, 1), 0)
    tq4 = jnp.concatenate([tq] * HPG, axis=0)
    gb = gb_ref[...]
    gw = HPG * HEAD_DIM

    def cmp_branch(w):
        def run():
            c_end = lax.broadcasted_iota(jnp.int32, (1, w), 1) * D_CMP + (L_CMP - 1)
            vis = c_end <= tq4
            imps = []
            for g in range(N_KV):
                qf = q_ref[:, g * gw:(g + 1) * gw].astype(F32)
                q4 = jnp.concatenate([v.astype(BF16) for v in _split_heads(qf)], axis=0)
                s = _dot_nt(q4, kcv_ref[0, g, 0:w, :])
                p = jnp.where(vis, _softmax(jnp.where(vis, s, NEG_INF)), 0.0)
                o_c = _bdot(p.astype(BF16), kcv_ref[0, N_KV + g, 0:w, :])
                psum = p[0:qs]
                for h in range(1, HPG):
                    psum = psum + p[h * qs:(h + 1) * qs]
                imps.append(_importance(psum, ovl_ref[0:w, :]))
                outs = [_head_gates(gb, g, 0, h) * o_c[h * qs:(h + 1) * qs] for h in range(HPG)]
                for pr, v in enumerate(_merge_heads(outs)):
                    ocg_ref[:, g * gw + pr * LANES:g * gw + (pr + 1) * LANES] = v
            return jnp.concatenate(imps, axis=0)
        return run

    cstep = LANES
    widths = [min((b + 1) * cstep, ncp) for b in range(-(-ncp // cstep))]
    bucket = jnp.minimum(((i + 1) * (qs // D_CMP) - 1) // cstep, len(widths) - 1)
    imp = lax.switch(bucket, [cmp_branch(w) for w in widths])
    nt = N_KV * qs // LANES
    imp_t = jnp.concatenate([jnp.transpose(imp[c * LANES:(c + 1) * LANES]) for c in range(nt)], axis=1)
    cur_t = (i * qs + lax.broadcasted_iota(jnp.int32, (1, N_KV * qs), 1) % qs) // L_SLC
    sel_t = _select_blocks(imp_t, cur_t, axis=0)
    for c in range(nt):
        sel = jnp.transpose(sel_t[:, c * LANES:(c + 1) * LANES])
        r0 = (c * LANES) % qs
        sb_ref[0, c * LANES // qs, r0:r0 + LANES, :] = jnp.where(sel > 0.0, 0.0, NEG_INF).astype(BF16)


def _select_prompt(q2d, gb2d, kcv, *, n_seq, seq_len, qs):
    ncp = seq_len // D_CMP
    nq = seq_len // qs
    ovl = _overlap_matrix(ncp, LANES)
    row = lambda w: pl.BlockSpec((qs, w), lambda n, i: (n * nq + i, 0))
    return pl.pallas_call(
        functools.partial(_select_kernel, qs=qs, ncp=ncp),
        grid=(n_seq, nq),
        in_specs=[row(N_HEADS * HEAD_DIM), row(N_GB), pl.BlockSpec(ovl.shape, lambda n, i: (0, 0)),
                  pl.BlockSpec((1,) + kcv.shape[1:], lambda n, i: (n, 0, 0, 0))],
        out_specs=[row(N_HEADS * HEAD_DIM), pl.BlockSpec((1, N_KV, qs, LANES), lambda n, i: (n, 0, i, 0))],
        out_shape=[jax.ShapeDtypeStruct((n_seq * seq_len, N_HEADS * HEAD_DIM), F32),
                   jax.ShapeDtypeStruct((n_seq, N_KV, seq_len, LANES), BF16)],
        compiler_params=_cparams(("arbitrary", "arbitrary")), name="select_prompt",
    )(q2d, gb2d, ovl, kcv)


def _attn_kernel(q_ref, gb_ref, ocg_ref, sb_ref, kaug_ref, vs_ref, kw_ref, vw_ref, o_ref, *, tk, sub):
    g = pl.program_id(1)
    i = pl.program_id(2)
    rows = HPG * Q_BLOCK
    q4 = jnp.concatenate([v.astype(BF16) for v in _split_heads(q_ref[...].astype(F32))], axis=0)
    tq = i * Q_BLOCK + lax.broadcasted_iota(jnp.int32, (Q_BLOCK, 1), 0)
    tq4 = jnp.concatenate([tq] * HPG, axis=0)
    q_aug = jnp.concatenate([jnp.concatenate([sb_ref[0, 0]] * HPG, axis=0), q4], axis=1)

    def tile(off, carry, width, causal):
        m, acc = carry
        s = _dot_nt(q_aug, kaug_ref[0, 0, pl.ds(off, width), :])
        if causal:
            kpos = off + lax.broadcasted_iota(jnp.int32, (1, width), 1)
            s = jnp.where(kpos <= tq4, s, NEG_INF)
        m_new = jnp.maximum(m, jnp.max(s, axis=-1, keepdims=True))
        e = jnp.exp2((s - m_new).astype(BF16))
        return m_new, jnp.exp2(m - m_new) * acc + _bdot(e, vs_ref[0, 0, pl.ds(off, width), :])

    init = (jnp.full((rows, 1), NEG_INF, F32), jnp.zeros((rows, LANES), F32))
    j_last = (i * Q_BLOCK) // tk
    carry = lax.fori_loop(0, j_last, lambda j, c: tile(pl.multiple_of(j * tk, tk), c, tk, False), init)
    base = pl.multiple_of(j_last * tk, tk)
    n_sub = (i * Q_BLOCK + Q_BLOCK - base + sub - 1) // sub
    tails = [functools.partial(tile, base, width=(w + 1) * sub, causal=True) for w in range(tk // sub)]
    _, acc = lax.switch(n_sub - 1, tails, carry)
    o_s = acc * (1.0 / acc[:, HEAD_DIM:HEAD_DIM + 1])

    wlen = WINDOW + Q_BLOCK
    start = pl.multiple_of(jnp.maximum(i - WINDOW // Q_BLOCK, 0) * Q_BLOCK, Q_BLOCK)
    s = _dot_nt(q4, kw_ref[0, 0, pl.ds(start, wlen), :])
    kpos = start + lax.broadcasted_iota(jnp.int32, (1, wlen), 1)
    s = jnp.where((kpos <= tq4) & (kpos > tq4 - WINDOW), s, NEG_INF)
    e = jnp.exp2((s - jnp.max(s, axis=-1, keepdims=True)).astype(BF16))
    acc = _bdot(e, vw_ref[0, 0, pl.ds(start, wlen), :])
    o_w = acc * (1.0 / acc[:, HEAD_DIM:HEAD_DIM + 1])

    gb = gb_ref[...]
    outs = []
    for h in range(HPG):
        sl = slice(h * Q_BLOCK, (h + 1) * Q_BLOCK)
        outs.append(_head_gates(gb, g, 1, h) * o_s[sl] + _head_gates(gb, g, 2, h) * o_w[sl])
    for pr, v in enumerate(_merge_heads(outs)):
        o_ref[:, pr * LANES:(pr + 1) * LANES] = (ocg_ref[:, pr * LANES:(pr + 1) * LANES] + v).astype(BF16)


def _attn_prompt(q2d, gb2d, ocg, sb, kaug, vs, kw, vw, *, n_seq, seq_len, tk, sub):
    nq = seq_len // Q_BLOCK
    gw = HPG * HEAD_DIM
    per_ng = lambda a: pl.BlockSpec((1, 1) + a.shape[2:], lambda n, g, i: (n, g, 0, 0))
    qblk = pl.BlockSpec((Q_BLOCK, gw), lambda n, g, i: (n * nq + i, g))
    return pl.pallas_call(
        functools.partial(_attn_kernel, tk=tk, sub=sub),
        grid=(n_seq, N_KV, nq),
        in_specs=[qblk, pl.BlockSpec((Q_BLOCK, N_GB), lambda n, g, i: (n * nq + i, 0)), qblk,
                  pl.BlockSpec((1, 1, Q_BLOCK, LANES), lambda n, g, i: (n, g, i, 0)),
                  per_ng(kaug), per_ng(vs), per_ng(kw), per_ng(vw)],
        out_specs=qblk,
        out_shape=jax.ShapeDtypeStruct((n_seq * seq_len, N_HEADS * HEAD_DIM), BF16),
        compiler_params=_cparams(("arbitrary", "arbitrary", "arbitrary")), name="attn_prompt",
    )(q2d, gb2d, ocg, sb, kaug, vs, kw, vw)


def _layer_norm(x, g, b):
    mu = jnp.mean(x, axis=-1, keepdims=True)
    xc = x - mu
    var = jnp.mean(xc * xc, axis=-1, keepdims=True)
    return xc * lax.rsqrt(var + LN_EPS) * g + b


def _mix_tail(cv, o_ref, gm_ref, x_ref, w, h_ref):
    dwb, clg, clb, pw, wo, wout, l1g, l1b = w
    hc = _layer_norm(cv + dwb[...], clg[...], clb[...])
    act = hc * jax.nn.sigmoid(hc)
    yc = _bdot(act.astype(BF16), pw[...])
    yo = _bdot(o_ref[...], wo[...])
    merged = gm_ref[:, :D_MODEL] * yc + gm_ref[:, D_MODEL:] * yo
    hh = DN_ALPHA * x_ref[...] + _bdot(merged.astype(BF16), wout[...])
    h_ref[...] = _layer_norm(hh, l1g[...], l1b[...])


CONV_HALO = 32
CONV_ROWS = 32


def _mix_prompt_kernel(u_ref, halo_ref, o_ref, gm_ref, x_ref, dww_ref, *rest, ts):
    w, h_ref, win_s, cv_s = rest[:8], rest[8], rest[9], rest[10]
    i = pl.program_id(1)

    @pl.when(i == 0)
    def _():
        win_s[0, 0:CONV_HALO] = jnp.zeros((CONV_HALO, C_CONV), F32)

    @pl.when(i > 0)
    def _():
        win_s[0, 0:CONV_HALO] = halo_ref[0]

    win_s[0, CONV_HALO:CONV_HALO + ts] = u_ref[0]
    span = ts + CONV_HALO - SUBLANES
    for r in range(1, SUBLANES):
        win_s[r, 0:span] = win_s[0, r:r + span]
    first = CONV_HALO - (CONV_K - 1)

    for c in range(ts // CONV_ROWS):
        acc = jnp.zeros((CONV_ROWS, C_CONV), F32)
        for k in range(CONV_K):
            r, a = (first + k) % SUBLANES, (first + k) // SUBLANES
            base = c * CONV_ROWS + a * SUBLANES
            acc = acc + win_s[r, base:base + CONV_ROWS, :] * dww_ref[k:k + 1, :]
        cv_s[c * CONV_ROWS:(c + 1) * CONV_ROWS, :] = acc
    _mix_tail(cv_s[...], o_ref, gm_ref, x_ref, w, h_ref)


def _mix_sample_kernel(u_ref, st_ref, o_ref, gm_ref, x_ref, dww_ref, *rest):
    w, h_ref = rest[:8], rest[8]
    cv = u_ref[...] * dww_ref[CONV_K - 1:CONV_K, :]
    for k in range(CONV_K - 1):
        cv = cv + st_ref[k] * dww_ref[k:k + 1, :]
    _mix_tail(cv, o_ref, gm_ref, x_ref, w, h_ref)


def _mix_weights(conv_dw_b, conv_ln_g, conv_ln_b, conv_w_pw, nsa_w_o, w_out, ln1_g, ln1_b):
    r = lambda v: v.reshape(1, -1)
    return (r(conv_dw_b), r(conv_ln_g), r(conv_ln_b), conv_w_pw.astype(BF16), nsa_w_o.astype(BF16),
            w_out.astype(BF16), r(ln1_g), r(ln1_b))


def _mix_prompt(u2d, o2d, gm2d, x2d, dww, mw, *, n_seq, seq_len, ts):
    tps = seq_len // ts
    hpt = ts // CONV_HALO
    u3 = u2d.reshape(n_seq, seq_len, C_CONV)
    row = lambda w: pl.BlockSpec((ts, w), lambda n, i: (n * tps + i, 0))
    return pl.pallas_call(
        functools.partial(_mix_prompt_kernel, ts=ts),
        grid=(n_seq, tps),
        in_specs=[pl.BlockSpec((1, ts, C_CONV), lambda n, i: (n, i, 0)),
                  pl.BlockSpec((1, CONV_HALO, C_CONV), lambda n, i: (n, jnp.maximum(i * hpt - 1, 0), 0)),
                  row(N_HEADS * HEAD_DIM), row(2 * D_MODEL), row(D_MODEL), _resident(dww.shape)]
                 + [_resident(a.shape) for a in mw],
        out_specs=row(D_MODEL),
        out_shape=jax.ShapeDtypeStruct((n_seq * seq_len, D_MODEL), F32),
        scratch_shapes=[pltpu.VMEM((SUBLANES, CONV_HALO + ts, C_CONV), F32), pltpu.VMEM((ts, C_CONV), F32)],
        compiler_params=_cparams(("arbitrary", "arbitrary")), name="mix_prompt",
    )(u3, u3, o2d, gm2d, x2d, dww, *mw)


def _mix_sample(u2d, st_t, o2d, gm2d, x2d, dww, mw):
    args = (u2d, st_t, o2d, gm2d, x2d, dww) + tuple(mw)
    full = lambda a: pl.BlockSpec(a.shape, lambda i, nd=a.ndim: (0,) * nd)
    return pl.pallas_call(
        _mix_sample_kernel, grid=(1,),
        in_specs=[full(a) for a in args],
        out_specs=pl.BlockSpec(x2d.shape, lambda i: (0, 0)),
        out_shape=jax.ShapeDtypeStruct(x2d.shape, F32),
        compiler_params=_cparams(("arbitrary",)), name="mix_sample",
    )(*args)


FFN_CW = 256


def _ffn_down(conv, h, wdown_ref, l2g_ref, l2b_ref, y_ref):
    acc = jnp.zeros(h.shape, F32)
    for c in range(D_FF // FFN_CW):
        val, gate = conv(c * FFN_CW), conv(D_FF + c * FFN_CW)
        act = gate * jax.nn.sigmoid(gate) * val
        acc = acc + _bdot(act.astype(BF16), wdown_ref[c * FFN_CW:(c + 1) * FFN_CW, :])
    y_ref[...] = _layer_norm(DN_ALPHA * h + acc, l2g_ref[...], l2b_ref[...])


def _ffn_prompt_kernel(h_ref, wup_ref, dw_ref, wdown_ref, l2g_ref, l2b_ref, y_ref, tail_ref, up_s, *, ts):
    i = pl.program_id(1)

    @pl.when(i == 0)
    def _():
        up_s[0:SUBLANES] = jnp.zeros((SUBLANES, 2 * D_FF), F32)

    @pl.when(i > 0)
    def _():
        up_s[0:SUBLANES] = up_s[ts:ts + SUBLANES]

    h = h_ref[...]
    hb = h.astype(BF16)
    cw = 2 * FFN_CW
    for c in range(2 * D_FF // cw):
        up_s[SUBLANES:SUBLANES + ts, c * cw:(c + 1) * cw] = _bdot(hb, wup_ref[:, c * cw:(c + 1) * cw])
    tail_ref[0] = up_s[ts:ts + SUBLANES]

    def conv(col):
        sl = slice(col, col + FFN_CW)
        out = up_s[SUBLANES:SUBLANES + ts, sl] * dw_ref[FFN_K - 1:FFN_K, sl]
        for k in range(FFN_K - 1):
            off = SUBLANES - (FFN_K - 1) + k
            out = out + up_s[off:off + ts, sl] * dw_ref[k:k + 1, sl]
        return out

    _ffn_down(conv, h, wdown_ref, l2g_ref, l2b_ref, y_ref)


def _ffn_sample_kernel(h_ref, s0_ref, s1_ref, wup_ref, dw_ref, wdown_ref, l2g_ref, l2b_ref, y_ref, up_ref):
    h = h_ref[...]
    hb = h.astype(BF16)
    cw = 2 * FFN_CW
    for c in range(2 * D_FF // cw):
        up_ref[:, c * cw:(c + 1) * cw] = _bdot(hb, wup_ref[:, c * cw:(c + 1) * cw])

    def conv(col):
        sl = slice(col, col + FFN_CW)
        return s0_ref[:, sl] * dw_ref[0:1, sl] + s1_ref[:, sl] * dw_ref[1:2, sl] + up_ref[:, sl] * dw_ref[2:3, sl]

    _ffn_down(conv, h, wdown_ref, l2g_ref, l2b_ref, y_ref)


def _ffn_prompt(h2d, fw, *, n_seq, seq_len, ts):
    tps = seq_len // ts
    row = pl.BlockSpec((ts, D_MODEL), lambda n, i: (n * tps + i, 0))
    return pl.pallas_call(
        functools.partial(_ffn_prompt_kernel, ts=ts),
        grid=(n_seq, tps),
        in_specs=[row] + [_resident(a.shape) for a in fw],
        out_specs=[row, pl.BlockSpec((1, SUBLANES, 2 * D_FF), lambda n, i: (n, 0, 0))],
        out_shape=[jax.ShapeDtypeStruct((n_seq * seq_len, D_MODEL), F32),
                   jax.ShapeDtypeStruct((n_seq, SUBLANES, 2 * D_FF), F32)],
        scratch_shapes=[pltpu.VMEM((SUBLANES + ts, 2 * D_FF), F32)],
        compiler_params=_cparams(("arbitrary", "arbitrary")), name="ffn_prompt",
    )(h2d, *fw)


def _ffn_sample(h2d, s0, s1, fw):
    args = (h2d, s0, s1) + tuple(fw)
    full = lambda a: pl.BlockSpec(a.shape, lambda i, nd=a.ndim: (0,) * nd)
    return pl.pallas_call(
        _ffn_sample_kernel, grid=(1,),
        in_specs=[full(a) for a in args],
        out_specs=[full(h2d), full(s0)],
        out_shape=[jax.ShapeDtypeStruct(h2d.shape, F32), jax.ShapeDtypeStruct(s0.shape, F32)],
        compiler_params=_cparams(("arbitrary",)), name="ffn_sample",
    )(*args)


def _ffn_weights(ffn_w_up, ffn_dw_w, ffn_w_down, ln2_g, ln2_b):
    return (ffn_w_up.astype(BF16), ffn_dw_w, ffn_w_down.astype(BF16), ln2_g.reshape(1, -1), ln2_b.reshape(1, -1))


PAGES_PER_STEP = 8


def _attend_new(q16, q16f, kt, vt, mask, new_row, new_ok):
    s = [jnp.where(mk, _bdot(q16, k), NEG_INF) for k, mk in zip(kt, mask)]
    s_new = jnp.where(new_ok, jnp.sum(q16f * new_row[:, :KV_W], axis=-1, keepdims=True), NEG_INF)
    m = s_new
    for x in s:
        m = jnp.maximum(m, jnp.max(x, axis=-1, keepdims=True))
    e_new = jnp.exp2(s_new - m)
    l = e_new
    acc = e_new * new_row[:, KV_W:]
    for x, v in zip(s, vt):
        e = jnp.exp2(x - m)
        l = l + jnp.sum(e, axis=-1, keepdims=True)
        acc = acc + _dot_nt(e.astype(BF16), v)
    return acc * (1.0 / l)


def _sample_nsa_kernel(pt_ref, *refs, n_pages):
    pps = PAGES_PER_STEP
    cmp_refs, slc_refs = refs[:pps], refs[pps:2 * pps]
    (q_ref, gt_ref, nslc_ref, nwin_ref, swt_ref, wcat_ref, w2_ref, pe_ref, ovl_ref, oh_ref, perm_ref,
     o_ref, xbuf, ktb, vtb) = refs[2 * pps:]
    p = pl.program_id(1)
    hpp = PAGE_SIZE // D_CMP
    for k in range(pps):
        off = pl.multiple_of((p * pps + k) * hpp, hpp)
        tp = _bdot(cmp_refs[k][0].astype(BF16), perm_ref[...])
        for pair in range(2 * KV_W // LANES):
            rows = jnp.transpose(tp[pair * LANES:(pair + 1) * LANES, :])
            for l in range(D_CMP):
                xbuf[pair, l, pl.ds(off, hpp), :] = rows[l * hpp:(l + 1) * hpp]
        ktb[p, :, k * PAGE_SIZE:(k + 1) * PAGE_SIZE] = slc_refs[k][0, 0:KV_W, :].astype(BF16)
        vtb[p, :, k * PAGE_SIZE:(k + 1) * PAGE_SIZE] = slc_refs[k][0, KV_W:, :].astype(BF16)

    @pl.when(p == n_pages // pps - 1)
    def _():
        past = n_pages * PAGE_SIZE
        t = past
        nh = past // D_CMP
        nb_past = past // L_SLC
        q16 = q_ref[0]
        q16f = q16.astype(F32)
        low = lax.broadcasted_iota(jnp.int32, (nh, LANES), 1) < HEAD_DIM

        halves = []
        for pair in range(2 * KV_W // LANES):
            xl = lambda l, pair=pair: xbuf[pair, l]
            kv = pair // (KV_W // LANES)
            a, b = _compress_pair(xl, wcat_ref.at[kv], w2_ref.at[kv], pe_ref, nh)
            halves.append(jnp.where(low, a, pltpu.roll(b, HEAD_DIM, 1)).astype(BF16))
        kc = jnp.concatenate(halves[:2], axis=1)
        vc = jnp.concatenate(halves[2:], axis=1)

        c_end = lax.broadcasted_iota(jnp.int32, (1, nh), 1) * D_CMP + (L_CMP - 1)
        vis = c_end <= t
        pc = jnp.where(vis, _softmax(jnp.where(vis, _dot_nt(q16, kc), NEG_INF)), 0.0)
        o_c = _bdot(pc.astype(BF16), vc)

        imp16 = _importance(pc, ovl_ref[...])
        impg = imp16
        for h in range(1, HPG):
            impg = impg + pltpu.roll(imp16, N_HEADS - h, 0)
        sel = _select_blocks(impg, jnp.full((N_HEADS, 1), t // L_SLC, jnp.int32))
        first = lax.broadcasted_iota(jnp.int32, (N_HEADS, 1), 0) % HPG == 0
        sel4 = jnp.where(first, sel, 0.0)
        sel16 = sel4
        for h in range(1, HPG):
            sel16 = sel16 + pltpu.roll(sel4, h, 0)

        nchunk = n_pages // pps
        cw = pps * PAGE_SIZE
        selb = sel16[:, :nb_past].astype(BF16)
        masks = [_bdot(selb, oh_ref[:, c * cw:(c + 1) * cw]) > 0.5 for c in range(nchunk)]
        o_s = _attend_new(q16, q16f, [ktb[c] for c in range(nchunk)], [vtb[c] for c in range(nchunk)], masks,
                          nslc_ref[0][0:1], sel16[:, nb_past:nb_past + 1] > 0.5)

        swt = swt_ref[0]
        w_eff = swt.shape[1]
        kpos = past - w_eff + lax.broadcasted_iota(jnp.int32, (1, w_eff), 1)
        ok = (kpos > t - WINDOW) & (kpos >= 0)
        o_w = _attend_new(q16, q16f, [swt[:KV_W].astype(BF16)], [swt[KV_W:].astype(BF16)], [ok], nwin_ref[0][0:1], True)

        gt = gt_ref[0]
        o_ref[0] = gt[:, 0:1] * o_c + gt[:, 1:2] * o_s + gt[:, 2:3] * o_w


def _sample_nsa(page_table, cmp_t, slc_t, q16, gt, nslc, nwin, swt, cw):
    wcat, w2, pe8 = cw
    ns, n_pages = page_table.shape
    past = n_pages * PAGE_SIZE
    nbl = 2 * LANES
    nb_past = past // L_SLC
    ovl = _overlap_matrix(past // D_CMP, nbl)
    onehot = (jnp.arange(past)[None, :] // L_SLC == jnp.arange(nb_past)[:, None]).astype(BF16)
    pos = jnp.arange(PAGE_SIZE)
    perm = (jnp.arange(PAGE_SIZE)[None, :] == ((pos % D_CMP) * (PAGE_SIZE // D_CMP) + pos // D_CMP)[:, None]).astype(BF16)
    pps = PAGES_PER_STEP
    assert n_pages % pps == 0 and nb_past + 1 <= nbl
    pages = [pl.BlockSpec((1, 2 * KV_W, PAGE_SIZE), lambda n, p, pt, k=k: (pt[n, p * pps + k], 0, 0))
             for k in range(pps)]
    per_n = lambda a: pl.BlockSpec((1,) + a.shape[1:], lambda n, p, pt, nd=a.ndim: (n,) + (0,) * (nd - 1))
    const = lambda a: pl.BlockSpec(a.shape, lambda n, p, pt, nd=a.ndim: (0,) * nd)
    grid_spec = pltpu.PrefetchScalarGridSpec(
        num_scalar_prefetch=1, grid=(ns, n_pages // pps),
        in_specs=pages + pages + [per_n(q16), per_n(gt), per_n(nslc), per_n(nwin), per_n(swt),
                                  const(wcat), const(w2), const(pe8), const(ovl), const(onehot), const(perm)],
        out_specs=pl.BlockSpec((1, N_HEADS, KV_W), lambda n, p, pt: (n, 0, 0)),
        scratch_shapes=[pltpu.VMEM((2 * KV_W // LANES, D_CMP, past // D_CMP, LANES), F32),
                        pltpu.VMEM((n_pages // pps, KV_W, pps * PAGE_SIZE), BF16),
                        pltpu.VMEM((n_pages // pps, KV_W, pps * PAGE_SIZE), BF16)])
    return pl.pallas_call(
        functools.partial(_sample_nsa_kernel, n_pages=n_pages),
        grid_spec=grid_spec,
        out_shape=jax.ShapeDtypeStruct((ns, N_HEADS, KV_W), F32),
        compiler_params=_cparams(("arbitrary", "arbitrary")), name="sample_nsa",
    )(page_table, *([cmp_t] * pps), *([slc_t] * pps), q16, gt, nslc, nwin, swt, wcat, w2, pe8, ovl, onehot, perm)


def _prompt_tiles(seq_len):
    def fit(pref):
        while seq_len % pref:
            pref //= 2
        assert pref >= Q_BLOCK
        return pref
    tiles = {"proj": fit(256), "select": fit(256), "mix": fit(512), "ffn": fit(512), "attn_keys": fit(2048)}
    tiles["attn_tail"] = min(512, tiles["attn_keys"])
    return tiles


def kernel(x_prompt, x_sample, cache_cmp, cache_slc, state_win, state_conv, state_ffn, page_table, w_in, cmp_pe,
           cmp_k_w1, cmp_k_w2, cmp_v_w1, cmp_v_w2, conv_dw_w, conv_dw_b, conv_ln_g, conv_ln_b, conv_w_pw, nsa_w_o,
           w_out, ln1_g, ln1_b, ffn_w_up, ffn_dw_w, ffn_w_down, ln2_g, ln2_b):
    assert w_in.shape[0] == DEPTH == 1
    nb, s, _ = x_prompt.shape
    ns = x_sample.shape[0]
    n_pages = page_table.shape[1]
    past = n_pages * PAGE_SIZE
    n_pool = cache_cmp.shape[1]

    w_r = _reorder_w_in(w_in[0])
    cw = _compress_weights(cmp_pe[0], cmp_k_w1[0], cmp_k_w2[0], cmp_v_w1[0], cmp_v_w2[0])
    mw = _mix_weights(conv_dw_b[0], conv_ln_g[0], conv_ln_b[0], conv_w_pw[0], nsa_w_o[0], w_out[0], ln1_g[0], ln1_b[0])
    fw = _ffn_weights(ffn_w_up[0], ffn_dw_w[0], ffn_w_down[0], ln2_g[0], ln2_b[0])
    dww = conv_dw_w[0]

    t = _prompt_tiles(s)
    xp = x_prompt.reshape(nb * s, D_MODEL)
    u, q, cmp_p, _, win_p, gm, gb, kaug, vs, kw, vw, cmp_t, slc_t = _proj(
        xp, w_r, _rope_tables(jnp.arange(s)), tm=t["proj"], seq_len=s, n_seq=nb, emit_att=True)
    kcv = _compress_prompt(cmp_p, cw, n_seq=nb, seq_len=s)
    ocg, sb = _select_prompt(q, gb, kcv, n_seq=nb, seq_len=s, qs=t["select"])
    o = _attn_prompt(q, gb, ocg, sb, kaug, vs, kw, vw, n_seq=nb, seq_len=s, tk=t["attn_keys"], sub=t["attn_tail"])
    h = _mix_prompt(u, o, gm, xp, dww, mw, n_seq=nb, seq_len=s, ts=t["mix"])
    y_p, up_tail = _ffn_prompt(h, fw, n_seq=nb, seq_len=s, ts=t["ffn"])

    kv_shape = lambda n, t: (1, n, t, 2, N_KV, HEAD_DIM)
    w_keep = min(WINDOW, s)
    y_prompt = y_p.reshape(nb, s, D_MODEL)
    from_feat = lambda a: jnp.transpose(a.reshape(nb, 2, N_KV, HEAD_DIM, s), (0, 4, 1, 2, 3))[None]
    cmp_prompt = from_feat(cmp_t)
    slc_prompt = from_feat(slc_t)
    win_prompt = win_p.reshape(nb, s, 2 * KV_W)[:, s - w_keep:].reshape(kv_shape(nb, w_keep))
    conv_prompt = u.reshape(nb, s, C_CONV)[:, s - (CONV_K - 1):][None]
    ffn_prompt = up_tail[:, SUBLANES - (FFN_K - 1):][None]

    xs = x_sample.reshape(ns, D_MODEL)
    u_s, q_s, cmp_s, slc_s, win_s, gm_s, gb_s = _proj(
        xs, w_r, _rope_tables(jnp.full((ns,), past)), tm=ns, seq_len=ns, n_seq=1, emit_att=False)
    q4 = q_s.reshape(ns, N_KV, HPG, 1, HEAD_DIM) * jnp.eye(N_KV, dtype=BF16)[None, :, None, :, None]
    q16 = q4.reshape(ns, N_HEADS, KV_W)
    gt = jnp.transpose(gb_s.reshape(ns, 3, N_HEADS), (0, 2, 1))
    gt = jnp.pad(gt, ((0, 0), (0, 0), (0, LANES - 3)))
    pad_rows = lambda a: jnp.pad(a[:, None, :], ((0, 0), (0, SUBLANES - 1), (0, 0)))
    w_eff = state_win.shape[2]
    feat_major = lambda a: jnp.transpose(a, (0, 2, 3, 4, 1)).reshape(a.shape[0], 2 * KV_W, a.shape[1])
    o_s = _sample_nsa(page_table, feat_major(cache_cmp[0]), feat_major(cache_slc[0]), q16, gt, pad_rows(slc_s),
                      pad_rows(win_s), feat_major(state_win[0]), cw)
    o_s = o_s.reshape(ns, N_KV, HPG, N_KV, HEAD_DIM)
    o_s = jnp.stack([o_s[:, g, :, g] for g in range(N_KV)], axis=1).reshape(ns, N_HEADS * HEAD_DIM).astype(BF16)
    h_s = _mix_sample(u_s, jnp.transpose(state_conv[0], (1, 0, 2)), o_s, gm_s, xs, dww, mw)
    y_s, up_s = _ffn_sample(h_s, state_ffn[0][:, 0], state_ffn[0][:, 1], fw)

    y_sample = y_s.reshape(ns, 1, D_MODEL)
    cmp_sample = cmp_s.reshape(kv_shape(ns, 1))
    slc_sample = slc_s.reshape(kv_shape(ns, 1))
    win_all = jnp.concatenate([state_win[0], win_s.reshape(ns, 1, 2, N_KV, HEAD_DIM)], axis=1)
    win_sample = win_all[:, -w_eff:][None]
    conv_sample = jnp.concatenate([state_conv[0], u_s[:, None]], axis=1)[:, -(CONV_K - 1):][None]
    ffn_sample = jnp.concatenate([state_ffn[0], up_s[:, None]], axis=1)[:, -(FFN_K - 1):][None]
    return (y_prompt, y_sample, cmp_prompt, cmp_sample, slc_prompt, slc_sample, win_prompt, win_sample,
            conv_prompt, conv_sample, ffn_prompt, ffn_sample)
```

```python
import functools

import jax
import jax.numpy as jnp
from jax import lax
from jax.experimental import pallas as pl
from jax.experimental.pallas import tpu as pltpu

F32 = jnp.float32
BF16 = jnp.bfloat16

D_MODEL = 1024
N_HEADS = 16
HEAD_DIM = 64
N_KV = 4
HPG = N_HEADS // N_KV
KV_W = N_KV * HEAD_DIM
ROT_DIM = HEAD_DIM // 4
ROPE_THETA = 500000.0
L_CMP = 32
D_CMP = 16
L_SLC = 64
N_SEL = 16
N_LOCAL = 2
WINDOW = 512
CMP_HID = 4 * HEAD_DIM
Q_BLOCK = 128
C_CONV = D_MODEL
CONV_K = 31
D_FF = 2816
FFN_K = 3
DEPTH = 1
DN_ALPHA = (2 * DEPTH) ** 0.25
LN_EPS = 1e-5
NEG_INF = -1e30
PAGE_SIZE = 128
LOG2E = 1.4426950408889634

LANES = 128
SUBLANES = 8
VMEM_LIMIT = 56 * 1024 * 1024

O_A = 0
O_B = C_CONV
O_Q = 2 * C_CONV
O_KV = O_Q + N_HEADS * HEAD_DIM
O_GM = O_KV + 6 * KV_W
O_GB = O_GM + 2 * D_MODEL
N_GB = 3 * N_HEADS
N_PROJ = O_GB + LANES


def _cparams(sem):
    return pltpu.CompilerParams(dimension_semantics=sem, vmem_limit_bytes=VMEM_LIMIT)


def _resident(shape):
    nd = len(shape)
    return pl.BlockSpec(shape, lambda *_: (0,) * nd, pipeline_mode=pl.Buffered(1))


def _bdot(a, b):
    return jnp.dot(a, b, preferred_element_type=F32)


def _dot_nt(a, b):
    return lax.dot_general(a, b, (((1,), (1,)), ((), ())), preferred_element_type=F32)


def _rope_tables(pos):
    half = ROT_DIM // 2
    inv = ROPE_THETA ** (-jnp.arange(half, dtype=F32) * 2.0 / ROT_DIM)
    ang = pos.astype(F32)[:, None] * inv[None, :]
    cos, sin = jnp.cos(ang), jnp.sin(ang)
    r = pos.shape[0]
    one = jnp.ones((r, HEAD_DIM - ROT_DIM), F32)
    zero = jnp.zeros((r, HEAD_DIM - ROT_DIM), F32)
    zh = jnp.zeros((r, half), F32)
    c = jnp.concatenate([cos, cos, one], axis=1)
    sa = jnp.concatenate([-sin, zh, zero], axis=1)
    sb = jnp.concatenate([zh, sin, zero], axis=1)
    rep = LANES // HEAD_DIM
    return jnp.tile(c, (1, rep)), jnp.tile(sa, (1, rep)), jnp.tile(sb, (1, rep))


def _rope(x, c, sa, sb):
    return x * c + pltpu.roll(x, LANES - ROT_DIM // 2, 1) * sa + pltpu.roll(x, ROT_DIM // 2, 1) * sb


def _proj_kernel(x_ref, w_ref, c_ref, sa_ref, sb_ref,
                 u_ref, q_ref, cmp_ref, slc_ref, win_ref, gm_ref, gb_ref, *att_refs,
                 tm, tiles_per_seq):
    xb = x_ref[...].astype(BF16)
    c, sa, sb = c_ref[...], sa_ref[...], sb_ref[...]
    cw = 2 * LANES

    def z(col, width=cw):
        return _bdot(xb, w_ref[:, col:col + width])

    for j in range(C_CONV // cw):
        u_ref[:, j * cw:(j + 1) * cw] = z(O_A + j * cw) * jax.nn.sigmoid(z(O_B + j * cw))

    def roped(col):
        zz = z(col)
        return [_rope(zz[:, h * LANES:(h + 1) * LANES], c, sa, sb) for h in range(cw // LANES)]

    scale = HEAD_DIM ** -0.5 * LOG2E
    for j in range(N_HEADS * HEAD_DIM // cw):
        halves = roped(O_Q + j * cw)
        for h, v in enumerate(halves):
            q_ref[:, j * cw + h * LANES:j * cw + (h + 1) * LANES] = (v * scale).astype(BF16)

    if att_refs:
        kaug_ref, vs_ref, kw_ref, vw_ref = att_refs[:4]
        feat_refs = att_refs[4:]
        lane = lax.broadcasted_iota(jnp.int32, (tm, LANES), 1)
        row = lax.broadcasted_iota(jnp.int32, (tm, LANES), 0)
        pos0 = (pl.program_id(0) % tiles_per_seq) * tm
        onehot = jnp.where(lane == (pos0 + row) // L_SLC, 1.0, 0.0).astype(BF16)
        low = lane < HEAD_DIM

        def per_group(halves, fill=0.0):
            out = []
            for g in range(N_KV):
                v = halves[g // 2]
                if g % 2:
                    v = pltpu.roll(v, HEAD_DIM, 1)
                out.append(jnp.where(low, v, fill).astype(BF16))
            return out

        ones_col = jnp.where(lane == HEAD_DIM, 1.0, 0.0)

    for br, dst in enumerate((cmp_ref, slc_ref, win_ref)):
        kh = roped(O_KV + br * 2 * KV_W)
        zv = z(O_KV + br * 2 * KV_W + KV_W)
        vh = [zv[:, h * LANES:(h + 1) * LANES] for h in range(cw // LANES)]
        for h in range(cw // LANES):
            dst[:, h * LANES:(h + 1) * LANES] = kh[h]
            dst[:, KV_W + h * LANES:KV_W + (h + 1) * LANES] = vh[h]
        if att_refs and br < 2:
            for h in range(cw // LANES):
                feat_refs[br][0, h * LANES:(h + 1) * LANES, :] = jnp.transpose(kh[h])
                feat_refs[br][0, KV_W + h * LANES:KV_W + (h + 1) * LANES, :] = jnp.transpose(vh[h])
        if att_refs and br == 1:
            for g, (kg, vg) in enumerate(zip(per_group(kh), per_group(vh, ones_col))):
                kaug_ref[0, g, :, 0:LANES] = onehot
                kaug_ref[0, g, :, LANES:2 * LANES] = kg
                vs_ref[0, g] = vg
        if att_refs and br == 2:
            for g, (kg, vg) in enumerate(zip(per_group(kh), per_group(vh, ones_col))):
                kw_ref[0, g] = kg
                vw_ref[0, g] = vg

    for j in range(2 * D_MODEL // cw):
        gm_ref[:, j * cw:(j + 1) * cw] = jax.nn.sigmoid(z(O_GM + j * cw))
    gb_ref[...] = jax.nn.sigmoid(z(O_GB, LANES))[:, :N_GB]


def _proj(x2d, w_r, tabs, *, tm, seq_len, n_seq, emit_att):
    r = x2d.shape[0]
    tps = seq_len // tm
    grid = (r // tm,)
    row = lambda w: pl.BlockSpec((tm, w), lambda i: (i, 0))
    tab = pl.BlockSpec((tm, LANES), lambda i: (i % tps, 0))
    out_shape = [jax.ShapeDtypeStruct((r, C_CONV), F32), jax.ShapeDtypeStruct((r, N_HEADS * HEAD_DIM), BF16),
                 jax.ShapeDtypeStruct((r, 2 * KV_W), F32), jax.ShapeDtypeStruct((r, 2 * KV_W), F32),
                 jax.ShapeDtypeStruct((r, 2 * KV_W), F32), jax.ShapeDtypeStruct((r, 2 * D_MODEL), F32),
                 jax.ShapeDtypeStruct((r, N_GB), F32)]
    out_specs = [row(C_CONV), row(N_HEADS * HEAD_DIM), row(2 * KV_W), row(2 * KV_W), row(2 * KV_W),
                 row(2 * D_MODEL), row(N_GB)]
    if emit_att:
        for w in (2 * LANES, LANES, LANES, LANES):
            out_shape.append(jax.ShapeDtypeStruct((n_seq, N_KV, seq_len, w), BF16))
            out_specs.append(pl.BlockSpec((1, N_KV, tm, w), lambda i: (i // tps, 0, i % tps, 0)))
        for _ in range(2):
            out_shape.append(jax.ShapeDtypeStruct((n_seq, 2 * KV_W, seq_len), F32))
            out_specs.append(pl.BlockSpec((1, 2 * KV_W, tm), lambda i: (i // tps, 0, i % tps)))
    return pl.pallas_call(
        functools.partial(_proj_kernel, tm=tm, tiles_per_seq=tps),
        grid=grid,
        in_specs=[row(D_MODEL), _resident(w_r.shape), tab, tab, tab],
        out_specs=out_specs, out_shape=out_shape,
        compiler_params=_cparams(("arbitrary",)), name="proj",
    )(x2d, w_r, *tabs)


def _reorder_w_in(w_in):
    pad = jnp.zeros((D_MODEL, LANES - N_GB), w_in.dtype)
    o3 = 2 * C_CONV + N_HEADS * HEAD_DIM + 6 * KV_W
    return jnp.concatenate([w_in[:, :o3], w_in[:, o3 + N_GB:], w_in[:, o3:o3 + N_GB], pad], axis=1).astype(BF16)


def _compress_pair(xl, wcat_ref, w2_ref, pe_ref, nh):
    lane = lax.broadcasted_iota(jnp.int32, (nh, LANES), 1)
    low = lane < HEAD_DIM
    even, odd = [], []
    for l in range(0, D_CMP, 2):
        a, b = xl(l), xl(l + 1)
        even.append(jnp.where(low, a, pltpu.roll(b, HEAD_DIM, 1)).astype(BF16))
        odd.append(jnp.where(low, pltpu.roll(a, HEAD_DIM, 1), b).astype(BF16))
    bias2 = _bdot(pe_ref[...], wcat_ref[...])
    bias = bias2[0:1, :CMP_HID] + bias2[1:2, CMP_HID:]
    outs = []
    for x_parts in (even, odd):
        ab = _bdot(jnp.concatenate(x_parts, axis=1), wcat_ref[...])
        h = ab[:, :CMP_HID] + pltpu.roll(ab[:, CMP_HID:], nh - 1, 0) + bias
        outs.append(_bdot(jax.nn.gelu(h).astype(BF16), w2_ref[...]))
    return outs


def _compress_kernel(x_ref, wcat_ref, w2_ref, pe_ref, o_ref, *, nh):
    xl = lambda l: x_ref[0, pl.ds(l, nh, stride=D_CMP), :]
    for j, v in enumerate(_compress_pair(xl, wcat_ref.at[0], w2_ref.at[0], pe_ref, nh)):
        o_ref[0, j] = v.astype(BF16)


def _compress_weights(cmp_pe, k_w1, k_w2, v_w1, v_w2):
    def cat(w1):
        return jnp.concatenate([w1[:D_CMP].reshape(D_CMP * HEAD_DIM, CMP_HID),
                                w1[D_CMP:].reshape(D_CMP * HEAD_DIM, CMP_HID)], axis=1)
    wcat = jnp.stack([cat(k_w1), cat(v_w1)]).astype(BF16)
    pad = jnp.zeros((CMP_HID, LANES - HEAD_DIM), F32)
    w2 = jnp.stack([jnp.concatenate([k_w2, pad], 1), jnp.concatenate([v_w2, pad], 1)]).astype(BF16)
    pe2 = cmp_pe.reshape(2, D_CMP * HEAD_DIM)
    pe8 = jnp.concatenate([pe2, jnp.zeros((SUBLANES - 2, D_CMP * HEAD_DIM), F32)], 0).astype(BF16)
    return wcat, w2, pe8


def _compress_prompt(cmp2d, cw, *, n_seq, seq_len):
    wcat, w2, pe8 = cw
    nh = seq_len // D_CMP
    x3 = cmp2d.reshape(n_seq, seq_len, 2 * KV_W)
    npair = 2 * KV_W // LANES
    return pl.pallas_call(
        functools.partial(_compress_kernel, nh=nh),
        grid=(n_seq, npair),
        in_specs=[pl.BlockSpec((1, seq_len, LANES), lambda n, p: (n, 0, p)),
                  pl.BlockSpec((1,) + wcat.shape[1:], lambda n, p: (p // (npair // 2), 0, 0)),
                  pl.BlockSpec((1,) + w2.shape[1:], lambda n, p: (p // (npair // 2), 0, 0)),
                  pl.BlockSpec(pe8.shape, lambda n, p: (0, 0))],
        out_specs=pl.BlockSpec((1, 2, nh, LANES), lambda n, p: (n, p, 0, 0)),
        out_shape=jax.ShapeDtypeStruct((n_seq, 2 * N_KV, nh, LANES), BF16),
        compiler_params=_cparams(("arbitrary", "arbitrary")), name="compress_prompt",
    )(x3, wcat, w2, pe8)


BIG = 1e30


def _overlap_matrix(ncp, nb_lanes):
    start = jnp.arange(ncp) * D_CMP
    bstart = jnp.arange(nb_lanes) * L_SLC
    ov = (start[:, None] < bstart[None, :] + L_SLC) & (start[:, None] + L_CMP > bstart[None, :])
    return ov.astype(BF16)


def _split_heads(x):
    r = x.shape[0]
    low = lax.broadcasted_iota(jnp.int32, (r, LANES), 1) < HEAD_DIM
    out = []
    for h in range(HPG):
        v = x[:, (h // 2) * LANES:(h // 2 + 1) * LANES]
        if h % 2:
            v = pltpu.roll(v, HEAD_DIM, 1)
        out.append(jnp.where(low, v, 0.0))
    return out


def _merge_heads(parts):
    r = parts[0].shape[0]
    low = lax.broadcasted_iota(jnp.int32, (r, LANES), 1) < HEAD_DIM
    return [jnp.where(low, parts[2 * p], pltpu.roll(parts[2 * p + 1], HEAD_DIM, 1)) for p in range(HPG // 2)]


def _softmax(s):
    m = jnp.max(s, axis=-1, keepdims=True)
    e = jnp.exp2(s - m)
    return e * (1.0 / jnp.sum(e, axis=-1, keepdims=True))


def _select_blocks(imp, cur, axis=1):
    nbl = imp.shape[axis]
    blk = lax.broadcasted_iota(jnp.int32, imp.shape, axis)
    blkf = blk.astype(F32)
    forced = (blk == 0) | ((blk <= cur) & (blk > cur - N_LOCAL))
    v = jnp.where(forced | (blk > cur), -BIG, imp)
    sel = jnp.where(forced, 1.0, 0.0)
    for _ in range(N_SEL - 1 - N_LOCAL):
        m = jnp.max(v, axis=axis, keepdims=True)
        idx = jnp.min(jnp.where(v == m, blkf, float(nbl)), axis=axis, keepdims=True)
        hit = blkf == idx
        sel = jnp.where(hit & (v > -BIG), 1.0, sel)
        v = jnp.where(hit, -BIG, v)
    return sel


def _importance(psum, ovl):
    hi = psum.astype(BF16)
    lo = (psum - hi.astype(F32)).astype(BF16)
    return _bdot(hi, ovl) + _bdot(lo, ovl)


def _head_gates(gb, g, br, h):
    col = lax.broadcasted_iota(jnp.int32, gb.shape, 1)
    return jnp.sum(jnp.where(col == br * N_HEADS + g * HPG + h, gb, 0.0), axis=-1, keepdims=True)


def _select_kernel(q_ref, gb_ref, ovl_ref, kcv_ref, ocg_ref, sb_ref, *, qs, ncp):
    i = pl.program_id(1)
    tq = i * qs + lax.broadcasted_iota(jnp.int32, (qs, 1), 0)
    tq4 = jnp.concatenate([tq] * HPG, axis=0)
    gb = gb_ref[...]
    gw = HPG * HEAD_DIM

    def cmp_branch(w):
        def run():
            c_end = lax.broadcasted_iota(jnp.int32, (1, w), 1) * D_CMP + (L_CMP - 1)
            vis = c_end <= tq4
            imps = []
            for g in range(N_KV):
                qf = q_ref[:, g * gw:(g + 1) * gw].astype(F32)
                q4 = jnp.concatenate([v.astype(BF16) for v in _split_heads(qf)], axis=0)
                s = _dot_nt(q4, kcv_ref[0, g, 0:w, :])
                p = jnp.where(vis, _softmax(jnp.where(vis, s, NEG_INF)), 0.0)
                o_c = _bdot(p.astype(BF16), kcv_ref[0, N_KV + g, 0:w, :])
                psum = p[0:qs]
                for h in range(1, HPG):
                    psum = psum + p[h * qs:(h + 1) * qs]
                imps.append(_importance(psum, ovl_ref[0:w, :]))
                outs = [_head_gates(gb, g, 0, h) * o_c[h * qs:(h + 1) * qs] for h in range(HPG)]
                for pr, v in enumerate(_merge_heads(outs)):
                    ocg_ref[:, g * gw + pr * LANES:g * gw + (pr + 1) * LANES] = v
            return jnp.concatenate(imps, axis=0)
        return run

    cstep = LANES
    widths = [min((b + 1) * cstep, ncp) for b in range(-(-ncp // cstep))]
    bucket = jnp.minimum(((i + 1) * (qs // D_CMP) - 1) // cstep, len(widths) - 1)
    imp = lax.switch(bucket, [cmp_branch(w) for w in widths])
    nt = N_KV * qs // LANES
    imp_t = jnp.concatenate([jnp.transpose(imp[c * LANES:(c + 1) * LANES]) for c in range(nt)], axis=1)
    cur_t = (i * qs + lax.broadcasted_iota(jnp.int32, (1, N_KV * qs), 1) % qs) // L_SLC
    sel_t = _select_blocks(imp_t, cur_t, axis=0)
    for c in range(nt):
        sel = jnp.transpose(sel_t[:, c * LANES:(c + 1) * LANES])
        r0 = (c * LANES) % qs
        sb_ref[0, c * LANES // qs, r0:r0 + LANES, :] = jnp.where(sel > 0.0, 0.0, NEG_INF).astype(BF16)


def _select_prompt(q2d, gb2d, kcv, *, n_seq, seq_len, qs):
    ncp = seq_len // D_CMP
    nq = seq_len // qs
    ovl = _overlap_matrix(ncp, LANES)
    row = lambda w: pl.BlockSpec((qs, w), lambda n, i: (n * nq + i, 0))
    return pl.pallas_call(
        functools.partial(_select_kernel, qs=qs, ncp=ncp),
        grid=(n_seq, nq),
        in_specs=[row(N_HEADS * HEAD_DIM), row(N_GB), pl.BlockSpec(ovl.shape, lambda n, i: (0, 0)),
                  pl.BlockSpec((1,) + kcv.shape[1:], lambda n, i: (n, 0, 0, 0))],
        out_specs=[row(N_HEADS * HEAD_DIM), pl.BlockSpec((1, N_KV, qs, LANES), lambda n, i: (n, 0, i, 0))],
        out_shape=[jax.ShapeDtypeStruct((n_seq * seq_len, N_HEADS * HEAD_DIM), F32),
                   jax.ShapeDtypeStruct((n_seq, N_KV, seq_len, LANES), BF16)],
        compiler_params=_cparams(("arbitrary", "arbitrary")), name="select_prompt",
    )(q2d, gb2d, ovl, kcv)


def _attn_kernel(q_ref, gb_ref, ocg_ref, sb_ref, kaug_ref, vs_ref, kw_ref, vw_ref, o_ref, *, tk, sub):
    g = pl.program_id(1)
    i = pl.program_id(2)
    rows = HPG * Q_BLOCK
    q4 = jnp.concatenate([v.astype(BF16) for v in _split_heads(q_ref[...].astype(F32))], axis=0)
    tq = i * Q_BLOCK + lax.broadcasted_iota(jnp.int32, (Q_BLOCK, 1), 0)
    tq4 = jnp.concatenate([tq] * HPG, axis=0)
    q_aug = jnp.concatenate([jnp.concatenate([sb_ref[0, 0]] * HPG, axis=0), q4], axis=1)

    def tile(off, carry, width, causal):
        m, acc = carry
        s = _dot_nt(q_aug, kaug_ref[0, 0, pl.ds(off, width), :])
        if causal:
            kpos = off + lax.broadcasted_iota(jnp.int32, (1, width), 1)
            s = jnp.where(kpos <= tq4, s, NEG_INF)
        m_new = jnp.maximum(m, jnp.max(s, axis=-1, keepdims=True))
        e = jnp.exp2((s - m_new).astype(BF16))
        return m_new, jnp.exp2(m - m_new) * acc + _bdot(e, vs_ref[0, 0, pl.ds(off, width), :])

    init = (jnp.full((rows, 1), NEG_INF, F32), jnp.zeros((rows, LANES), F32))
    j_last = (i * Q_BLOCK) // tk
    carry = lax.fori_loop(0, j_last, lambda j, c: tile(pl.multiple_of(j * tk, tk), c, tk, False), init)
    base = pl.multiple_of(j_last * tk, tk)
    n_sub = (i * Q_BLOCK + Q_BLOCK - base + sub - 1) // sub
    tails = [functools.partial(tile, base, width=(w + 1) * sub, causal=True) for w in range(tk // sub)]
    _, acc = lax.switch(n_sub - 1, tails, carry)
    o_s = acc * (1.0 / acc[:, HEAD_DIM:HEAD_DIM + 1])

    wlen = WINDOW + Q_BLOCK
    start = pl.multiple_of(jnp.maximum(i - WINDOW // Q_BLOCK, 0) * Q_BLOCK, Q_BLOCK)
    s = _dot_nt(q4, kw_ref[0, 0, pl.ds(start, wlen), :])
    kpos = start + lax.broadcasted_iota(jnp.int32, (1, wlen), 1)

    def band_mask(x):
        return jnp.where((kpos <= tq4) & (kpos > tq4 - WINDOW), x, NEG_INF)

    def edge_mask(x):
        old = jnp.where(kpos[:, :Q_BLOCK] > tq4 - WINDOW, x[:, :Q_BLOCK], NEG_INF)
        new = jnp.where(kpos[:, WINDOW:] <= tq4, x[:, WINDOW:], NEG_INF)
        return jnp.concatenate([old, x[:, Q_BLOCK:WINDOW], new], axis=1)

    s = lax.cond(i >= WINDOW // Q_BLOCK, edge_mask, band_mask, s)
    e = jnp.exp2((s - jnp.max(s, axis=-1, keepdims=True)).astype(BF16))
    acc = _bdot(e, vw_ref[0, 0, pl.ds(start, wlen), :])
    o_w = acc * (1.0 / acc[:, HEAD_DIM:HEAD_DIM + 1])

    gb = gb_ref[...]
    outs = []
    for h in range(HPG):
        sl = slice(h * Q_BLOCK, (h + 1) * Q_BLOCK)
        outs.append(_head_gates(gb, g, 1, h) * o_s[sl] + _head_gates(gb, g, 2, h) * o_w[sl])
    for pr, v in enumerate(_merge_heads(outs)):
        o_ref[:, pr * LANES:(pr + 1) * LANES] = (ocg_ref[:, pr * LANES:(pr + 1) * LANES] + v).astype(BF16)


def _attn_prompt(q2d, gb2d, ocg, sb, kaug, vs, kw, vw, *, n_seq, seq_len, tk, sub):
    nq = seq_len // Q_BLOCK
    gw = HPG * HEAD_DIM
    per_ng = lambda a: pl.BlockSpec((1, 1) + a.shape[2:], lambda n, g, i: (n, g, 0, 0))
    qblk = pl.BlockSpec((Q_BLOCK, gw), lambda n, g, i: (n * nq + i, g))
    return pl.pallas_call(
        functools.partial(_attn_kernel, tk=tk, sub=sub),
        grid=(n_seq, N_KV, nq),
        in_specs=[qblk, pl.BlockSpec((Q_BLOCK, N_GB), lambda n, g, i: (n * nq + i, 0)), qblk,
                  pl.BlockSpec((1, 1, Q_BLOCK, LANES), lambda n, g, i: (n, g, i, 0)),
                  per_ng(kaug), per_ng(vs), per_ng(kw), per_ng(vw)],
        out_specs=qblk,
        out_shape=jax.ShapeDtypeStruct((n_seq * seq_len, N_HEADS * HEAD_DIM), BF16),
        compiler_params=_cparams(("arbitrary", "arbitrary", "arbitrary")), name="attn_prompt",
    )(q2d, gb2d, ocg, sb, kaug, vs, kw, vw)


def _layer_norm(x, g, b):
    mu = jnp.mean(x, axis=-1, keepdims=True)
    xc = x - mu
    var = jnp.mean(xc * xc, axis=-1, keepdims=True)
    return xc * lax.rsqrt(var + LN_EPS) * g + b


def _mix_tail(cv, o_ref, gm_ref, x_ref, w, h_ref):
    dwb, clg, clb, pw, wo, wout, l1g, l1b = w
    hc = _layer_norm(cv + dwb[...], clg[...], clb[...])
    act = hc * jax.nn.sigmoid(hc)
    yc = _bdot(act.astype(BF16), pw[...])
    yo = _bdot(o_ref[...], wo[...])
    merged = gm_ref[:, :D_MODEL] * yc + gm_ref[:, D_MODEL:] * yo
    hh = DN_ALPHA * x_ref[...] + _bdot(merged.astype(BF16), wout[...])
    h_ref[...] = _layer_norm(hh, l1g[...], l1b[...])


CONV_HALO = 32
CONV_ROWS = 32


def _mix_prompt_kernel(u_ref, halo_ref, o_ref, gm_ref, x_ref, dww_ref, *rest, ts):
    w, h_ref, win_s, cv_s = rest[:8], rest[8], rest[9], rest[10]
    i = pl.program_id(1)

    @pl.when(i == 0)
    def _():
        win_s[0, 0:CONV_HALO] = jnp.zeros((CONV_HALO, C_CONV), F32)

    @pl.when(i > 0)
    def _():
        win_s[0, 0:CONV_HALO] = halo_ref[0]

    win_s[0, CONV_HALO:CONV_HALO + ts] = u_ref[0]
    span = ts + CONV_HALO - SUBLANES
    for r in range(1, SUBLANES):
        win_s[r, 0:span] = win_s[0, r:r + span]
    first = CONV_HALO - (CONV_K - 1)

    for c in range(ts // CONV_ROWS):
        acc = jnp.zeros((CONV_ROWS, C_CONV), F32)
        for k in range(CONV_K):
            r, a = (first + k) % SUBLANES, (first + k) // SUBLANES
            base = c * CONV_ROWS + a * SUBLANES
            acc = acc + win_s[r, base:base + CONV_ROWS, :] * dww_ref[k:k + 1, :]
        cv_s[c * CONV_ROWS:(c + 1) * CONV_ROWS, :] = acc
    _mix_tail(cv_s[...], o_ref, gm_ref, x_ref, w, h_ref)


def _mix_sample_kernel(u_ref, st_ref, o_ref, gm_ref, x_ref, dww_ref, *rest):
    w, h_ref = rest[:8], rest[8]
    cv = u_ref[...] * dww_ref[CONV_K - 1:CONV_K, :]
    for k in range(CONV_K - 1):
        cv = cv + st_ref[k] * dww_ref[k:k + 1, :]
    _mix_tail(cv, o_ref, gm_ref, x_ref, w, h_ref)


def _mix_weights(conv_dw_b, conv_ln_g, conv_ln_b, conv_w_pw, nsa_w_o, w_out, ln1_g, ln1_b):
    r = lambda v: v.reshape(1, -1)
    return (r(conv_dw_b), r(conv_ln_g), r(conv_ln_b), conv_w_pw.astype(BF16), nsa_w_o.astype(BF16),
            w_out.astype(BF16), r(ln1_g), r(ln1_b))


def _mix_prompt(u2d, o2d, gm2d, x2d, dww, mw, *, n_seq, seq_len, ts):
    tps = seq_len // ts
    hpt = ts // CONV_HALO
    u3 = u2d.reshape(n_seq, seq_len, C_CONV)
    row = lambda w: pl.BlockSpec((ts, w), lambda n, i: (n * tps + i, 0))
    return pl.pallas_call(
        functools.partial(_mix_prompt_kernel, ts=ts),
        grid=(n_seq, tps),
        in_specs=[pl.BlockSpec((1, ts, C_CONV), lambda n, i: (n, i, 0)),
                  pl.BlockSpec((1, CONV_HALO, C_CONV), lambda n, i: (n, jnp.maximum(i * hpt - 1, 0), 0)),
                  row(N_HEADS * HEAD_DIM), row(2 * D_MODEL), row(D_MODEL), _resident(dww.shape)]
                 + [_resident(a.shape) for a in mw],
        out_specs=row(D_MODEL),
        out_shape=jax.ShapeDtypeStruct((n_seq * seq_len, D_MODEL), F32),
        scratch_shapes=[pltpu.VMEM((SUBLANES, CONV_HALO + ts, C_CONV), F32), pltpu.VMEM((ts, C_CONV), F32)],
        compiler_params=_cparams(("arbitrary", "arbitrary")), name="mix_prompt",
    )(u3, u3, o2d, gm2d, x2d, dww, *mw)


def _mix_sample(u2d, st_t, o2d, gm2d, x2d, dww, mw):
    args = (u2d, st_t, o2d, gm2d, x2d, dww) + tuple(mw)
    full = lambda a: pl.BlockSpec(a.shape, lambda i, nd=a.ndim: (0,) * nd)
    return pl.pallas_call(
        _mix_sample_kernel, grid=(1,),
        in_specs=[full(a) for a in args],
        out_specs=pl.BlockSpec(x2d.shape, lambda i: (0, 0)),
        out_shape=jax.ShapeDtypeStruct(x2d.shape, F32),
        compiler_params=_cparams(("arbitrary",)), name="mix_sample",
    )(*args)


FFN_CW = 256


def _ffn_down(conv, h, wdown_ref, l2g_ref, l2b_ref, y_ref):
    acc = jnp.zeros(h.shape, F32)
    for c in range(D_FF // FFN_CW):
        val, gate = conv(c * FFN_CW), conv(D_FF + c * FFN_CW)
        act = gate * jax.nn.sigmoid(gate) * val
        acc = acc + _bdot(act.astype(BF16), wdown_ref[c * FFN_CW:(c + 1) * FFN_CW, :])
    y_ref[...] = _layer_norm(DN_ALPHA * h + acc, l2g_ref[...], l2b_ref[...])


def _ffn_prompt_kernel(h_ref, wup_ref, dw_ref, wdown_ref, l2g_ref, l2b_ref, y_ref, tail_ref, up_s, *, ts):
    i = pl.program_id(1)

    @pl.when(i == 0)
    def _():
        up_s[0:SUBLANES] = jnp.zeros((SUBLANES, 2 * D_FF), F32)

    @pl.when(i > 0)
    def _():
        up_s[0:SUBLANES] = up_s[ts:ts + SUBLANES]

    h = h_ref[...]
    hb = h.astype(BF16)
    cw = 2 * FFN_CW
    for c in range(2 * D_FF // cw):
        up_s[SUBLANES:SUBLANES + ts, c * cw:(c + 1) * cw] = _bdot(hb, wup_ref[:, c * cw:(c + 1) * cw])
    tail_ref[0] = up_s[ts:ts + SUBLANES]

    def conv(col):
        sl = slice(col, col + FFN_CW)
        out = up_s[SUBLANES:SUBLANES + ts, sl] * dw_ref[FFN_K - 1:FFN_K, sl]
        for k in range(FFN_K - 1):
            off = SUBLANES - (FFN_K - 1) + k
            out = out + up_s[off:off + ts, sl] * dw_ref[k:k + 1, sl]
        return out

    _ffn_down(conv, h, wdown_ref, l2g_ref, l2b_ref, y_ref)


def _ffn_sample_kernel(h_ref, s0_ref, s1_ref, wup_ref, dw_ref, wdown_ref, l2g_ref, l2b_ref, y_ref, up_ref):
    h = h_ref[...]
    hb = h.astype(BF16)
    cw = 2 * FFN_CW
    for c in range(2 * D_FF // cw):
        up_ref[:, c * cw:(c + 1) * cw] = _bdot(hb, wup_ref[:, c * cw:(c + 1) * cw])

    def conv(col):
        sl = slice(col, col + FFN_CW)
        return s0_ref[:, sl] * dw_ref[0:1, sl] + s1_ref[:, sl] * dw_ref[1:2, sl] + up_ref[:, sl] * dw_ref[2:3, sl]

    _ffn_down(conv, h, wdown_ref, l2g_ref, l2b_ref, y_ref)


def _ffn_prompt(h2d, fw, *, n_seq, seq_len, ts):
    tps = seq_len // ts
    row = pl.BlockSpec((ts, D_MODEL), lambda n, i: (n * tps + i, 0))
    return pl.pallas_call(
        functools.partial(_ffn_prompt_kernel, ts=ts),
        grid=(n_seq, tps),
        in_specs=[row] + [_resident(a.shape) for a in fw],
        out_specs=[row, pl.BlockSpec((1, SUBLANES, 2 * D_FF), lambda n, i: (n, 0, 0))],
        out_shape=[jax.ShapeDtypeStruct((n_seq * seq_len, D_MODEL), F32),
                   jax.ShapeDtypeStruct((n_seq, SUBLANES, 2 * D_FF), F32)],
        scratch_shapes=[pltpu.VMEM((SUBLANES + ts, 2 * D_FF), F32)],
        compiler_params=_cparams(("arbitrary", "arbitrary")), name="ffn_prompt",
    )(h2d, *fw)


def _ffn_sample(h2d, s0, s1, fw):
    args = (h2d, s0, s1) + tuple(fw)
    full = lambda a: pl.BlockSpec(a.shape, lambda i, nd=a.ndim: (0,) * nd)
    return pl.pallas_call(
        _ffn_sample_kernel, grid=(1,),
        in_specs=[full(a) for a in args],
        out_specs=[full(h2d), full(s0)],
        out_shape=[jax.ShapeDtypeStruct(h2d.shape, F32), jax.ShapeDtypeStruct(s0.shape, F32)],
        compiler_params=_cparams(("arbitrary",)), name="ffn_sample",
    )(*args)


def _ffn_weights(ffn_w_up, ffn_dw_w, ffn_w_down, ln2_g, ln2_b):
    return (ffn_w_up.astype(BF16), ffn_dw_w, ffn_w_down.astype(BF16), ln2_g.reshape(1, -1), ln2_b.reshape(1, -1))


PAGES_PER_STEP = 16


def _attend_new(q16, q16f, kt, vt, mask, new_row, new_ok):
    s = [jnp.where(mk, _bdot(q16, k), NEG_INF) for k, mk in zip(kt, mask)]
    s_new = jnp.where(new_ok, jnp.sum(q16f * new_row[:, :KV_W], axis=-1, keepdims=True), NEG_INF)
    m = s_new
    for x in s:
        m = jnp.maximum(m, jnp.max(x, axis=-1, keepdims=True))
    e_new = jnp.exp2(s_new - m)
    l = e_new
    acc = e_new * new_row[:, KV_W:]
    for x, v in zip(s, vt):
        e = jnp.exp2(x - m)
        l = l + jnp.sum(e, axis=-1, keepdims=True)
        acc = acc + _dot_nt(e.astype(BF16), v)
    return acc * (1.0 / l)


def _sample_nsa_kernel(pt_ref, *refs, n_pages):
    pps = PAGES_PER_STEP
    cmp_refs, slc_refs = refs[:pps], refs[pps:2 * pps]
    (q_ref, gt_ref, nslc_ref, nwin_ref, swt_ref, wcat_ref, w2_ref, pe_ref, ovl_ref, oh_ref, perm_ref,
     o_ref, xbuf, ktb, vtb) = refs[2 * pps:]
    p = pl.program_id(1)
    hpp = PAGE_SIZE // D_CMP
    for k in range(pps):
        off = pl.multiple_of((p * pps + k) * hpp, hpp)
        tp = _bdot(cmp_refs[k][0].astype(BF16), perm_ref[...])
        for pair in range(2 * KV_W // LANES):
            rows = jnp.transpose(tp[pair * LANES:(pair + 1) * LANES, :])
            for l in range(D_CMP):
                xbuf[pair, l, pl.ds(off, hpp), :] = rows[l * hpp:(l + 1) * hpp]
        ktb[p, :, k * PAGE_SIZE:(k + 1) * PAGE_SIZE] = slc_refs[k][0, 0:KV_W, :].astype(BF16)
        vtb[p, :, k * PAGE_SIZE:(k + 1) * PAGE_SIZE] = slc_refs[k][0, KV_W:, :].astype(BF16)

    @pl.when(p == n_pages // pps - 1)
    def _():
        past = n_pages * PAGE_SIZE
        t = past
        nh = past // D_CMP
        nb_past = past // L_SLC
        q16 = q_ref[0]
        q16f = q16.astype(F32)
        low = lax.broadcasted_iota(jnp.int32, (nh, LANES), 1) < HEAD_DIM

        halves = []
        for pair in range(2 * KV_W // LANES):
            xl = lambda l, pair=pair: xbuf[pair, l]
            kv = pair // (KV_W // LANES)
            a, b = _compress_pair(xl, wcat_ref.at[kv], w2_ref.at[kv], pe_ref, nh)
            halves.append(jnp.where(low, a, pltpu.roll(b, HEAD_DIM, 1)).astype(BF16))
        kc = jnp.concatenate(halves[:2], axis=1)
        vc = jnp.concatenate(halves[2:], axis=1)

        c_end = lax.broadcasted_iota(jnp.int32, (1, nh), 1) * D_CMP + (L_CMP - 1)
        vis = c_end <= t
        pc = jnp.where(vis, _softmax(jnp.where(vis, _dot_nt(q16, kc), NEG_INF)), 0.0)
        o_c = _bdot(pc.astype(BF16), vc)

        imp16 = _importance(pc, ovl_ref[...])
        impg = imp16
        for h in range(1, HPG):
            impg = impg + pltpu.roll(imp16, N_HEADS - h, 0)
        sel = _select_blocks(impg, jnp.full((N_HEADS, 1), t // L_SLC, jnp.int32))
        first = lax.broadcasted_iota(jnp.int32, (N_HEADS, 1), 0) % HPG == 0
        sel4 = jnp.where(first, sel, 0.0)
        sel16 = sel4
        for h in range(1, HPG):
            sel16 = sel16 + pltpu.roll(sel4, h, 0)

        nchunk = n_pages // pps
        cw = pps * PAGE_SIZE
        selb = sel16[:, :nb_past].astype(BF16)
        masks = [_bdot(selb, oh_ref[:, c * cw:(c + 1) * cw]) > 0.5 for c in range(nchunk)]
        o_s = _attend_new(q16, q16f, [ktb[c] for c in range(nchunk)], [vtb[c] for c in range(nchunk)], masks,
                          nslc_ref[0][0:1], sel16[:, nb_past:nb_past + 1] > 0.5)

        swt = swt_ref[0]
        w_eff = swt.shape[1]
        kpos = past - w_eff + lax.broadcasted_iota(jnp.int32, (1, w_eff), 1)
        ok = (kpos > t - WINDOW) & (kpos >= 0)
        o_w = _attend_new(q16, q16f, [swt[:KV_W].astype(BF16)], [swt[KV_W:].astype(BF16)], [ok], nwin_ref[0][0:1], True)

        gt = gt_ref[0]
        o_ref[0] = gt[:, 0:1] * o_c + gt[:, 1:2] * o_s + gt[:, 2:3] * o_w


def _sample_nsa(page_table, cmp_t, slc_t, q16, gt, nslc, nwin, swt, cw):
    wcat, w2, pe8 = cw
    ns, n_pages = page_table.shape
    past = n_pages * PAGE_SIZE
    nbl = 2 * LANES
    nb_past = past // L_SLC
    ovl = _overlap_matrix(past // D_CMP, nbl)
    onehot = (jnp.arange(past)[None, :] // L_SLC == jnp.arange(nb_past)[:, None]).astype(BF16)
    pos = jnp.arange(PAGE_SIZE)
    perm = (jnp.arange(PAGE_SIZE)[None, :] == ((pos % D_CMP) * (PAGE_SIZE // D_CMP) + pos // D_CMP)[:, None]).astype(BF16)
    pps = PAGES_PER_STEP
    assert n_pages % pps == 0 and nb_past + 1 <= nbl
    pages = [pl.BlockSpec((1, 2 * KV_W, PAGE_SIZE), lambda n, p, pt, k=k: (pt[n, p * pps + k], 0, 0))
             for k in range(pps)]
    per_n = lambda a: pl.BlockSpec((1,) + a.shape[1:], lambda n, p, pt, nd=a.ndim: (n,) + (0,) * (nd - 1))
    const = lambda a: pl.BlockSpec(a.shape, lambda n, p, pt, nd=a.ndim: (0,) * nd)
    grid_spec = pltpu.PrefetchScalarGridSpec(
        num_scalar_prefetch=1, grid=(ns, n_pages // pps),
        in_specs=pages + pages + [per_n(q16), per_n(gt), per_n(nslc), per_n(nwin), per_n(swt),
                                  const(wcat), const(w2), const(pe8), const(ovl), const(onehot), const(perm)],
        out_specs=pl.BlockSpec((1, N_HEADS, KV_W), lambda n, p, pt: (n, 0, 0)),
        scratch_shapes=[pltpu.VMEM((2 * KV_W // LANES, D_CMP, past // D_CMP, LANES), F32),
                        pltpu.VMEM((n_pages // pps, KV_W, pps * PAGE_SIZE), BF16),
                        pltpu.VMEM((n_pages // pps, KV_W, pps * PAGE_SIZE), BF16)])
    return pl.pallas_call(
        functools.partial(_sample_nsa_kernel, n_pages=n_pages),
        grid_spec=grid_spec,
        out_shape=jax.ShapeDtypeStruct((ns, N_HEADS, KV_W), F32),
        compiler_params=_cparams(("arbitrary", "arbitrary")), name="sample_nsa",
    )(page_table, *([cmp_t] * pps), *([slc_t] * pps), q16, gt, nslc, nwin, swt, wcat, w2, pe8, ovl, onehot, perm)


def _prompt_tiles(seq_len):
    def fit(pref):
        while seq_len % pref:
            pref //= 2
        assert pref >= Q_BLOCK
        return pref
    tiles = {"proj": fit(256), "select": fit(256), "mix": fit(512), "ffn": fit(512), "attn_keys": fit(2048)}
    tiles["attn_tail"] = min(256, tiles["attn_keys"])
    return tiles


def kernel(x_prompt, x_sample, cache_cmp, cache_slc, state_win, state_conv, state_ffn, page_table, w_in, cmp_pe,
           cmp_k_w1, cmp_k_w2, cmp_v_w1, cmp_v_w2, conv_dw_w, conv_dw_b, conv_ln_g, conv_ln_b, conv_w_pw, nsa_w_o,
           w_out, ln1_g, ln1_b, ffn_w_up, ffn_dw_w, ffn_w_down, ln2_g, ln2_b):
    assert w_in.shape[0] == DEPTH == 1
    nb, s, _ = x_prompt.shape
    ns = x_sample.shape[0]
    n_pages = page_table.shape[1]
    past = n_pages * PAGE_SIZE
    n_pool = cache_cmp.shape[1]

    w_r = _reorder_w_in(w_in[0])
    cw = _compress_weights(cmp_pe[0], cmp_k_w1[0], cmp_k_w2[0], cmp_v_w1[0], cmp_v_w2[0])
    mw = _mix_weights(conv_dw_b[0], conv_ln_g[0], conv_ln_b[0], conv_w_pw[0], nsa_w_o[0], w_out[0], ln1_g[0], ln1_b[0])
    fw = _ffn_weights(ffn_w_up[0], ffn_dw_w[0], ffn_w_down[0], ln2_g[0], ln2_b[0])
    dww = conv_dw_w[0]

    t = _prompt_tiles(s)
    xp = x_prompt.reshape(nb * s, D_MODEL)
    u, q, cmp_p, _, win_p, gm, gb, kaug, vs, kw, vw, cmp_t, slc_t = _proj(
        xp, w_r, _rope_tables(jnp.arange(s)), tm=t["proj"], seq_len=s, n_seq=nb, emit_att=True)
    kcv = _compress_prompt(cmp_p, cw, n_seq=nb, seq_len=s)
    ocg, sb = _select_prompt(q, gb, kcv, n_seq=nb, seq_len=s, qs=t["select"])
    o = _attn_prompt(q, gb, ocg, sb, kaug, vs, kw, vw, n_seq=nb, seq_len=s, tk=t["attn_keys"], sub=t["attn_tail"])
    h = _mix_prompt(u, o, gm, xp, dww, mw, n_seq=nb, seq_len=s, ts=t["mix"])
    y_p, up_tail = _ffn_prompt(h, fw, n_seq=nb, seq_len=s, ts=t["ffn"])

    kv_shape = lambda n, t: (1, n, t, 2, N_KV, HEAD_DIM)
    w_keep = min(WINDOW, s)
    y_prompt = y_p.reshape(nb, s, D_MODEL)
    from_feat = lambda a: jnp.transpose(a.reshape(nb, 2, N_KV, HEAD_DIM, s), (0, 4, 1, 2, 3))[None]
    cmp_prompt = from_feat(cmp_t)
    slc_prompt = from_feat(slc_t)
    win_prompt = win_p.reshape(nb, s, 2 * KV_W)[:, s - w_keep:].reshape(kv_shape(nb, w_keep))
    conv_prompt = u.reshape(nb, s, C_CONV)[:, s - (CONV_K - 1):][None]
    ffn_prompt = up_tail[:, SUBLANES - (FFN_K - 1):][None]

    xs = x_sample.reshape(ns, D_MODEL)
    u_s, q_s, cmp_s, slc_s, win_s, gm_s, gb_s = _proj(
        xs, w_r, _rope_tables(jnp.full((ns,), past)), tm=ns, seq_len=ns, n_seq=1, emit_att=False)
    q4 = q_s.reshape(ns, N_KV, HPG, 1, HEAD_DIM) * jnp.eye(N_KV, dtype=BF16)[None, :, None, :, None]
    q16 = q4.reshape(ns, N_HEADS, KV_W)
    gt = jnp.transpose(gb_s.reshape(ns, 3, N_HEADS), (0, 2, 1))
    gt = jnp.pad(gt, ((0, 0), (0, 0), (0, LANES - 3)))
    pad_rows = lambda a: jnp.pad(a[:, None, :], ((0, 0), (0, SUBLANES - 1), (0, 0)))
    w_eff = state_win.shape[2]
    feat_major = lambda a: jnp.transpose(a, (0, 2, 3, 4, 1)).reshape(a.shape[0], 2 * KV_W, a.shape[1])
    o_s = _sample_nsa(page_table, feat_major(cache_cmp[0]), feat_major(cache_slc[0]), q16, gt, pad_rows(slc_s),
                      pad_rows(win_s), feat_major(state_win[0]), cw)
    o_s = o_s.reshape(ns, N_KV, HPG, N_KV, HEAD_DIM)
    o_s = jnp.stack([o_s[:, g, :, g] for g in range(N_KV)], axis=1).reshape(ns, N_HEADS * HEAD_DIM).astype(BF16)
    h_s = _mix_sample(u_s, jnp.transpose(state_conv[0], (1, 0, 2)), o_s, gm_s, xs, dww, mw)
    y_s, up_s = _ffn_sample(h_s, state_ffn[0][:, 0], state_ffn[0][:, 1], fw)

    y_sample = y_s.reshape(ns, 1, D_MODEL)
    cmp_sample = cmp_s.reshape(kv_shape(ns, 1))
    slc_sample = slc_s.reshape(kv_shape(ns, 1))
    win_all = jnp.concatenate([state_win[0], win_s.reshape(ns, 1, 2, N_KV, HEAD_DIM)], axis=1)
    win_sample = win_all[:, -w_eff:][None]
    conv_sample = jnp.concatenate([state_conv[0], u_s[:, None]], axis=1)[:, -(CONV_K - 1):][None]
    ffn_sample = jnp.concatenate([state_ffn[0], up_s[:, None]], axis=1)[:, -(FFN_K - 1):][None]
    return (y_prompt, y_sample, cmp_prompt, cmp_sample, slc_prompt, slc_sample, win_prompt, win_sample,
            conv_prompt, conv_sample, ffn_prompt, ffn_sample)
```

```python
import functools

import jax
import jax.numpy as jnp
from jax import lax
from jax.experimental import pallas as pl
from jax.experimental.pallas import tpu as pltpu

F32 = jnp.float32
BF16 = jnp.bfloat16

D_MODEL = 1024
N_HEADS = 16
HEAD_DIM = 64
N_KV = 4
HPG = N_HEADS // N_KV
KV_W = N_KV * HEAD_DIM
ROT_DIM = HEAD_DIM // 4
ROPE_THETA = 500000.0
L_CMP = 32
D_CMP = 16
L_SLC = 64
N_SEL = 16
N_LOCAL = 2
WINDOW = 512
CMP_HID = 4 * HEAD_DIM
Q_BLOCK = 128
C_CONV = D_MODEL
CONV_K = 31
D_FF = 2816
FFN_K = 3
DEPTH = 1
DN_ALPHA = (2 * DEPTH) ** 0.25
LN_EPS = 1e-5
NEG_INF = -1e30
PAGE_SIZE = 128
LOG2E = 1.4426950408889634

LANES = 128
SUBLANES = 8
VMEM_LIMIT = 56 * 1024 * 1024

O_A = 0
O_B = C_CONV
O_Q = 2 * C_CONV
O_KV = O_Q + N_HEADS * HEAD_DIM
O_GM = O_KV + 6 * KV_W
O_GB = O_GM + 2 * D_MODEL
N_GB = 3 * N_HEADS
N_PROJ = O_GB + LANES


def _cparams(sem):
    return pltpu.CompilerParams(dimension_semantics=sem, vmem_limit_bytes=VMEM_LIMIT)


def _resident(shape):
    nd = len(shape)
    return pl.BlockSpec(shape, lambda *_: (0,) * nd, pipeline_mode=pl.Buffered(1))


def _bdot(a, b):
    return jnp.dot(a, b, preferred_element_type=F32)


def _dot_nt(a, b):
    return lax.dot_general(a, b, (((1,), (1,)), ((), ())), preferred_element_type=F32)


def _rope_tables(pos):
    half = ROT_DIM // 2
    inv = ROPE_THETA ** (-jnp.arange(half, dtype=F32) * 2.0 / ROT_DIM)
    ang = pos.astype(F32)[:, None] * inv[None, :]
    cos, sin = jnp.cos(ang), jnp.sin(ang)
    r = pos.shape[0]
    one = jnp.ones((r, HEAD_DIM - ROT_DIM), F32)
    zero = jnp.zeros((r, HEAD_DIM - ROT_DIM), F32)
    zh = jnp.zeros((r, half), F32)
    c = jnp.concatenate([cos, cos, one], axis=1)
    sa = jnp.concatenate([-sin, zh, zero], axis=1)
    sb = jnp.concatenate([zh, sin, zero], axis=1)
    rep = LANES // HEAD_DIM
    return jnp.tile(c, (1, rep)), jnp.tile(sa, (1, rep)), jnp.tile(sb, (1, rep))


def _rope(x, c, sa, sb):
    return x * c + pltpu.roll(x, LANES - ROT_DIM // 2, 1) * sa + pltpu.roll(x, ROT_DIM // 2, 1) * sb


def _proj_kernel(x_ref, w_ref, c_ref, sa_ref, sb_ref,
                 u_ref, q_ref, cmp_ref, slc_ref, win_ref, gm_ref, gb_ref, *att_refs,
                 tm, tiles_per_seq):
    xb = x_ref[...].astype(BF16)
    c, sa, sb = c_ref[...], sa_ref[...], sb_ref[...]
    cw = 2 * LANES

    def z(col, width=cw):
        return _bdot(xb, w_ref[:, col:col + width])

    for j in range(C_CONV // cw):
        u_ref[:, j * cw:(j + 1) * cw] = z(O_A + j * cw) * jax.nn.sigmoid(z(O_B + j * cw))

    def roped(col):
        zz = z(col)
        return [_rope(zz[:, h * LANES:(h + 1) * LANES], c, sa, sb) for h in range(cw // LANES)]

    scale = HEAD_DIM ** -0.5 * LOG2E
    for j in range(N_HEADS * HEAD_DIM // cw):
        halves = roped(O_Q + j * cw)
        for h, v in enumerate(halves):
            q_ref[:, j * cw + h * LANES:j * cw + (h + 1) * LANES] = (v * scale).astype(BF16)

    if att_refs:
        kaug_ref, vs_ref, kw_ref, vw_ref = att_refs[:4]
        feat_refs = att_refs[4:]
        lane = lax.broadcasted_iota(jnp.int32, (tm, LANES), 1)
        row = lax.broadcasted_iota(jnp.int32, (tm, LANES), 0)
        pos0 = (pl.program_id(0) % tiles_per_seq) * tm
        onehot = jnp.where(lane == (pos0 + row) // L_SLC, 1.0, 0.0).astype(BF16)
        low = lane < HEAD_DIM

        def per_group(halves, fill=0.0):
            out = []
            for g in range(N_KV):
                v = halves[g // 2]
                if g % 2:
                    v = pltpu.roll(v, HEAD_DIM, 1)
                out.append(jnp.where(low, v, fill).astype(BF16))
            return out

        ones_col = jnp.where(lane == HEAD_DIM, 1.0, 0.0)

    for br, dst in enumerate((cmp_ref, slc_ref, win_ref)):
        kh = roped(O_KV + br * 2 * KV_W)
        zv = z(O_KV + br * 2 * KV_W + KV_W)
        vh = [zv[:, h * LANES:(h + 1) * LANES] for h in range(cw // LANES)]
        for h in range(cw // LANES):
            dst[:, h * LANES:(h + 1) * LANES] = kh[h]
            dst[:, KV_W + h * LANES:KV_W + (h + 1) * LANES] = vh[h]
        if att_refs and br < 2:
            for h in range(cw // LANES):
                feat_refs[br][0, h * LANES:(h + 1) * LANES, :] = jnp.transpose(kh[h])
                feat_refs[br][0, KV_W + h * LANES:KV_W + (h + 1) * LANES, :] = jnp.transpose(vh[h])
        if att_refs and br == 1:
            for g, (kg, vg) in enumerate(zip(per_group(kh), per_group(vh, ones_col))):
                kaug_ref[0, g, :, 0:LANES] = onehot
                kaug_ref[0, g, :, LANES:2 * LANES] = kg
                vs_ref[0, g] = vg
        if att_refs and br == 2:
            for g, (kg, vg) in enumerate(zip(per_group(kh), per_group(vh, ones_col))):
                kw_ref[0, g] = kg
                vw_ref[0, g] = vg

    for j in range(2 * D_MODEL // cw):
        gm_ref[:, j * cw:(j + 1) * cw] = jax.nn.sigmoid(z(O_GM + j * cw))
    gb_ref[...] = jax.nn.sigmoid(z(O_GB, LANES))[:, :N_GB]


def _proj(x2d, w_r, tabs, *, tm, seq_len, n_seq, emit_att):
    r = x2d.shape[0]
    tps = seq_len // tm
    grid = (r // tm,)
    row = lambda w: pl.BlockSpec((tm, w), lambda i: (i, 0))
    tab = pl.BlockSpec((tm, LANES), lambda i: (i % tps, 0))
    out_shape = [jax.ShapeDtypeStruct((r, C_CONV), F32), jax.ShapeDtypeStruct((r, N_HEADS * HEAD_DIM), BF16),
                 jax.ShapeDtypeStruct((r, 2 * KV_W), F32), jax.ShapeDtypeStruct((r, 2 * KV_W), F32),
                 jax.ShapeDtypeStruct((r, 2 * KV_W), F32), jax.ShapeDtypeStruct((r, 2 * D_MODEL), F32),
                 jax.ShapeDtypeStruct((r, N_GB), F32)]
    out_specs = [row(C_CONV), row(N_HEADS * HEAD_DIM), row(2 * KV_W), row(2 * KV_W), row(2 * KV_W),
                 row(2 * D_MODEL), row(N_GB)]
    if emit_att:
        for w in (2 * LANES, LANES, LANES, LANES):
            out_shape.append(jax.ShapeDtypeStruct((n_seq, N_KV, seq_len, w), BF16))
            out_specs.append(pl.BlockSpec((1, N_KV, tm, w), lambda i: (i // tps, 0, i % tps, 0)))
        for _ in range(2):
            out_shape.append(jax.ShapeDtypeStruct((n_seq, 2 * KV_W, seq_len), F32))
            out_specs.append(pl.BlockSpec((1, 2 * KV_W, tm), lambda i: (i // tps, 0, i % tps)))
    return pl.pallas_call(
        functools.partial(_proj_kernel, tm=tm, tiles_per_seq=tps),
        grid=grid,
        in_specs=[row(D_MODEL), _resident(w_r.shape), tab, tab, tab],
        out_specs=out_specs, out_shape=out_shape,
        compiler_params=_cparams(("arbitrary",)), name="proj",
    )(x2d, w_r, *tabs)


def _reorder_w_in(w_in):
    pad = jnp.zeros((D_MODEL, LANES - N_GB), w_in.dtype)
    o3 = 2 * C_CONV + N_HEADS * HEAD_DIM + 6 * KV_W
    return jnp.concatenate([w_in[:, :o3], w_in[:, o3 + N_GB:], w_in[:, o3:o3 + N_GB], pad], axis=1).astype(BF16)


def _compress_pair(xl, wcat_ref, w2_ref, pe_ref, nh):
    lane = lax.broadcasted_iota(jnp.int32, (nh, LANES), 1)
    low = lane < HEAD_DIM
    even, odd = [], []
    for l in range(0, D_CMP, 2):
        a, b = xl(l), xl(l + 1)
        even.append(jnp.where(low, a, pltpu.roll(b, HEAD_DIM, 1)).astype(BF16))
        odd.append(jnp.where(low, pltpu.roll(a, HEAD_DIM, 1), b).astype(BF16))
    bias2 = _bdot(pe_ref[...], wcat_ref[...])
    bias = bias2[0:1, :CMP_HID] + bias2[1:2, CMP_HID:]
    outs = []
    for x_parts in (even, odd):
        ab = _bdot(jnp.concatenate(x_parts, axis=1), wcat_ref[...])
        h = ab[:, :CMP_HID] + pltpu.roll(ab[:, CMP_HID:], nh - 1, 0) + bias
        outs.append(_bdot(jax.nn.gelu(h).astype(BF16), w2_ref[...]))
    return outs


def _compress_kernel(x_ref, wcat_ref, w2_ref, pe_ref, o_ref, *, nh):
    xl = lambda l: x_ref[0, pl.ds(l, nh, stride=D_CMP), :]
    for j, v in enumerate(_compress_pair(xl, wcat_ref.at[0], w2_ref.at[0], pe_ref, nh)):
        o_ref[0, j] = v.astype(BF16)


def _compress_weights(cmp_pe, k_w1, k_w2, v_w1, v_w2):
    def cat(w1):
        return jnp.concatenate([w1[:D_CMP].reshape(D_CMP * HEAD_DIM, CMP_HID),
                                w1[D_CMP:].reshape(D_CMP * HEAD_DIM, CMP_HID)], axis=1)
    wcat = jnp.stack([cat(k_w1), cat(v_w1)]).astype(BF16)
    pad = jnp.zeros((CMP_HID, LANES - HEAD_DIM), F32)
    w2 = jnp.stack([jnp.concatenate([k_w2, pad], 1), jnp.concatenate([v_w2, pad], 1)]).astype(BF16)
    pe2 = cmp_pe.reshape(2, D_CMP * HEAD_DIM)
    pe8 = jnp.concatenate([pe2, jnp.zeros((SUBLANES - 2, D_CMP * HEAD_DIM), F32)], 0).astype(BF16)
    return wcat, w2, pe8


def _compress_prompt(cmp2d, cw, *, n_seq, seq_len):
    wcat, w2, pe8 = cw
    nh = seq_len // D_CMP
    x3 = cmp2d.reshape(n_seq, seq_len, 2 * KV_W)
    npair = 2 * KV_W // LANES
    return pl.pallas_call(
        functools.partial(_compress_kernel, nh=nh),
        grid=(n_seq, npair),
        in_specs=[pl.BlockSpec((1, seq_len, LANES), lambda n, p: (n, 0, p)),
                  pl.BlockSpec((1,) + wcat.shape[1:], lambda n, p: (p // (npair // 2), 0, 0)),
                  pl.BlockSpec((1,) + w2.shape[1:], lambda n, p: (p // (npair // 2), 0, 0)),
                  pl.BlockSpec(pe8.shape, lambda n, p: (0, 0))],
        out_specs=pl.BlockSpec((1, 2, nh, LANES), lambda n, p: (n, p, 0, 0)),
        out_shape=jax.ShapeDtypeStruct((n_seq, 2 * N_KV, nh, LANES), BF16),
        compiler_params=_cparams(("arbitrary", "arbitrary")), name="compress_prompt",
    )(x3, wcat, w2, pe8)


BIG = 1e30


def _overlap_matrix(ncp, nb_lanes):
    start = jnp.arange(ncp) * D_CMP
    bstart = jnp.arange(nb_lanes) * L_SLC
    ov = (start[:, None] < bstart[None, :] + L_SLC) & (start[:, None] + L_CMP > bstart[None, :])
    return ov.astype(BF16)


def _split_heads(x):
    r = x.shape[0]
    low = lax.broadcasted_iota(jnp.int32, (r, LANES), 1) < HEAD_DIM
    out = []
    for h in range(HPG):
        v = x[:, (h // 2) * LANES:(h // 2 + 1) * LANES]
        if h % 2:
            v = pltpu.roll(v, HEAD_DIM, 1)
        out.append(jnp.where(low, v, 0.0))
    return out


def _merge_heads(parts):
    r = parts[0].shape[0]
    low = lax.broadcasted_iota(jnp.int32, (r, LANES), 1) < HEAD_DIM
    return [jnp.where(low, parts[2 * p], pltpu.roll(parts[2 * p + 1], HEAD_DIM, 1)) for p in range(HPG // 2)]


def _softmax(s):
    m = jnp.max(s, axis=-1, keepdims=True)
    e = jnp.exp2(s - m)
    return e * (1.0 / jnp.sum(e, axis=-1, keepdims=True))


def _select_blocks(imp, cur, axis=1):
    nbl = imp.shape[axis]
    blk = lax.broadcasted_iota(jnp.int32, imp.shape, axis)
    blkf = blk.astype(F32)
    forced = (blk == 0) | ((blk <= cur) & (blk > cur - N_LOCAL))
    v = jnp.where(forced | (blk > cur), -BIG, imp)
    sel = jnp.where(forced, 1.0, 0.0)
    for _ in range(N_SEL - 1 - N_LOCAL):
        m = jnp.max(v, axis=axis, keepdims=True)
        idx = jnp.min(jnp.where(v == m, blkf, float(nbl)), axis=axis, keepdims=True)
        hit = blkf == idx
        sel = jnp.where(hit & (v > -BIG), 1.0, sel)
        v = jnp.where(hit, -BIG, v)
    return sel


def _importance(psum, ovl):
    hi = psum.astype(BF16)
    lo = (psum - hi.astype(F32)).astype(BF16)
    return _bdot(hi, ovl) + _bdot(lo, ovl)


def _head_gates(gb, g, br, h):
    col = lax.broadcasted_iota(jnp.int32, gb.shape, 1)
    return jnp.sum(jnp.where(col == br * N_HEADS + g * HPG + h, gb, 0.0), axis=-1, keepdims=True)


def _select_kernel(q_ref, gb_ref, ovl_ref, kcv_ref, ocg_ref, sb_ref, *, qs, ncp):
    i = pl.program_id(1)
    tq = i * qs + lax.broadcasted_iota(jnp.int32, (qs, 1), 0)
    tq4 = jnp.concatenate([tq] * HPG, axis=0)
    gb = gb_ref[...]
    gw = HPG * HEAD_DIM

    def cmp_branch(w):
        def run():
            c_end = lax.broadcasted_iota(jnp.int32, (1, w), 1) * D_CMP + (L_CMP - 1)
            vis = c_end <= tq4
            imps = []
            for g in range(N_KV):
                qf = q_ref[:, g * gw:(g + 1) * gw].astype(F32)
                q4 = jnp.concatenate([v.astype(BF16) for v in _split_heads(qf)], axis=0)
                s = _dot_nt(q4, kcv_ref[0, g, 0:w, :])
                p = jnp.where(vis, _softmax(jnp.where(vis, s, NEG_INF)), 0.0)
                o_c = _bdot(p.astype(BF16), kcv_ref[0, N_KV + g, 0:w, :])
                psum = p[0:qs]
                for h in range(1, HPG):
                    psum = psum + p[h * qs:(h + 1) * qs]
                imps.append(_importance(psum, ovl_ref[0:w, :]))
                outs = [_head_gates(gb, g, 0, h) * o_c[h * qs:(h + 1) * qs] for h in range(HPG)]
                for pr, v in enumerate(_merge_heads(outs)):
                    ocg_ref[:, g * gw + pr * LANES:g * gw + (pr + 1) * LANES] = v
            return jnp.concatenate(imps, axis=0)
        return run

    cstep = LANES
    widths = [min((b + 1) * cstep, ncp) for b in range(-(-ncp // cstep))]
    bucket = jnp.minimum(((i + 1) * (qs // D_CMP) - 1) // cstep, len(widths) - 1)
    imp = lax.switch(bucket, [cmp_branch(w) for w in widths])
    nt = N_KV * qs // LANES
    imp_t = jnp.concatenate([jnp.transpose(imp[c * LANES:(c + 1) * LANES]) for c in range(nt)], axis=1)
    cur_t = (i * qs + lax.broadcasted_iota(jnp.int32, (1, N_KV * qs), 1) % qs) // L_SLC
    sel_t = _select_blocks(imp_t, cur_t, axis=0)
    for c in range(nt):
        sel = jnp.transpose(sel_t[:, c * LANES:(c + 1) * LANES])
        r0 = (c * LANES) % qs
        sb_ref[0, c * LANES // qs, r0:r0 + LANES, :] = jnp.where(sel > 0.0, 0.0, NEG_INF).astype(BF16)


def _select_prompt(q2d, gb2d, kcv, *, n_seq, seq_len, qs):
    ncp = seq_len // D_CMP
    nq = seq_len // qs
    ovl = _overlap_matrix(ncp, LANES)
    row = lambda w: pl.BlockSpec((qs, w), lambda n, i: (n * nq + i, 0))
    return pl.pallas_call(
        functools.partial(_select_kernel, qs=qs, ncp=ncp),
        grid=(n_seq, nq),
        in_specs=[row(N_HEADS * HEAD_DIM), row(N_GB), pl.BlockSpec(ovl.shape, lambda n, i: (0, 0)),
                  pl.BlockSpec((1,) + kcv.shape[1:], lambda n, i: (n, 0, 0, 0))],
        out_specs=[row(N_HEADS * HEAD_DIM), pl.BlockSpec((1, N_KV, qs, LANES), lambda n, i: (n, 0, i, 0))],
        out_shape=[jax.ShapeDtypeStruct((n_seq * seq_len, N_HEADS * HEAD_DIM), F32),
                   jax.ShapeDtypeStruct((n_seq, N_KV, seq_len, LANES), BF16)],
        compiler_params=_cparams(("arbitrary", "arbitrary")), name="select_prompt",
    )(q2d, gb2d, ovl, kcv)


def _attn_kernel(q_ref, gb_ref, ocg_ref, sb_ref, kaug_ref, vs_ref, kw_ref, vw_ref, o_ref, *, tk, sub):
    g = pl.program_id(1)
    i = pl.program_id(2)
    rows = HPG * Q_BLOCK
    q4 = jnp.concatenate([v.astype(BF16) for v in _split_heads(q_ref[...].astype(F32))], axis=0)
    tq = i * Q_BLOCK + lax.broadcasted_iota(jnp.int32, (Q_BLOCK, 1), 0)
    tq4 = jnp.concatenate([tq] * HPG, axis=0)
    q_aug = jnp.concatenate([jnp.concatenate([sb_ref[0, 0]] * HPG, axis=0), q4], axis=1)

    def tile(off, carry, width, causal):
        m, acc = carry
        s = _dot_nt(q_aug, kaug_ref[0, 0, pl.ds(off, width), :])
        if causal:
            kpos = off + lax.broadcasted_iota(jnp.int32, (1, width), 1)
            s = jnp.where(kpos <= tq4, s, NEG_INF)
        m_new = jnp.maximum(m, jnp.max(s, axis=-1, keepdims=True))
        e = jnp.exp2((s - m_new).astype(BF16))
        return m_new, jnp.exp2(m - m_new) * acc + _bdot(e, vs_ref[0, 0, pl.ds(off, width), :])

    init = (jnp.full((rows, 1), NEG_INF, F32), jnp.zeros((rows, LANES), F32))
    j_last = (i * Q_BLOCK) // tk
    carry = lax.fori_loop(0, j_last, lambda j, c: tile(pl.multiple_of(j * tk, tk), c, tk, False), init)
    base = pl.multiple_of(j_last * tk, tk)
    n_sub = (i * Q_BLOCK + Q_BLOCK - base + sub - 1) // sub
    tails = [functools.partial(tile, base, width=(w + 1) * sub, causal=True) for w in range(tk // sub)]
    _, acc = lax.switch(n_sub - 1, tails, carry)
    o_s = acc * (1.0 / acc[:, HEAD_DIM:HEAD_DIM + 1])

    wlen = WINDOW + Q_BLOCK
    start = pl.multiple_of(jnp.maximum(i - WINDOW // Q_BLOCK, 0) * Q_BLOCK, Q_BLOCK)
    s = _dot_nt(q4, kw_ref[0, 0, pl.ds(start, wlen), :])
    kpos = start + lax.broadcasted_iota(jnp.int32, (1, wlen), 1)
    s = jnp.where((kpos <= tq4) & (kpos > tq4 - WINDOW), s, NEG_INF)
    e = jnp.exp2((s - jnp.max(s, axis=-1, keepdims=True)).astype(BF16))
    acc = _bdot(e, vw_ref[0, 0, pl.ds(start, wlen), :])
    o_w = acc * (1.0 / acc[:, HEAD_DIM:HEAD_DIM + 1])

    gb = gb_ref[...]
    outs = []
    for h in range(HPG):
        sl = slice(h * Q_BLOCK, (h + 1) * Q_BLOCK)
        outs.append(_head_gates(gb, g, 1, h) * o_s[sl] + _head_gates(gb, g, 2, h) * o_w[sl])
    for pr, v in enumerate(_merge_heads(outs)):
        o_ref[:, pr * LANES:(pr + 1) * LANES] = (ocg_ref[:, pr * LANES:(pr + 1) * LANES] + v).astype(BF16)


def _attn_prompt(q2d, gb2d, ocg, sb, kaug, vs, kw, vw, *, n_seq, seq_len, tk, sub):
    nq = seq_len // Q_BLOCK
    gw = HPG * HEAD_DIM
    per_ng = lambda a: pl.BlockSpec((1, 1) + a.shape[2:], lambda n, g, i: (n, g, 0, 0))
    qblk = pl.BlockSpec((Q_BLOCK, gw), lambda n, g, i: (n * nq + i, g))
    return pl.pallas_call(
        functools.partial(_attn_kernel, tk=tk, sub=sub),
        grid=(n_seq, N_KV, nq),
        in_specs=[qblk, pl.BlockSpec((Q_BLOCK, N_GB), lambda n, g, i: (n * nq + i, 0)), qblk,
                  pl.BlockSpec((1, 1, Q_BLOCK, LANES), lambda n, g, i: (n, g, i, 0)),
                  per_ng(kaug), per_ng(vs), per_ng(kw), per_ng(vw)],
        out_specs=qblk,
        out_shape=jax.ShapeDtypeStruct((n_seq * seq_len, N_HEADS * HEAD_DIM), BF16),
        compiler_params=_cparams(("arbitrary", "arbitrary", "arbitrary")), name="attn_prompt",
    )(q2d, gb2d, ocg, sb, kaug, vs, kw, vw)


def _layer_norm(x, g, b):
    mu = jnp.mean(x, axis=-1, keepdims=True)
    xc = x - mu
    var = jnp.mean(xc * xc, axis=-1, keepdims=True)
    return xc * lax.rsqrt(var + LN_EPS) * g + b


def _mix_tail(cv, o_ref, gm_ref, x_ref, w, h_ref):
    dwb, clg, clb, pw, wo, wout, l1g, l1b = w
    hc = _layer_norm(cv + dwb[...], clg[...], clb[...])
    act = hc * jax.nn.sigmoid(hc)
    yc = _bdot(act.astype(BF16), pw[...])
    yo = _bdot(o_ref[...], wo[...])
    merged = gm_ref[:, :D_MODEL] * yc + gm_ref[:, D_MODEL:] * yo
    hh = DN_ALPHA * x_ref[...] + _bdot(merged.astype(BF16), wout[...])
    h_ref[...] = _layer_norm(hh, l1g[...], l1b[...])


CONV_HALO = 32
CONV_ROWS = 32


def _mix_prompt_kernel(u_ref, halo_ref, o_ref, gm_ref, x_ref, dww_ref, *rest, ts):
    w, h_ref, win_s, cv_s = rest[:8], rest[8], rest[9], rest[10]
    i = pl.program_id(1)

    @pl.when(i == 0)
    def _():
        win_s[0, 0:CONV_HALO] = jnp.zeros((CONV_HALO, C_CONV), F32)

    @pl.when(i > 0)
    def _():
        win_s[0, 0:CONV_HALO] = halo_ref[0]

    win_s[0, CONV_HALO:CONV_HALO + ts] = u_ref[0]
    span = ts + CONV_HALO - SUBLANES
    for r in range(1, SUBLANES):
        win_s[r, 0:span] = win_s[0, r:r + span]
    first = CONV_HALO - (CONV_K - 1)

    for c in range(ts // CONV_ROWS):
        acc = jnp.zeros((CONV_ROWS, C_CONV), F32)
        for k in range(CONV_K):
            r, a = (first + k) % SUBLANES, (first + k) // SUBLANES
            base = c * CONV_ROWS + a * SUBLANES
            acc = acc + win_s[r, base:base + CONV_ROWS, :] * dww_ref[k:k + 1, :]
        cv_s[c * CONV_ROWS:(c + 1) * CONV_ROWS, :] = acc
    _mix_tail(cv_s[...], o_ref, gm_ref, x_ref, w, h_ref)


def _mix_sample_kernel(u_ref, st_ref, o_ref, gm_ref, x_ref, dww_ref, *rest):
    w, h_ref = rest[:8], rest[8]
    cv = u_ref[...] * dww_ref[CONV_K - 1:CONV_K, :]
    for k in range(CONV_K - 1):
        cv = cv + st_ref[k] * dww_ref[k:k + 1, :]
    _mix_tail(cv, o_ref, gm_ref, x_ref, w, h_ref)


def _mix_weights(conv_dw_b, conv_ln_g, conv_ln_b, conv_w_pw, nsa_w_o, w_out, ln1_g, ln1_b):
    r = lambda v: v.reshape(1, -1)
    return (r(conv_dw_b), r(conv_ln_g), r(conv_ln_b), conv_w_pw.astype(BF16), nsa_w_o.astype(BF16),
            w_out.astype(BF16), r(ln1_g), r(ln1_b))


def _mix_prompt(u2d, o2d, gm2d, x2d, dww, mw, *, n_seq, seq_len, ts):
    tps = seq_len // ts
    hpt = ts // CONV_HALO
    u3 = u2d.reshape(n_seq, seq_len, C_CONV)
    row = lambda w: pl.BlockSpec((ts, w), lambda n, i: (n * tps + i, 0))
    return pl.pallas_call(
        functools.partial(_mix_prompt_kernel, ts=ts),
        grid=(n_seq, tps),
        in_specs=[pl.BlockSpec((1, ts, C_CONV), lambda n, i: (n, i, 0)),
                  pl.BlockSpec((1, CONV_HALO, C_CONV), lambda n, i: (n, jnp.maximum(i * hpt - 1, 0), 0)),
                  row(N_HEADS * HEAD_DIM), row(2 * D_MODEL), row(D_MODEL), _resident(dww.shape)]
                 + [_resident(a.shape) for a in mw],
        out_specs=row(D_MODEL),
        out_shape=jax.ShapeDtypeStruct((n_seq * seq_len, D_MODEL), F32),
        scratch_shapes=[pltpu.VMEM((SUBLANES, CONV_HALO + ts, C_CONV), F32), pltpu.VMEM((ts, C_CONV), F32)],
        compiler_params=_cparams(("arbitrary", "arbitrary")), name="mix_prompt",
    )(u3, u3, o2d, gm2d, x2d, dww, *mw)


def _mix_sample(u2d, st_t, o2d, gm2d, x2d, dww, mw):
    args = (u2d, st_t, o2d, gm2d, x2d, dww) + tuple(mw)
    full = lambda a: pl.BlockSpec(a.shape, lambda i, nd=a.ndim: (0,) * nd)
    return pl.pallas_call(
        _mix_sample_kernel, grid=(1,),
        in_specs=[full(a) for a in args],
        out_specs=pl.BlockSpec(x2d.shape, lambda i: (0, 0)),
        out_shape=jax.ShapeDtypeStruct(x2d.shape, F32),
        compiler_params=_cparams(("arbitrary",)), name="mix_sample",
    )(*args)


FFN_CW = 256


def _ffn_down(conv, h, wdown_ref, l2g_ref, l2b_ref, y_ref):
    acc = jnp.zeros(h.shape, F32)
    for c in range(D_FF // FFN_CW):
        val, gate = conv(c * FFN_CW), conv(D_FF + c * FFN_CW)
        act = gate * jax.nn.sigmoid(gate) * val
        acc = acc + _bdot(act.astype(BF16), wdown_ref[c * FFN_CW:(c + 1) * FFN_CW, :])
    y_ref[...] = _layer_norm(DN_ALPHA * h + acc, l2g_ref[...], l2b_ref[...])


def _ffn_prompt_kernel(h_ref, wup_ref, dw_ref, wdown_ref, l2g_ref, l2b_ref, y_ref, tail_ref, up_s, *, ts):
    i = pl.program_id(1)

    @pl.when(i == 0)
    def _():
        up_s[0:SUBLANES] = jnp.zeros((SUBLANES, 2 * D_FF), F32)

    @pl.when(i > 0)
    def _():
        up_s[0:SUBLANES] = up_s[ts:ts + SUBLANES]

    h = h_ref[...]
    hb = h.astype(BF16)
    cw = 2 * FFN_CW
    for c in range(2 * D_FF // cw):
        up_s[SUBLANES:SUBLANES + ts, c * cw:(c + 1) * cw] = _bdot(hb, wup_ref[:, c * cw:(c + 1) * cw])
    tail_ref[0] = up_s[ts:ts + SUBLANES]

    def conv(col):
        sl = slice(col, col + FFN_CW)
        out = up_s[SUBLANES:SUBLANES + ts, sl] * dw_ref[FFN_K - 1:FFN_K, sl]
        for k in range(FFN_K - 1):
            off = SUBLANES - (FFN_K - 1) + k
            out = out + up_s[off:off + ts, sl] * dw_ref[k:k + 1, sl]
        return out

    _ffn_down(conv, h, wdown_ref, l2g_ref, l2b_ref, y_ref)


def _ffn_sample_kernel(h_ref, s0_ref, s1_ref, wup_ref, dw_ref, wdown_ref, l2g_ref, l2b_ref, y_ref, up_ref):
    h = h_ref[...]
    hb = h.astype(BF16)
    cw = 2 * FFN_CW
    for c in range(2 * D_FF // cw):
        up_ref[:, c * cw:(c + 1) * cw] = _bdot(hb, wup_ref[:, c * cw:(c + 1) * cw])

    def conv(col):
        sl = slice(col, col + FFN_CW)
        return s0_ref[:, sl] * dw_ref[0:1, sl] + s1_ref[:, sl] * dw_ref[1:2, sl] + up_ref[:, sl] * dw_ref[2:3, sl]

    _ffn_down(conv, h, wdown_ref, l2g_ref, l2b_ref, y_ref)


def _ffn_prompt(h2d, fw, *, n_seq, seq_len, ts):
    tps = seq_len // ts
    row = pl.BlockSpec((ts, D_MODEL), lambda n, i: (n * tps + i, 0))
    return pl.pallas_call(
        functools.partial(_ffn_prompt_kernel, ts=ts),
        grid=(n_seq, tps),
        in_specs=[row] + [_resident(a.shape) for a in fw],
        out_specs=[row, pl.BlockSpec((1, SUBLANES, 2 * D_FF), lambda n, i: (n, 0, 0))],
        out_shape=[jax.ShapeDtypeStruct((n_seq * seq_len, D_MODEL), F32),
                   jax.ShapeDtypeStruct((n_seq, SUBLANES, 2 * D_FF), F32)],
        scratch_shapes=[pltpu.VMEM((SUBLANES + ts, 2 * D_FF), F32)],
        compiler_params=_cparams(("arbitrary", "arbitrary")), name="ffn_prompt",
    )(h2d, *fw)


def _ffn_sample(h2d, s0, s1, fw):
    args = (h2d, s0, s1) + tuple(fw)
    full = lambda a: pl.BlockSpec(a.shape, lambda i, nd=a.ndim: (0,) * nd)
    return pl.pallas_call(
        _ffn_sample_kernel, grid=(1,),
        in_specs=[full(a) for a in args],
        out_specs=[full(h2d), full(s0)],
        out_shape=[jax.ShapeDtypeStruct(h2d.shape, F32), jax.ShapeDtypeStruct(s0.shape, F32)],
        compiler_params=_cparams(("arbitrary",)), name="ffn_sample",
    )(*args)


def _ffn_weights(ffn_w_up, ffn_dw_w, ffn_w_down, ln2_g, ln2_b):
    return (ffn_w_up.astype(BF16), ffn_dw_w, ffn_w_down.astype(BF16), ln2_g.reshape(1, -1), ln2_b.reshape(1, -1))


PAGES_PER_STEP = 16


def _attend_new(q16, q16f, kt, vt, mask, new_row, new_ok):
    s = [jnp.where(mk, _bdot(q16, k), NEG_INF) for k, mk in zip(kt, mask)]
    s_new = jnp.where(new_ok, jnp.sum(q16f * new_row[:, :KV_W], axis=-1, keepdims=True), NEG_INF)
    m = s_new
    for x in s:
        m = jnp.maximum(m, jnp.max(x, axis=-1, keepdims=True))
    e_new = jnp.exp2(s_new - m)
    l = e_new
    acc = e_new * new_row[:, KV_W:]
    for x, v in zip(s, vt):
        e = jnp.exp2(x - m)
        l = l + jnp.sum(e, axis=-1, keepdims=True)
        acc = acc + _dot_nt(e.astype(BF16), v)
    return acc * (1.0 / l)


def _sample_nsa_kernel(pt_ref, *refs, n_pages):
    pps = PAGES_PER_STEP
    cmp_refs, slc_refs = refs[:pps], refs[pps:2 * pps]
    (q_ref, gt_ref, nslc_ref, nwin_ref, swt_ref, wcat_ref, w2_ref, pe_ref, ovl_ref, oh_ref, perm_ref,
     o_ref, xbuf, ktb, vtb) = refs[2 * pps:]
    p = pl.program_id(1)
    hpp = PAGE_SIZE // D_CMP
    for k in range(pps):
        off = pl.multiple_of((p * pps + k) * hpp, hpp)
        tp = _bdot(cmp_refs[k][0].astype(BF16), perm_ref[...])
        for pair in range(2 * KV_W // LANES):
            rows = jnp.transpose(tp[pair * LANES:(pair + 1) * LANES, :])
            for l in range(D_CMP):
                xbuf[pair, l, pl.ds(off, hpp), :] = rows[l * hpp:(l + 1) * hpp]
        ktb[p, :, k * PAGE_SIZE:(k + 1) * PAGE_SIZE] = slc_refs[k][0, 0:KV_W, :].astype(BF16)
        vtb[p, :, k * PAGE_SIZE:(k + 1) * PAGE_SIZE] = slc_refs[k][0, KV_W:, :].astype(BF16)

    @pl.when(p == n_pages // pps - 1)
    def _():
        past = n_pages * PAGE_SIZE
        t = past
        nh = past // D_CMP
        nb_past = past // L_SLC
        q16 = q_ref[0]
        q16f = q16.astype(F32)
        low = lax.broadcasted_iota(jnp.int32, (nh, LANES), 1) < HEAD_DIM

        halves = []
        for pair in range(2 * KV_W // LANES):
            xl = lambda l, pair=pair: xbuf[pair, l]
            kv = pair // (KV_W // LANES)
            a, b = _compress_pair(xl, wcat_ref.at[kv], w2_ref.at[kv], pe_ref, nh)
            halves.append(jnp.where(low, a, pltpu.roll(b, HEAD_DIM, 1)).astype(BF16))
        kc = jnp.concatenate(halves[:2], axis=1)
        vc = jnp.concatenate(halves[2:], axis=1)

        c_end = lax.broadcasted_iota(jnp.int32, (1, nh), 1) * D_CMP + (L_CMP - 1)
        vis = c_end <= t
        pc = jnp.where(vis, _softmax(jnp.where(vis, _dot_nt(q16, kc), NEG_INF)), 0.0)
        o_c = _bdot(pc.astype(BF16), vc)

        imp16 = _importance(pc, ovl_ref[...])
        impg = imp16
        for h in range(1, HPG):
            impg = impg + pltpu.roll(imp16, N_HEADS - h, 0)
        sel = _select_blocks(impg, jnp.full((N_HEADS, 1), t // L_SLC, jnp.int32))
        first = lax.broadcasted_iota(jnp.int32, (N_HEADS, 1), 0) % HPG == 0
        sel4 = jnp.where(first, sel, 0.0)
        sel16 = sel4
        for h in range(1, HPG):
            sel16 = sel16 + pltpu.roll(sel4, h, 0)

        nchunk = n_pages // pps
        cw = pps * PAGE_SIZE
        selb = sel16[:, :nb_past].astype(BF16)
        masks = [_bdot(selb, oh_ref[:, c * cw:(c + 1) * cw]) > 0.5 for c in range(nchunk)]
        o_s = _attend_new(q16, q16f, [ktb[c] for c in range(nchunk)], [vtb[c] for c in range(nchunk)], masks,
                          nslc_ref[0][0:1], sel16[:, nb_past:nb_past + 1] > 0.5)

        swt = swt_ref[0]
        w_eff = swt.shape[1]
        kpos = past - w_eff + lax.broadcasted_iota(jnp.int32, (1, w_eff), 1)
        ok = (kpos > t - WINDOW) & (kpos >= 0)
        o_w = _attend_new(q16, q16f, [swt[:KV_W].astype(BF16)], [swt[KV_W:].astype(BF16)], [ok], nwin_ref[0][0:1], True)

        gt = gt_ref[0]
        o_ref[0] = gt[:, 0:1] * o_c + gt[:, 1:2] * o_s + gt[:, 2:3] * o_w


def _sample_nsa(page_table, cmp_t, slc_t, q16, gt, nslc, nwin, swt, cw):
    wcat, w2, pe8 = cw
    ns, n_pages = page_table.shape
    past = n_pages * PAGE_SIZE
    nbl = 2 * LANES
    nb_past = past // L_SLC
    ovl = _overlap_matrix(past // D_CMP, nbl)
    onehot = (jnp.arange(past)[None, :] // L_SLC == jnp.arange(nb_past)[:, None]).astype(BF16)
    pos = jnp.arange(PAGE_SIZE)
    perm = (jnp.arange(PAGE_SIZE)[None, :] == ((pos % D_CMP) * (PAGE_SIZE // D_CMP) + pos // D_CMP)[:, None]).astype(BF16)
    pps = PAGES_PER_STEP
    assert n_pages % pps == 0 and nb_past + 1 <= nbl
    pages = [pl.BlockSpec((1, 2 * KV_W, PAGE_SIZE), lambda n, p, pt, k=k: (pt[n, p * pps + k], 0, 0))
             for k in range(pps)]
    per_n = lambda a: pl.BlockSpec((1,) + a.shape[1:], lambda n, p, pt, nd=a.ndim: (n,) + (0,) * (nd - 1))
    const = lambda a: pl.BlockSpec(a.shape, lambda n, p, pt, nd=a.ndim: (0,) * nd)
    grid_spec = pltpu.PrefetchScalarGridSpec(
        num_scalar_prefetch=1, grid=(ns, n_pages // pps),
        in_specs=pages + pages + [per_n(q16), per_n(gt), per_n(nslc), per_n(nwin), per_n(swt),
                                  const(wcat), const(w2), const(pe8), const(ovl), const(onehot), const(perm)],
        out_specs=pl.BlockSpec((1, N_HEADS, KV_W), lambda n, p, pt: (n, 0, 0)),
        scratch_shapes=[pltpu.VMEM((2 * KV_W // LANES, D_CMP, past // D_CMP, LANES), F32),
                        pltpu.VMEM((n_pages // pps, KV_W, pps * PAGE_SIZE), BF16),
                        pltpu.VMEM((n_pages // pps, KV_W, pps * PAGE_SIZE), BF16)])
    return pl.pallas_call(
        functools.partial(_sample_nsa_kernel, n_pages=n_pages),
        grid_spec=grid_spec,
        out_shape=jax.ShapeDtypeStruct((ns, N_HEADS, KV_W), F32),
        compiler_params=_cparams(("arbitrary", "arbitrary")), name="sample_nsa",
    )(page_table, *([cmp_t] * pps), *([slc_t] * pps), q16, gt, nslc, nwin, swt, wcat, w2, pe8, ovl, onehot, perm)


def _prompt_tiles(seq_len):
    def fit(pref):
        while seq_len % pref:
            pref //= 2
        assert pref >= Q_BLOCK
        return pref
    tiles = {"proj": fit(256), "select": fit(256), "mix": fit(512), "ffn": fit(512), "attn_keys": fit(2048)}
    tiles["attn_tail"] = min(512, tiles["attn_keys"])
    return tiles


def kernel(x_prompt, x_sample, cache_cmp, cache_slc, state_win, state_conv, state_ffn, page_table, w_in, cmp_pe,
           cmp_k_w1, cmp_k_w2, cmp_v_w1, cmp_v_w2, conv_dw_w, conv_dw_b, conv_ln_g, conv_ln_b, conv_w_pw, nsa_w_o,
           w_out, ln1_g, ln1_b, ffn_w_up, ffn_dw_w, ffn_w_down, ln2_g, ln2_b):
    assert w_in.shape[0] == DEPTH == 1
    nb, s, _ = x_prompt.shape
    ns = x_sample.shape[0]
    n_pages = page_table.shape[1]
    past = n_pages * PAGE_SIZE
    n_pool = cache_cmp.shape[1]

    w_r = _reorder_w_in(w_in[0])
    cw = _compress_weights(cmp_pe[0], cmp_k_w1[0], cmp_k_w2[0], cmp_v_w1[0], cmp_v_w2[0])
    mw = _mix_weights(conv_dw_b[0], conv_ln_g[0], conv_ln_b[0], conv_w_pw[0], nsa_w_o[0], w_out[0], ln1_g[0], ln1_b[0])
    fw = _ffn_weights(ffn_w_up[0], ffn_dw_w[0], ffn_w_down[0], ln2_g[0], ln2_b[0])
    dww = conv_dw_w[0]

    t = _prompt_tiles(s)
    xp = x_prompt.reshape(nb * s, D_MODEL)
    u, q, cmp_p, _, win_p, gm, gb, kaug, vs, kw, vw, cmp_t, slc_t = _proj(
        xp, w_r, _rope_tables(jnp.arange(s)), tm=t["proj"], seq_len=s, n_seq=nb, emit_att=True)
    kcv = _compress_prompt(cmp_p, cw, n_seq=nb, seq_len=s)
    ocg, sb = _select_prompt(q, gb, kcv, n_seq=nb, seq_len=s, qs=t["select"])
    o = _attn_prompt(q, gb, ocg, sb, kaug, vs, kw, vw, n_seq=nb, seq_len=s, tk=t["attn_keys"], sub=t["attn_tail"])
    h = _mix_prompt(u, o, gm, xp, dww, mw, n_seq=nb, seq_len=s, ts=t["mix"])
    y_p, up_tail = _ffn_prompt(h, fw, n_seq=nb, seq_len=s, ts=t["ffn"])

    kv_shape = lambda n, t: (1, n, t, 2, N_KV, HEAD_DIM)
    w_keep = min(WINDOW, s)
    y_prompt = y_p.reshape(nb, s, D_MODEL)
    from_feat = lambda a: jnp.transpose(a.reshape(nb, 2, N_KV, HEAD_DIM, s), (0, 4, 1, 2, 3))[None]
    cmp_prompt = from_feat(cmp_t)
    slc_prompt = from_feat(slc_t)
    win_prompt = win_p.reshape(nb, s, 2 * KV_W)[:, s - w_keep:].reshape(kv_shape(nb, w_keep))
    conv_prompt = u.reshape(nb, s, C_CONV)[:, s - (CONV_K - 1):][None]
    ffn_prompt = up_tail[:, SUBLANES - (FFN_K - 1):][None]

    xs = x_sample.reshape(ns, D_MODEL)
    u_s, q_s, cmp_s, slc_s, win_s, gm_s, gb_s = _proj(
        xs, w_r, _rope_tables(jnp.full((ns,), past)), tm=ns, seq_len=ns, n_seq=1, emit_att=False)
    q4 = q_s.reshape(ns, N_KV, HPG, 1, HEAD_DIM) * jnp.eye(N_KV, dtype=BF16)[None, :, None, :, None]
    q16 = q4.reshape(ns, N_HEADS, KV_W)
    gt = jnp.transpose(gb_s.reshape(ns, 3, N_HEADS), (0, 2, 1))
    gt = jnp.pad(gt, ((0, 0), (0, 0), (0, LANES - 3)))
    pad_rows = lambda a: jnp.pad(a[:, None, :], ((0, 0), (0, SUBLANES - 1), (0, 0)))
    w_eff = state_win.shape[2]
    feat_major = lambda a: jnp.transpose(a, (0, 2, 3, 4, 1)).reshape(a.shape[0], 2 * KV_W, a.shape[1])
    o_s = _sample_nsa(page_table, feat_major(cache_cmp[0]), feat_major(cache_slc[0]), q16, gt, pad_rows(slc_s),
                      pad_rows(win_s), feat_major(state_win[0]), cw)
    o_s = o_s.reshape(ns, N_KV, HPG, N_KV, HEAD_DIM)
    o_s = jnp.stack([o_s[:, g, :, g] for g in range(N_KV)], axis=1).reshape(ns, N_HEADS * HEAD_DIM).astype(BF16)
    h_s = _mix_sample(u_s, jnp.transpose(state_conv[0], (1, 0, 2)), o_s, gm_s, xs, dww, mw)
    y_s, up_s = _ffn_sample(h_s, state_ffn[0][:, 0], state_ffn[0][:, 1], fw)

    y_sample = y_s.reshape(ns, 1, D_MODEL)
    cmp_sample = cmp_s.reshape(kv_shape(ns, 1))
    slc_sample = slc_s.reshape(kv_shape(ns, 1))
    win_all = jnp.concatenate([state_win[0], win_s.reshape(ns, 1, 2, N_KV, HEAD_DIM)], axis=1)
    win_sample = win_all[:, -w_eff:][None]
    conv_sample = jnp.concatenate([state_conv[0], u_s[:, None]], axis=1)[:, -(CONV_K - 1):][None]
    ffn_sample = jnp.concatenate([state_ffn[0], up_s[:, None]], axis=1)[:, -(FFN_K - 1):][None]
    return (y_prompt, y_sample, cmp_prompt, cmp_sample, slc_prompt, slc_sample, win_prompt, win_sample,
            conv_prompt, conv_sample, ffn_prompt, ffn_sample)
```
